```python
import math
import jax, jax.numpy as jnp
from jax import lax
import numpy as np

D_MODEL = 2048
BATCH = 4
SEQ = 2048
DEPTH = 4
DEC_BATCH = 128
DEC_SEQ = 1
PAST_LEN = 16384
PAGE_SIZE = 128

N_MIXERS = 3
N_LAYERS_SSM = (DEPTH + 2) // 3
N_LAYERS_DN = (DEPTH + 1) // 3
N_LAYERS_RET = DEPTH // 3

CHUNK = 64
CONV_W = 4
NORM_EPS = 1e-6

SSM_D_INNER = 2 * D_MODEL
SSM_HEAD_DIM = 64
SSM_HEADS = SSM_D_INNER // SSM_HEAD_DIM
SSM_GROUPS = 8
SSM_HPG = SSM_HEADS // SSM_GROUPS
SSM_D_STATE = 128
SSM_CONV_DIM = SSM_D_INNER + 2 * SSM_GROUPS * SSM_D_STATE
SSM_IN_DIM = SSM_D_INNER + SSM_CONV_DIM + SSM_HEADS

DN_K_HEADS = D_MODEL // 128
DN_V_HEADS = 2 * DN_K_HEADS
DN_HEAD_K = 128
DN_HEAD_V = 128
DN_QK_DIM = DN_K_HEADS * DN_HEAD_K
DN_V_DIM = DN_V_HEADS * DN_HEAD_V
DN_CONV_DIM = 2 * DN_QK_DIM + DN_V_DIM
DN_IN_DIM = DN_CONV_DIM + DN_V_DIM + 2 * DN_V_HEADS

RET_HEADS = D_MODEL // 256
RET_DK = 256
RET_DV = 512
RET_QK_DIM = RET_HEADS * RET_DK
RET_V_DIM = RET_HEADS * RET_DV
RET_IN_DIM = 2 * RET_QK_DIM + 2 * RET_V_DIM
RET_ROPE_BASE = 10000.0

D_FF = ((8 * D_MODEL + 767) // 768) * 256
PLE_DIM = 256

kernel_name = "hybrid_ssd_deltanet_retention_step"

F32 = jnp.float32


def _rms_norm(x, gain=None):
    x32 = x.astype(F32)
    y = x32 * lax.rsqrt(jnp.mean(x32 * x32, axis=-1, keepdims=True) + NORM_EPS)
    if gain is not None:
        y = y * gain.astype(F32)
    return y.astype(x.dtype)


def _l2norm(x):
    x32 = x.astype(F32)
    return x32 * lax.rsqrt(jnp.sum(x32 * x32, axis=-1, keepdims=True) + NORM_EPS)


def _causal_conv(u, buf, w):
    L = u.shape[1]
    full = jnp.concatenate([buf.astype(u.dtype), u], axis=1)
    out = full[:, 0:L] * w[0]
    for t in range(1, CONV_W):
        out = out + full[:, t:t + L] * w[t]
    return out, full[:, L:]


def _to_chunks(t, q):
    b, l = t.shape[0], t.shape[1]
    return jnp.moveaxis(t.reshape((b, l // q, q) + t.shape[2:]), 1, 0)


def _from_chunks(t):
    t = jnp.moveaxis(t, 0, 1)
    return t.reshape((t.shape[0], t.shape[1] * t.shape[2]) + t.shape[3:])


def _decayed_linear_scan(q, k, v, log_a, s0):
    L = q.shape[1]
    Q = math.gcd(L, CHUNK)
    causal = jnp.tril(jnp.ones((Q, Q), dtype=bool))

    def step(s, inp):
        qc, kc, vc, ac = inp
        acs = jnp.cumsum(ac, axis=1)
        seg = acs[:, :, None] - acs[:, None, :]
        decay = jnp.exp(jnp.where(causal[None, :, :, None, None], seg, -jnp.inf))
        scores = jnp.einsum('bign,bjgn->bijg', qc, kc)
        y = jnp.einsum('bijg,bijgr,bjgrp->bigrp', scores, decay, vc)
        y = y + jnp.einsum('bign,bgrnp->bigrp', qc, s) * jnp.exp(acs)[..., None]
        last = acs[:, -1]
        w = jnp.exp(last[:, None] - acs)
        s = s * jnp.exp(last)[..., None, None] + jnp.einsum('bjgn,bjgr,bjgrp->bgrnp', kc, w, vc)
        return s, y

    xs = tuple(_to_chunks(t.astype(F32), Q) for t in (q, k, v, log_a))
    s, ys = lax.scan(step, s0.astype(F32), xs)
    return _from_chunks(ys), s


def _gated_delta_scan(q, k, v, g, beta, s0):
    L = q.shape[1]
    V = v.shape[-1]
    Q = math.gcd(L, CHUNK)
    incl = jnp.tril(jnp.ones((Q, Q), dtype=bool))
    strict = jnp.tril(jnp.ones((Q, Q), dtype=bool), -1)
    eye = jnp.eye(Q, dtype=F32)

    def step(s, inp):
        qc, kc, vc, gc, bc = inp
        acs = jnp.cumsum(gc, axis=1)
        acs_h = jnp.moveaxis(acs, 1, 2)
        seg = acs_h[..., :, None] - acs_h[..., None, :]
        gam = jnp.exp(jnp.where(incl, seg, -jnp.inf))
        b_h = jnp.moveaxis(bc, 1, 2)
        kk = jnp.einsum('bihd,bjhd->bhij', kc, kc)
        m = eye + jnp.where(strict, kk * gam * b_h[..., :, None], 0.0)
        rhs = jnp.concatenate([
            jnp.einsum('bjh,bjhv->bhjv', bc, vc),
            jnp.einsum('bjh,bjhd->bhjd', bc * jnp.exp(acs), kc)], axis=-1)
        sol = lax.linalg.triangular_solve(m, rhs, left_side=True, lower=True, unit_diagonal=True)
        u = sol[..., :V] - jnp.einsum('bhjd,bhdv->bhjv', sol[..., V:], s)
        qk = jnp.einsum('bihd,bjhd->bhij', qc, kc) * gam
        o = jnp.einsum('bhij,bhjv->bihv', qk, u) + jnp.einsum('bihd,bhdv->bihv', qc, s) * jnp.exp(acs)[..., None]
        last = acs_h[..., -1]
        w = jnp.exp(last[..., None] - acs_h)
        s = s * jnp.exp(last)[..., None, None] + jnp.einsum('bjhd,bhj,bhjv->bhdv', kc, w, u)
        return s, o

    xs = tuple(_to_chunks(t.astype(F32), Q) for t in (q, k, v, g, beta))
    s, os_ = lax.scan(step, s0.astype(F32), xs)
    return _from_chunks(os_), s


def _mamba2(u, s0, c0, w_in, conv_w, conv_b, dt_bias, a_log, d_skip, norm_w, w_out):
    B, L, _ = u.shape
    G, R, N, P = SSM_GROUPS, SSM_HPG, SSM_D_STATE, SSM_HEAD_DIM
    proj = u @ w_in
    z = proj[..., :SSM_D_INNER]
    xbc = proj[..., SSM_D_INNER:SSM_D_INNER + SSM_CONV_DIM]
    dt = proj[..., SSM_D_INNER + SSM_CONV_DIM:]
    xbc, c_new = _causal_conv(xbc, c0, conv_w)
    xbc = jax.nn.silu(xbc + conv_b)
    xs = xbc[..., :SSM_D_INNER].reshape(B, L, G, R, P)
    b_in = xbc[..., SSM_D_INNER:SSM_D_INNER + G * N].reshape(B, L, G, N)
    c_out = xbc[..., SSM_D_INNER + G * N:].reshape(B, L, G, N)
    dt = jax.nn.softplus(dt.astype(F32) + dt_bias.astype(F32))
    log_a = (dt * -jnp.exp(a_log.astype(F32))).reshape(B, L, G, R)
    x32 = xs.astype(F32)
    v = x32 * dt.reshape(B, L, G, R)[..., None]
    y, s = _decayed_linear_scan(c_out, b_in, v, log_a, s0.reshape(B, G, R, N, P))
    y = y + x32 * d_skip.astype(F32).reshape(G, R, 1)
    y = y.reshape(B, L, SSM_D_INNER) * jax.nn.silu(z.astype(F32))
    y = _rms_norm(y.reshape(B, L, G, SSM_D_INNER // G), norm_w.reshape(G, SSM_D_INNER // G))
    out = y.reshape(B, L, SSM_D_INNER).astype(u.dtype) @ w_out
    return out, s.reshape(B, SSM_HEADS, N, P).astype(u.dtype), c_new


def _gated_deltanet(u, s0, c0, w_in, conv_w, a_log, dt_bias, norm_w, w_out):
    B, L, _ = u.shape
    proj = u @ w_in
    qkv = proj[..., :DN_CONV_DIM]
    z = proj[..., DN_CONV_DIM:DN_CONV_DIM + DN_V_DIM].reshape(B, L, DN_V_HEADS, DN_HEAD_V)
    b = proj[..., DN_CONV_DIM + DN_V_DIM:DN_CONV_DIM + DN_V_DIM + DN_V_HEADS]
    a = proj[..., DN_CONV_DIM + DN_V_DIM + DN_V_HEADS:]
    qkv, c_new = _causal_conv(qkv, c0, conv_w)
    qkv = jax.nn.silu(qkv)
    rep = DN_V_HEADS // DN_K_HEADS
    q = _l2norm(qkv[..., :DN_QK_DIM].reshape(B, L, DN_K_HEADS, DN_HEAD_K))
    k = _l2norm(qkv[..., DN_QK_DIM:2 * DN_QK_DIM].reshape(B, L, DN_K_HEADS, DN_HEAD_K))
    q = jnp.repeat(q, rep, axis=2) * (DN_HEAD_K ** -0.5)
    k = jnp.repeat(k, rep, axis=2)
    v = qkv[..., 2 * DN_QK_DIM:].reshape(B, L, DN_V_HEADS, DN_HEAD_V)
    beta = jax.nn.sigmoid(b.astype(F32))
    g = -jnp.exp(a_log.astype(F32)) * jax.nn.softplus(a.astype(F32) + dt_bias.astype(F32))
    o, s = _gated_delta_scan(q, k, v, g, beta, s0)
    o = _rms_norm(o, norm_w) * jax.nn.silu(z.astype(F32))
    out = o.reshape(B, L, DN_V_DIM).astype(u.dtype) @ w_out
    return out, s.astype(u.dtype), c_new


def _rotary(x, pos):
    half = x.shape[-1] // 2
    inv = 1.0 / (RET_ROPE_BASE ** jnp.linspace(0.0, 1.0, half, dtype=F32))
    ang = pos.astype(F32)[:, None] * inv[None, :]
    cos = jnp.cos(ang)[None, :, None, :]
    sin = jnp.sin(ang)[None, :, None, :]
    x2 = x.astype(F32).reshape(x.shape[:-1] + (half, 2))
    xe, xo = x2[..., 0], x2[..., 1]
    return jnp.stack([xe * cos - xo * sin, xo * cos + xe * sin], axis=-1).reshape(x.shape)


def _retention(u, s0, pos, w_in, w_out):
    B, L, _ = u.shape
    H = RET_HEADS
    proj = u @ w_in
    q = _rotary(proj[..., :RET_QK_DIM].reshape(B, L, H, RET_DK), pos)
    k = _rotary(proj[..., RET_QK_DIM:2 * RET_QK_DIM].reshape(B, L, H, RET_DK), pos) * (RET_DK ** -0.5)
    v = proj[..., 2 * RET_QK_DIM:2 * RET_QK_DIM + RET_V_DIM].reshape(B, L, H, 1, RET_DV)
    gate = proj[..., 2 * RET_QK_DIM + RET_V_DIM:]
    log_gamma = jnp.log(1.0 - 2.0 ** (-5.0 - jnp.arange(H, dtype=F32)))
    log_a = jnp.broadcast_to(log_gamma[None, None, :, None], (B, L, H, 1))
    y, s = _decayed_linear_scan(q, k, v, log_a, s0[:, :, None])
    y = _rms_norm(y.reshape(B, L, H, RET_DV))
    y = y.reshape(B, L, RET_V_DIM) * jax.nn.silu(gate.astype(F32))
    out = y.astype(u.dtype) @ w_out
    return out, s[:, :, 0].astype(u.dtype)


def _forward(x, p, pos, ssm_s, ssm_c, dn_s, dn_c, ret_s, prm):
    h = x
    o_ssm, o_ssm_c, o_dn, o_dn_c, o_ret = [], [], [], [], []
    for i in range(DEPTH):
        kind, j = i % N_MIXERS, i // N_MIXERS
        u = _rms_norm(h, prm["norm_mix_pre"][i])
        if kind == 0:
            mix, s, c = _mamba2(u, ssm_s[j], ssm_c[j], prm["ssm_w_in"][j], prm["ssm_conv_w"][j],
                                prm["ssm_conv_b"][j], prm["ssm_dt_bias"][j], prm["ssm_a_log"][j],
                                prm["ssm_d"][j], prm["ssm_norm"][j], prm["ssm_w_out"][j])
            o_ssm.append(s)
            o_ssm_c.append(c)
        elif kind == 1:
            mix, s, c = _gated_deltanet(u, dn_s[j], dn_c[j], prm["dn_w_in"][j], prm["dn_conv_w"][j],
                                        prm["dn_a_log"][j], prm["dn_dt_bias"][j], prm["dn_norm"][j],
                                        prm["dn_w_out"][j])
            o_dn.append(s)
            o_dn_c.append(c)
        else:
            mix, s = _retention(u, ret_s[j], pos, prm["ret_w_in"][j], prm["ret_w_out"][j])
            o_ret.append(s)
        h = h + _rms_norm(mix, prm["norm_mix_post"][i]).astype(h.dtype)
        f = _rms_norm(h, prm["norm_ffn_pre"][i])
        f = (jax.nn.silu(f @ prm["ffn_w_gate"][i]) * (f @ prm["ffn_w_up"][i])) @ prm["ffn_w_down"][i]
        h = h + _rms_norm(f, prm["norm_ffn_post"][i]).astype(h.dtype)
        gate = jax.nn.sigmoid((_rms_norm(h, prm["norm_ple"][i]) @ prm["ple_w_gate"][i]).astype(F32))
        h = h + ((p[i] @ prm["ple_w_proj"][i]).astype(F32) * gate).astype(h.dtype)
    return h, jnp.stack(o_ssm), jnp.stack(o_ssm_c), jnp.stack(o_dn), jnp.stack(o_dn_c), jnp.stack(o_ret)


def setup_inputs(seed: int = 0) -> dict:
    key = jax.random.key(seed)
    ks = iter(jax.random.split(key, 48))
    d = D_MODEL
    na, nb, nc = N_LAYERS_SSM, N_LAYERS_DN, N_LAYERS_RET

    def nrm(shape, scale):
        return scale * jax.random.normal(next(ks), shape, F32)

    def gain(shape):
        return 1.0 + nrm(shape, 0.05)

    def dt_bias(shape):
        dt = jnp.exp(jax.random.uniform(next(ks), shape, F32, math.log(1e-3), math.log(1e-1)))
        return dt + jnp.log(-jnp.expm1(-dt))

    def a_log(shape):
        return jnp.log(jax.random.uniform(next(ks), shape, F32, 1.0, 16.0))

    return {
        "x_prompt": nrm((BATCH, SEQ, d), 1.0),
        "x_sample": nrm((DEC_BATCH, DEC_SEQ, d), 1.0),
        "state_ssm": nrm((na, DEC_BATCH, SSM_HEADS, SSM_D_STATE, SSM_HEAD_DIM), 0.1),
        "state_ssm_conv": nrm((na, DEC_BATCH, CONV_W - 1, SSM_CONV_DIM), 1.0),
        "state_delta": nrm((nb, DEC_BATCH, DN_V_HEADS, DN_HEAD_K, DN_HEAD_V), 0.1),
        "state_delta_conv": nrm((nb, DEC_BATCH, CONV_W - 1, DN_CONV_DIM), 1.0),
        "state_ret": nrm((nc, DEC_BATCH, RET_HEADS, RET_DK, RET_DV), 0.1),
        "p_prompt": nrm((DEPTH, BATCH, SEQ, PLE_DIM), 1.0),
        "p_sample": nrm((DEPTH, DEC_BATCH, DEC_SEQ, PLE_DIM), 1.0),
        "norm_mix_pre": gain((DEPTH, d)),
        "norm_mix_post": gain((DEPTH, d)),
        "norm_ffn_pre": gain((DEPTH, d)),
        "norm_ffn_post": gain((DEPTH, d)),
        "norm_ple": gain((DEPTH, d)),
        "ffn_w_gate": nrm((DEPTH, d, D_FF), d ** -0.5),
        "ffn_w_up": nrm((DEPTH, d, D_FF), d ** -0.5),
        "ffn_w_down": nrm((DEPTH, D_FF, d), D_FF ** -0.5),
        "ple_w_proj": nrm((DEPTH, PLE_DIM, d), PLE_DIM ** -0.5),
        "ple_w_gate": nrm((DEPTH, d, d), d ** -0.5),
        "ssm_w_in": nrm((na, d, SSM_IN_DIM), d ** -0.5),
        "ssm_conv_w": nrm((na, CONV_W, SSM_CONV_DIM), 0.5),
        "ssm_conv_b": nrm((na, SSM_CONV_DIM), 0.02),
        "ssm_dt_bias": dt_bias((na, SSM_HEADS)),
        "ssm_a_log": a_log((na, SSM_HEADS)),
        "ssm_d": gain((na, SSM_HEADS)),
        "ssm_norm": gain((na, SSM_D_INNER)),
        "ssm_w_out": nrm((na, SSM_D_INNER, d), SSM_D_INNER ** -0.5),
        "dn_w_in": nrm((nb, d, DN_IN_DIM), d ** -0.5),
        "dn_conv_w": nrm((nb, CONV_W, DN_CONV_DIM), 0.5),
        "dn_a_log": a_log((nb, DN_V_HEADS)),
        "dn_dt_bias": dt_bias((nb, DN_V_HEADS)),
        "dn_norm": gain((nb, DN_HEAD_V)),
        "dn_w_out": nrm((nb, DN_V_DIM, d), DN_V_DIM ** -0.5),
        "ret_w_in": nrm((nc, d, RET_IN_DIM), d ** -0.5),
        "ret_w_out": nrm((nc, RET_V_DIM, d), RET_V_DIM ** -0.5),
    }


def reference(x_prompt, x_sample, state_ssm, state_ssm_conv, state_delta, state_delta_conv, state_ret,
              p_prompt, p_sample,
              norm_mix_pre, norm_mix_post, norm_ffn_pre, norm_ffn_post, norm_ple,
              ffn_w_gate, ffn_w_up, ffn_w_down, ple_w_proj, ple_w_gate,
              ssm_w_in, ssm_conv_w, ssm_conv_b, ssm_dt_bias, ssm_a_log, ssm_d, ssm_norm, ssm_w_out,
              dn_w_in, dn_conv_w, dn_a_log, dn_dt_bias, dn_norm, dn_w_out,
              ret_w_in, ret_w_out):
    prm = {
        "norm_mix_pre": norm_mix_pre, "norm_mix_post": norm_mix_post,
        "norm_ffn_pre": norm_ffn_pre, "norm_ffn_post": norm_ffn_post, "norm_ple": norm_ple,
        "ffn_w_gate": ffn_w_gate, "ffn_w_up": ffn_w_up, "ffn_w_down": ffn_w_down,
        "ple_w_proj": ple_w_proj, "ple_w_gate": ple_w_gate,
        "ssm_w_in": ssm_w_in, "ssm_conv_w": ssm_conv_w, "ssm_conv_b": ssm_conv_b,
        "ssm_dt_bias": ssm_dt_bias, "ssm_a_log": ssm_a_log, "ssm_d": ssm_d, "ssm_norm": ssm_norm,
        "ssm_w_out": ssm_w_out,
        "dn_w_in": dn_w_in, "dn_conv_w": dn_conv_w, "dn_a_log": dn_a_log, "dn_dt_bias": dn_dt_bias,
        "dn_norm": dn_norm, "dn_w_out": dn_w_out,
        "ret_w_in": ret_w_in, "ret_w_out": ret_w_out,
    }
    bp = x_prompt.shape[0]
    dt_ = x_prompt.dtype
    z_ssm = jnp.zeros((state_ssm.shape[0], bp) + state_ssm.shape[2:], dt_)
    z_ssm_c = jnp.zeros((state_ssm_conv.shape[0], bp) + state_ssm_conv.shape[2:], dt_)
    z_dn = jnp.zeros((state_delta.shape[0], bp) + state_delta.shape[2:], dt_)
    z_dn_c = jnp.zeros((state_delta_conv.shape[0], bp) + state_delta_conv.shape[2:], dt_)
    z_ret = jnp.zeros((state_ret.shape[0], bp) + state_ret.shape[2:], dt_)
    pos_prompt = jnp.arange(x_prompt.shape[1], dtype=jnp.int32)
    pos_sample = PAST_LEN + jnp.arange(x_sample.shape[1], dtype=jnp.int32)

    y_prompt, ssm_p, ssmc_p, dn_p, dnc_p, ret_p = _forward(
        x_prompt, p_prompt, pos_prompt, z_ssm, z_ssm_c, z_dn, z_dn_c, z_ret, prm)
    y_sample, ssm_s, ssmc_s, dn_s, dnc_s, ret_s = _forward(
        x_sample, p_sample, pos_sample, state_ssm, state_ssm_conv, state_delta, state_delta_conv, state_ret, prm)
    return (y_prompt, y_sample, ssm_p, ssmc_p, dn_p, dnc_p, ret_p, ssm_s, ssmc_s, dn_s, dnc_s, ret_s)
```

```python
import functools
import math

import jax
import jax.numpy as jnp
from jax import lax
from jax.experimental import pallas as pl
from jax.experimental.pallas import tpu as pltpu

F32 = jnp.float32
BF16 = jnp.bfloat16

NORM_EPS = 1e-6
PAST_LEN = 16384
CONV_W = 4
SSM_GROUPS = 8
SSM_HEAD_DIM = 64
SSM_D_STATE = 128
DN_HEAD = 128
RET_DK = 256
RET_DV = 512
RET_ROPE_BASE = 10000.0
PLE_DIM = 256

LANES = 128
SUBLANES = 8
NEG_BIG = -1e30
SSD_CHUNK = 128
RET_CHUNK = 128
DN_CHUNK = 64
STEP_BATCH = 8
VMEM_LIMIT = 52 * 1024 * 1024


def _pick(n, prefs):
    for p in prefs:
        if n % p == 0:
            return p
    return n


def _cparams(sem):
    return pltpu.CompilerParams(dimension_semantics=sem, vmem_limit_bytes=VMEM_LIMIT)


def _sigmoid(x):
    return 1.0 / (1.0 + jnp.exp(-x))


def _silu(x):
    return x * _sigmoid(x)


def _softplus(x):
    return jnp.maximum(x, 0.0) + jnp.log1p(jnp.exp(-jnp.abs(x)))


def _dot(a, b):
    return jnp.dot(a.astype(BF16), b.astype(BF16), preferred_element_type=F32)


def _dot_nt(a, b):
    return lax.dot_general(a.astype(BF16), b.astype(BF16), (((1,), (1,)), ((), ())),
                           preferred_element_type=F32)


def _split2(x):
    hi = x.astype(BF16)
    lo = (x - hi.astype(F32)).astype(BF16)
    return hi, lo


def _dot3(a, b):
    ah, al = _split2(a)
    bh, bl = _split2(b)
    out = jnp.dot(ah, bh, preferred_element_type=F32)
    out = out + jnp.dot(ah, bl, preferred_element_type=F32)
    return out + jnp.dot(al, bh, preferred_element_type=F32)


def _dot01(m01, x):
    hi = x.astype(BF16)
    r = x - hi.astype(F32)
    mid = r.astype(BF16)
    lo = (r - mid.astype(F32)).astype(BF16)
    out = jnp.dot(m01, hi, preferred_element_type=F32)
    out = out + jnp.dot(m01, mid, preferred_element_type=F32)
    return out + jnp.dot(m01, lo, preferred_element_type=F32)


def _tr(x):
    r, c = x.shape
    assert c == LANES and r <= LANES
    if r < LANES:
        x = jnp.concatenate([x, jnp.zeros((LANES - r, c), x.dtype)], axis=0)
    return x.T[:, :r]


def _expand64(x, nheads):
    rows = x.shape[0]
    lane = lax.broadcasted_iota(jnp.int32, (rows, LANES), 1)
    parts = []
    for j in range(nheads // 2):
        a = jnp.broadcast_to(x[:, 2 * j:2 * j + 1], (rows, LANES))
        b = jnp.broadcast_to(x[:, 2 * j + 1:2 * j + 2], (rows, LANES))
        parts.append(jnp.where(lane < 64, a, b))
    return parts[0] if len(parts) == 1 else jnp.concatenate(parts, axis=1)


def _rms(y):
    return y * lax.rsqrt(jnp.mean(y * y, axis=-1, keepdims=True) + NORM_EPS)


def _conv_chunk(buf_ref, col0, x, w, bias):
    q, width = x.shape
    cols = slice(col0, col0 + width)
    buf_ref[SUBLANES:SUBLANES + q, cols] = x
    acc = x * w[CONV_W - 1:CONV_W, :]
    for s in range(1, CONV_W):
        acc = acc + buf_ref[SUBLANES - s:SUBLANES - s + q, cols] * w[CONV_W - 1 - s:CONV_W - s, :]
    buf_ref[0:SUBLANES, cols] = x[q - SUBLANES:q, :]
    if bias is not None:
        acc = acc + bias
    return acc


def _tri_inv(a, row, col):
    q = a.shape[0]
    eye = (row == col).astype(F32)
    d = eye - jnp.where(jnp.right_shift(row, 1) == jnp.right_shift(col, 1), a, 0.0)
    sh = 1
    while (1 << sh) < q:
        same_big = jnp.right_shift(row, sh + 1) == jnp.right_shift(col, sh + 1)
        diff_small = jnp.right_shift(row, sh) != jnp.right_shift(col, sh)
        lb = jnp.where(same_big & diff_small, a, 0.0)
        d = d - _dot3(d, _dot3(lb, d))
        sh += 1
    return d


def _norm_mm_kernel(x_ref, g_ref, w_ref, o_ref, xn_ref):
    @pl.when(pl.program_id(1) == 0)
    def _():
        xn_ref[...] = (_rms(x_ref[...]) * g_ref[...]).astype(BF16)

    o_ref[...] = jnp.dot(xn_ref[...], w_ref[...], preferred_element_type=F32)


def _norm_mm(x, gain, w):
    m, k = x.shape
    n = w.shape[1]
    tm = _pick(m, (1024, 512, 256, 128))
    tn = _pick(n, (512, 256, 128))
    return pl.pallas_call(
        _norm_mm_kernel,
        grid=(m // tm, n // tn),
        in_specs=[pl.BlockSpec((tm, k), lambda i, j: (i, 0)),
                  pl.BlockSpec((1, k), lambda i, j: (0, 0)),
                  pl.BlockSpec((k, tn), lambda i, j: (0, j))],
        out_specs=pl.BlockSpec((tm, tn), lambda i, j: (i, j)),
        out_shape=jax.ShapeDtypeStruct((m, n), F32),
        scratch_shapes=[pltpu.VMEM((tm, k), BF16)],
        compiler_params=_cparams(("parallel", "arbitrary")),
        name="norm_mm",
    )(x, gain.reshape(1, k), w)


def _mm_out_kernel(a_ref, w_ref, h_ref, g_ref, o_ref, acc_ref):
    kk = pl.program_id(1)

    @pl.when(kk == 0)
    def _():
        acc_ref[...] = jnp.zeros_like(acc_ref)

    acc_ref[...] += jnp.dot(a_ref[...], w_ref[...], preferred_element_type=F32)

    @pl.when(kk == pl.num_programs(1) - 1)
    def _():
        o_ref[...] = h_ref[...] + _rms(acc_ref[...]) * g_ref[...]


def _mm_out(a, w, h, gain):
    m, k = a.shape
    n = w.shape[1]
    tm = _pick(m, (512, 256, 128))
    tk = _pick(k, (512, 256, 128))
    return pl.pallas_call(
        _mm_out_kernel,
        grid=(m // tm, k // tk),
        in_specs=[pl.BlockSpec((tm, tk), lambda i, j: (i, j)),
                  pl.BlockSpec((tk, n), lambda i, j: (j, 0)),
                  pl.BlockSpec((tm, n), lambda i, j: (i, 0)),
                  pl.BlockSpec((1, n), lambda i, j: (0, 0))],
        out_specs=pl.BlockSpec((tm, n), lambda i, j: (i, 0)),
        out_shape=jax.ShapeDtypeStruct((m, n), F32),
        scratch_shapes=[pltpu.VMEM((tm, n), F32)],
        compiler_params=_cparams(("parallel", "arbitrary")),
        name="mm_out",
    )(a, w, h, gain.reshape(1, n))


def _ffn_in_kernel(x_ref, g_ref, wg_ref, wu_ref, o_ref, xn_ref):
    @pl.when(pl.program_id(1) == 0)
    def _():
        xn_ref[...] = (_rms(x_ref[...]) * g_ref[...]).astype(BF16)

    xn = xn_ref[...]
    gate = jnp.dot(xn, wg_ref[...], preferred_element_type=F32)
    up = jnp.dot(xn, wu_ref[...], preferred_element_type=F32)
    o_ref[...] = (_silu(gate) * up).astype(BF16)


def _ffn_in(x, gain, wg, wu):
    m, k = x.shape
    n = wg.shape[1]
    tm = _pick(m, (1024, 512, 256, 128))
    tn = _pick(n, (512, 256, 128))
    return pl.pallas_call(
        _ffn_in_kernel,
        grid=(m // tm, n // tn),
        in_specs=[pl.BlockSpec((tm, k), lambda i, j: (i, 0)),
                  pl.BlockSpec((1, k), lambda i, j: (0, 0)),
                  pl.BlockSpec((k, tn), lambda i, j: (0, j)),
                  pl.BlockSpec((k, tn), lambda i, j: (0, j))],
        out_specs=pl.BlockSpec((tm, tn), lambda i, j: (i, j)),
        out_shape=jax.ShapeDtypeStruct((m, n), BF16),
        scratch_shapes=[pltpu.VMEM((tm, k), BF16)],
        compiler_params=_cparams(("parallel", "arbitrary")),
        name="ffn_in",
    )(x, gain.reshape(1, k), wg, wu)


def _ple_kernel(h_ref, hc_ref, g_ref, wg_ref, p_ref, wp_ref, o_ref, xn_ref):
    @pl.when(pl.program_id(1) == 0)
    def _():
        xn_ref[...] = (_rms(h_ref[...]) * g_ref[...]).astype(BF16)

    gate = _sigmoid(jnp.dot(xn_ref[...], wg_ref[...], preferred_element_type=F32))
    proj = jnp.dot(p_ref[...].astype(BF16), wp_ref[...], preferred_element_type=F32)
    o_ref[...] = hc_ref[...] + proj * gate


def _ple(h, gain, wg, p, wp):
    m, k = h.shape
    n = wg.shape[1]
    pd = p.shape[1]
    tm = _pick(m, (1024, 512, 256, 128))
    tn = _pick(n, (512, 256, 128))
    return pl.pallas_call(
        _ple_kernel,
        grid=(m // tm, n // tn),
        in_specs=[pl.BlockSpec((tm, k), lambda i, j: (i, 0)),
                  pl.BlockSpec((tm, tn), lambda i, j: (i, j)),
                  pl.BlockSpec((1, k), lambda i, j: (0, 0)),
                  pl.BlockSpec((k, tn), lambda i, j: (0, j)),
                  pl.BlockSpec((tm, pd), lambda i, j: (i, 0)),
                  pl.BlockSpec((pd, tn), lambda i, j: (0, j))],
        out_specs=pl.BlockSpec((tm, tn), lambda i, j: (i, j)),
        out_shape=jax.ShapeDtypeStruct((m, n), F32),
        scratch_shapes=[pltpu.VMEM((tm, k), BF16)],
        compiler_params=_cparams(("parallel", "arbitrary")),
        name="ple",
    )(h, h, gain.reshape(1, k), wg, p, wp)


def _ssd_prompt_kernel(z_ref, xs_ref, b_ref, c_ref, dt_ref, wx_ref, wb_ref, wc_ref, bx_ref, bb_ref, bc_ref,
                       dtb_ref, alog_ref, d_ref, nw_ref, y_ref, st_ref, s_scr, cbuf, *, nheads):
    c = pl.program_id(2)
    q, width = xs_ref.shape
    n = b_ref.shape[1]

    @pl.when(c == 0)
    def _():
        s_scr[...] = jnp.zeros_like(s_scr)
        cbuf[0:SUBLANES, :] = jnp.zeros((SUBLANES, cbuf.shape[1]), F32)

    xs = _silu(_conv_chunk(cbuf, 0, xs_ref[...], wx_ref[...], bx_ref[...]))
    bm = _silu(_conv_chunk(cbuf, width, b_ref[...], wb_ref[...], bb_ref[...]))
    cm = _silu(_conv_chunk(cbuf, width + n, c_ref[...], wc_ref[...], bc_ref[...]))

    dtv = _softplus(dt_ref[...] + dtb_ref[...])
    la = dtv * (-jnp.exp(alog_ref[...]))
    row = lax.broadcasted_iota(jnp.int32, (q, q), 0)
    col = lax.broadcasted_iota(jnp.int32, (q, q), 1)
    causal = row >= col
    acs = _dot01(causal.astype(BF16), la)
    acs_t = _tr(acs)
    scores = _dot_nt(cm, bm)
    v_all = xs * _expand64(dtv, nheads)
    s_old = s_scr[...]
    y = _dot(cm, s_old) * _expand64(jnp.exp(acs), nheads)
    lane = lax.broadcasted_iota(jnp.int32, (q, LANES), 1)
    parts = []
    for j in range(nheads // 2):
        vp = v_all[:, LANES * j:LANES * (j + 1)].astype(BF16)
        ys = []
        for t in range(2):
            r = 2 * j + t
            seg = acs[:, r:r + 1] - acs_t[r:r + 1, :]
            decay = jnp.exp(jnp.where(causal, seg, NEG_BIG))
            ys.append(jnp.dot((scores * decay).astype(BF16), vp, preferred_element_type=F32))
        parts.append(jnp.where(lane < 64, ys[0], ys[1]))
    y = y + (parts[0] if len(parts) == 1 else jnp.concatenate(parts, axis=1))
    y = y + xs * _expand64(d_ref[...], nheads)

    last = acs[q - 1:q, :]
    wv = v_all * _expand64(jnp.exp(last - acs), nheads)
    s_new = s_old * _expand64(jnp.exp(last), nheads) + _dot(_tr(bm), wv)
    s_scr[...] = s_new

    y = y * _silu(z_ref[...])
    y_ref[...] = (_rms(y) * nw_ref[...]).astype(BF16)

    @pl.when(c == pl.num_programs(2) - 1)
    def _():
        for r in range(nheads):
            st_ref[0, r] = s_new[:, SSM_HEAD_DIM * r:SSM_HEAD_DIM * (r + 1)]


def _ssd_prompt(proj, dtp, w, batch, seq):
    d_inner, nh, g, n = w["d_inner"], w["hpg"], SSM_GROUPS, SSM_D_STATE
    width = nh * SSM_HEAD_DIM
    q = _pick(seq, (SSD_CHUNK, 64, 32, 16, 8))
    nc = seq // q
    xs0, b0, c0 = d_inner // width, 2 * d_inner // n, 2 * d_inner // n + g
    cw0 = d_inner // n
    tok = lambda b, gi, c: b * nc + c
    in_specs = [
        pl.BlockSpec((q, width), lambda b, gi, c: (tok(b, gi, c), gi)),
        pl.BlockSpec((q, width), lambda b, gi, c: (tok(b, gi, c), xs0 + gi)),
        pl.BlockSpec((q, n), lambda b, gi, c: (tok(b, gi, c), b0 + gi)),
        pl.BlockSpec((q, n), lambda b, gi, c: (tok(b, gi, c), c0 + gi)),
        pl.BlockSpec((q, LANES), lambda b, gi, c: (tok(b, gi, c), gi)),
        pl.BlockSpec((CONV_W, width), lambda b, gi, c: (0, gi)),
        pl.BlockSpec((CONV_W, n), lambda b, gi, c: (0, cw0 + gi)),
        pl.BlockSpec((CONV_W, n), lambda b, gi, c: (0, cw0 + g + gi)),
        pl.BlockSpec((1, width), lambda b, gi, c: (0, gi)),
        pl.BlockSpec((1, n), lambda b, gi, c: (0, cw0 + gi)),
        pl.BlockSpec((1, n), lambda b, gi, c: (0, cw0 + g + gi)),
        pl.BlockSpec((1, LANES), lambda b, gi, c: (0, gi)),
        pl.BlockSpec((1, LANES), lambda b, gi, c: (0, gi)),
        pl.BlockSpec((1, LANES), lambda b, gi, c: (0, gi)),
        pl.BlockSpec((1, width), lambda b, gi, c: (0, gi)),
    ]
    y, st = pl.pallas_call(
        functools.partial(_ssd_prompt_kernel, nheads=nh),
        grid=(batch, g, nc),
        in_specs=in_specs,
        out_specs=[pl.BlockSpec((q, width), lambda b, gi, c: (tok(b, gi, c), gi)),
                   pl.BlockSpec((1, nh, n, SSM_HEAD_DIM), lambda b, gi, c: (b, gi, 0, 0))],
        out_shape=[jax.ShapeDtypeStruct((batch * seq, d_inner), BF16),
                   jax.ShapeDtypeStruct((batch, g * nh, n, SSM_HEAD_DIM), F32)],
        scratch_shapes=[pltpu.VMEM((n, width), F32),
                        pltpu.VMEM((q + SUBLANES, width + 2 * n), F32)],
        compiler_params=_cparams(("parallel", "parallel", "arbitrary")),
        name="ssd_prompt",
    )(proj, proj, proj, proj, dtp, w["conv_w"], w["conv_w"], w["conv_w"], w["conv_b"], w["conv_b"], w["conv_b"],
      w["dt_bias"], w["a_log"], w["d"], w["norm"])
    return y, st


def _dn_prompt_kernel(q_ref, k_ref, v_ref, z_ref, ba_ref, wq_ref, wk_ref, wv_ref, alog_ref, dtb_ref, nw_ref,
                      o_ref, st_ref, s_scr, cbuf):
    c = pl.program_id(2)
    q = q_ref.shape[0]
    hd = DN_HEAD

    @pl.when(c == 0)
    def _():
        s_scr[...] = jnp.zeros_like(s_scr)
        cbuf[0:SUBLANES, :] = jnp.zeros((SUBLANES, cbuf.shape[1]), F32)

    qq = _silu(_conv_chunk(cbuf, 0, q_ref[...], wq_ref[...], None))
    kk_ = _silu(_conv_chunk(cbuf, hd, k_ref[...], wk_ref[...], None))
    vv = _silu(_conv_chunk(cbuf, 2 * hd, v_ref[...], wv_ref[...], None))
    qq = qq * lax.rsqrt(jnp.sum(qq * qq, axis=-1, keepdims=True) + NORM_EPS) * (hd ** -0.5)
    kk_ = kk_ * lax.rsqrt(jnp.sum(kk_ * kk_, axis=-1, keepdims=True) + NORM_EPS)

    ba = ba_ref[...]
    beta = _sigmoid(ba)
    gg = -jnp.exp(alog_ref[...]) * _softplus(ba + dtb_ref[...])
    row = lax.broadcasted_iota(jnp.int32, (q, q), 0)
    col = lax.broadcasted_iota(jnp.int32, (q, q), 1)
    incl = row >= col
    strict = row > col
    acs = _dot01(incl.astype(BF16), gg)
    acs_t = _tr(acs)
    kk = _dot_nt(kk_, kk_)
    qk = _dot_nt(qq, kk_)
    k_t = _tr(kk_).astype(BF16)

    for j in range(2):
        a_col = acs[:, 2 + j:3 + j]
        seg = a_col - acs_t[2 + j:3 + j, :]
        gam = jnp.exp(jnp.where(incl, seg, NEG_BIG))
        bcol = beta[:, j:j + 1]
        a_mat = jnp.where(strict, kk * gam * bcol, 0.0)
        t_mat = _tri_inv(a_mat, row, col)
        vj = vv[:, hd * j:hd * (j + 1)]
        rhs = jnp.concatenate([bcol * vj, (bcol * jnp.exp(a_col)) * kk_], axis=1)
        sol = _dot3(t_mat, rhs)
        s_old = s_scr[j]
        u = sol[:, :hd] - _dot(sol[:, hd:], s_old)
        o = _dot(qk * gam, u) + _dot(qq, s_old) * jnp.exp(a_col)
        last = acs[q - 1:q, 2 + j:3 + j]
        s_new = s_old * jnp.exp(last) + jnp.dot(k_t, (jnp.exp(last - a_col) * u).astype(BF16),
                                                preferred_element_type=F32)
        s_scr[j] = s_new
        on = _rms(o) * nw_ref[...]
        o_ref[:, hd * j:hd * (j + 1)] = (on * _silu(z_ref[:, hd * j:hd * (j + 1)])).astype(BF16)

    @pl.when(c == pl.num_programs(2) - 1)
    def _():
        st_ref[0] = s_scr[...]


def _dn_prompt(proj, ba, w, batch, seq):
    hk, hd = w["hk"], DN_HEAD
    q = _pick(seq, (DN_CHUNK, 32, 16, 8))
    nc = seq // q
    tok = lambda b, h, c: b * nc + c
    v0 = 2 * hk * hd // (2 * hd)
    z0 = (4 * hk * hd) // (2 * hd)
    in_specs = [
        pl.BlockSpec((q, hd), lambda b, h, c: (tok(b, h, c), h)),
        pl.BlockSpec((q, hd), lambda b, h, c: (tok(b, h, c), hk + h)),
        pl.BlockSpec((q, 2 * hd), lambda b, h, c: (tok(b, h, c), v0 + h)),
        pl.BlockSpec((q, 2 * hd), lambda b, h, c: (tok(b, h, c), z0 + h)),
        pl.BlockSpec((q, LANES), lambda b, h, c: (tok(b, h, c), h)),
        pl.BlockSpec((CONV_W, hd), lambda b, h, c: (0, h)),
        pl.BlockSpec((CONV_W, hd), lambda b, h, c: (0, hk + h)),
        pl.BlockSpec((CONV_W, 2 * hd), lambda b, h, c: (0, v0 + h)),
        pl.BlockSpec((1, LANES), lambda b, h, c: (0, h)),
        pl.BlockSpec((1, LANES), lambda b, h, c: (0, h)),
        pl.BlockSpec((1, hd), lambda b, h, c: (0, 0)),
    ]
    o, st = pl.pallas_call(
        _dn_prompt_kernel,
        grid=(batch, hk, nc),
        in_specs=in_specs,
        out_specs=[pl.BlockSpec((q, 2 * hd), lambda b, h, c: (tok(b, h, c), h)),
                   pl.BlockSpec((1, 2, hd, hd), lambda b, h, c: (b, h, 0, 0))],
        out_shape=[jax.ShapeDtypeStruct((batch * seq, 2 * hk * hd), BF16),
                   jax.ShapeDtypeStruct((batch, 2 * hk, hd, hd), F32)],
        scratch_shapes=[pltpu.VMEM((2, hd, hd), F32),
                        pltpu.VMEM((q + SUBLANES, 4 * hd), F32)],
        compiler_params=_cparams(("parallel", "parallel", "arbitrary")),
        name="dn_prompt",
    )(proj, proj, proj, proj, ba, w["conv_w"], w["conv_w"], w["conv_w"], w["a_log"], w["dt_bias"], w["norm"])
    return o, st


def _rotate(x, cos, s1, s2):
    w = x.shape[1]
    return x * cos + pltpu.roll(x, w - 1, 1) * s1 + pltpu.roll(x, 1, 1) * s2


def _ret_prompt_kernel(q_ref, k_ref, v_ref, g_ref, cos_ref, s1_ref, s2_ref, lg_ref, y_ref, st_ref, s_scr):
    c = pl.program_id(2)
    q = q_ref.shape[0]

    @pl.when(c == 0)
    def _():
        s_scr[...] = jnp.zeros_like(s_scr)

    cos, s1, s2 = cos_ref[...], s1_ref[...], s2_ref[...]
    qq = _rotate(q_ref[...], cos, s1, s2)
    kk = _rotate(k_ref[...], cos, s1, s2) * (RET_DK ** -0.5)
    vv = v_ref[...]
    lg = lg_ref[0][0:1, 0:1]
    row = lax.broadcasted_iota(jnp.int32, (q, q), 0)
    col = lax.broadcasted_iota(jnp.int32, (q, q), 1)
    causal = row >= col
    decay = jnp.exp(jnp.where(causal, (row - col).astype(F32) * lg, NEG_BIG))
    pos = lax.broadcasted_iota(jnp.int32, (q, 1), 0).astype(F32)
    scores = _dot_nt(qq, kk)
    s_old = s_scr[...]
    y = _dot(scores * decay, vv) + _dot(qq, s_old) * jnp.exp((pos + 1.0) * lg)
    wv = jnp.exp((float(q - 1) - pos) * lg) * vv
    k_t = jnp.concatenate([_tr(kk[:, LANES * i:LANES * (i + 1)]) for i in range(RET_DK // LANES)], axis=0)
    s_new = s_old * jnp.exp(float(q) * lg) + _dot(k_t, wv)
    s_scr[...] = s_new
    y_ref[...] = (_rms(y) * _silu(g_ref[...])).astype(BF16)

    @pl.when(c == pl.num_programs(2) - 1)
    def _():
        st_ref[0, 0] = s_new


def _ret_prompt(proj, rope, lg, nheads, batch, seq):
    q = _pick(seq, (RET_CHUNK, 64, 32, 16, 8))
    nc = seq // q
    tok = lambda b, h, c: b * nc + c
    v0 = 2 * nheads * RET_DK // RET_DV
    g0 = v0 + nheads
    cos, s1, s2 = rope
    in_specs = [
        pl.BlockSpec((q, RET_DK), lambda b, h, c: (tok(b, h, c), h)),
        pl.BlockSpec((q, RET_DK), lambda b, h, c: (tok(b, h, c), nheads + h)),
        pl.BlockSpec((q, RET_DV), lambda b, h, c: (tok(b, h, c), v0 + h)),
        pl.BlockSpec((q, RET_DV), lambda b, h, c: (tok(b, h, c), g0 + h)),
        pl.BlockSpec((q, RET_DK), lambda b, h, c: (c, 0)),
        pl.BlockSpec((q, RET_DK), lambda b, h, c: (c, 0)),
        pl.BlockSpec((q, RET_DK), lambda b, h, c: (c, 0)),
        pl.BlockSpec((1, SUBLANES, LANES), lambda b, h, c: (h, 0, 0)),
    ]
    y, st = pl.pallas_call(
        _ret_prompt_kernel,
        grid=(batch, nheads, nc),
        in_specs=in_specs,
        out_specs=[pl.BlockSpec((q, RET_DV), lambda b, h, c: (tok(b, h, c), h)),
                   pl.BlockSpec((1, 1, RET_DK, RET_DV), lambda b, h, c: (b, h, 0, 0))],
        out_shape=[jax.ShapeDtypeStruct((batch * seq, nheads * RET_DV), BF16),
                   jax.ShapeDtypeStruct((batch, nheads, RET_DK, RET_DV), F32)],
        scratch_shapes=[pltpu.VMEM((RET_DK, RET_DV), F32)],
        compiler_params=_cparams(("parallel", "parallel", "arbitrary")),
        name="ret_prompt",
    )(proj, proj, proj, proj, cos, s1, s2, lg)
    return y, st


def _conv_step_kernel(*refs, has_bias):
    if has_bias:
        x_ref, c0_ref, c1_ref, c2_ref, w_ref, b_ref, o_ref = refs
    else:
        x_ref, c0_ref, c1_ref, c2_ref, w_ref, o_ref = refs
    w = w_ref[...]
    acc = c0_ref[...] * w[0:1, :] + c1_ref[...] * w[1:2, :] + c2_ref[...] * w[2:3, :] + x_ref[...] * w[3:4, :]
    if has_bias:
        acc = acc + b_ref[...]
    o_ref[...] = _silu(acc)


def _conv_step(proj, col0, cdim, cstate, conv_w, conv_b):
    m = proj.shape[0]
    cb = _pick(cdim, (512, 256, 128))
    nb = cdim // cb
    x0 = col0 // cb
    cflat = cstate.reshape(m, (CONV_W - 1) * cdim)
    in_specs = [pl.BlockSpec((m, cb), lambda j: (0, x0 + j)),
                pl.BlockSpec((m, cb), lambda j: (0, j)),
                pl.BlockSpec((m, cb), lambda j: (0, nb + j)),
                pl.BlockSpec((m, cb), lambda j: (0, 2 * nb + j)),
                pl.BlockSpec((CONV_W, cb), lambda j: (0, j))]
    args = [proj, cflat, cflat, cflat, conv_w]
    if conv_b is not None:
        in_specs.append(pl.BlockSpec((1, cb), lambda j: (0, j)))
        args.append(conv_b)
    return pl.pallas_call(
        functools.partial(_conv_step_kernel, has_bias=conv_b is not None),
        grid=(nb,),
        in_specs=in_specs,
        out_specs=pl.BlockSpec((m, cb), lambda j: (0, j)),
        out_shape=jax.ShapeDtypeStruct((m, cdim), F32),
        compiler_params=_cparams(("parallel",)),
        name="conv_step",
    )(*args)


def _ssd_step_kernel(z_ref, xs_ref, b_ref, c_ref, dt_ref, dtb_ref, alog_ref, d_ref, nw_ref, st_ref,
                     y_ref, so_ref, ybuf, *, nheads):
    nb = xs_ref.shape[0]
    xs = xs_ref[...]
    dtv = _softplus(dt_ref[...] + dtb_ref[...])
    decay = jnp.exp(dtv * (-jnp.exp(alog_ref[...])))
    k_t = _tr(b_ref[...])
    q_t = _tr(c_ref[...])
    hd = SSM_HEAD_DIM
    for bi in range(nb):
        kcol = k_t[:, bi:bi + 1]
        qcol = q_t[:, bi:bi + 1]
        for r in range(nheads):
            v = xs[bi:bi + 1, hd * r:hd * (r + 1)] * dtv[bi:bi + 1, r:r + 1]
            s_new = st_ref[bi, r] * decay[bi:bi + 1, r:r + 1] + kcol * v
            so_ref[bi, r] = s_new
            ybuf[bi:bi + 1, hd * r:hd * (r + 1)] = jnp.sum(qcol * s_new, axis=0, keepdims=True)
    y = ybuf[...] + xs * _expand64(d_ref[...], nheads)
    y = y * _silu(z_ref[...])
    y_ref[...] = (_rms(y) * nw_ref[...]).astype(BF16)


def _ssd_step(proj, xbc, dtp, w, state):
    m = proj.shape[0]
    d_inner, nh, g, n = w["d_inner"], w["hpg"], SSM_GROUPS, SSM_D_STATE
    width = nh * SSM_HEAD_DIM
    nb = _pick(m, (STEP_BATCH,))
    b0 = d_inner // n
    in_specs = [
        pl.BlockSpec((nb, width), lambda i, gi: (i, gi)),
        pl.BlockSpec((nb, width), lambda i, gi: (i, gi)),
        pl.BlockSpec((nb, n), lambda i, gi: (i, b0 + gi)),
        pl.BlockSpec((nb, n), lambda i, gi: (i, b0 + g + gi)),
        pl.BlockSpec((nb, LANES), lambda i, gi: (i, gi)),
        pl.BlockSpec((1, LANES), lambda i, gi: (0, gi)),
        pl.BlockSpec((1, LANES), lambda i, gi: (0, gi)),
        pl.BlockSpec((1, LANES), lambda i, gi: (0, gi)),
        pl.BlockSpec((1, width), lambda i, gi: (0, gi)),
        pl.BlockSpec((nb, nh, n, SSM_HEAD_DIM), lambda i, gi: (i, gi, 0, 0)),
    ]
    y, st = pl.pallas_call(
        functools.partial(_ssd_step_kernel, nheads=nh),
        grid=(m // nb, g),
        in_specs=in_specs,
        out_specs=[pl.BlockSpec((nb, width), lambda i, gi: (i, gi)),
                   pl.BlockSpec((nb, nh, n, SSM_HEAD_DIM), lambda i, gi: (i, gi, 0, 0))],
        out_shape=[jax.ShapeDtypeStruct((m, d_inner), BF16),
                   jax.ShapeDtypeStruct(state.shape, F32)],
        scratch_shapes=[pltpu.VMEM((nb, width), F32)],
        compiler_params=_cparams(("parallel", "parallel")),
        name="ssd_step",
    )(proj, xbc, xbc, xbc, dtp, w["dt_bias"], w["a_log"], w["d"], w["norm"], state)
    return y, st


def _dn_step_kernel(q_ref, k_ref, v_ref, z_ref, ba_ref, alog_ref, dtb_ref, nw_ref, st_ref,
                    o_ref, so_ref, obuf):
    nb = q_ref.shape[0]
    hd = DN_HEAD
    qq = q_ref[...]
    kk = k_ref[...]
    qq = qq * lax.rsqrt(jnp.sum(qq * qq, axis=-1, keepdims=True) + NORM_EPS) * (hd ** -0.5)
    kk = kk * lax.rsqrt(jnp.sum(kk * kk, axis=-1, keepdims=True) + NORM_EPS)
    vv = v_ref[...]
    ba = ba_ref[...]
    beta = _sigmoid(ba)
    eg = jnp.exp(-jnp.exp(alog_ref[...]) * _softplus(ba + dtb_ref[...]))
    k_t = _tr(kk)
    q_t = _tr(qq)
    for bi in range(nb):
        kcol = k_t[:, bi:bi + 1]
        qcol = q_t[:, bi:bi + 1]
        for j in range(2):
            s_old = st_ref[bi, j]
            b = beta[bi:bi + 1, j:j + 1]
            e = eg[bi:bi + 1, 2 + j:3 + j]
            ks = jnp.sum(kcol * s_old, axis=0, keepdims=True)
            u = b * vv[bi:bi + 1, hd * j:hd * (j + 1)] - (b * e) * ks
            s_new = s_old * e + kcol * u
            so_ref[bi, j] = s_new
            obuf[bi:bi + 1, hd * j:hd * (j + 1)] = jnp.sum(qcol * s_new, axis=0, keepdims=True)
    for j in range(2):
        o = obuf[:, hd * j:hd * (j + 1)]
        on = _rms(o) * nw_ref[...]
        o_ref[:, hd * j:hd * (j + 1)] = (on * _silu(z_ref[:, hd * j:hd * (j + 1)])).astype(BF16)


def _dn_step(proj, qkv, ba, w, state):
    m = proj.shape[0]
    hk, hd = w["hk"], DN_HEAD
    nb = _pick(m, (STEP_BATCH,))
    v0 = hk
    z0 = 2 * hk
    in_specs = [
        pl.BlockSpec((nb, hd), lambda i, h: (i, h)),
        pl.BlockSpec((nb, hd), lambda i, h: (i, hk + h)),
        pl.BlockSpec((nb, 2 * hd), lambda i, h: (i, v0 + h)),
        pl.BlockSpec((nb, 2 * hd), lambda i, h: (i, z0 + h)),
        pl.BlockSpec((nb, LANES), lambda i, h: (i, h)),
        pl.BlockSpec((1, LANES), lambda i, h: (0, h)),
        pl.BlockSpec((1, LANES), lambda i, h: (0, h)),
        pl.BlockSpec((1, hd), lambda i, h: (0, 0)),
        pl.BlockSpec((nb, 2, hd, hd), lambda i, h: (i, h, 0, 0)),
    ]
    o, st = pl.pallas_call(
        _dn_step_kernel,
        grid=(m // nb, hk),
        in_specs=in_specs,
        out_specs=[pl.BlockSpec((nb, 2 * hd), lambda i, h: (i, h)),
                   pl.BlockSpec((nb, 2, hd, hd), lambda i, h: (i, h, 0, 0))],
        out_shape=[jax.ShapeDtypeStruct((m, 2 * hk * hd), BF16),
                   jax.ShapeDtypeStruct(state.shape, F32)],
        scratch_shapes=[pltpu.VMEM((nb, 2 * hd), F32)],
        compiler_params=_cparams(("parallel", "parallel")),
        name="dn_step",
    )(qkv, qkv, qkv, proj, ba, w["a_log"], w["dt_bias"], w["norm"], state)
    return o, st


def _ret_step_kernel(q_ref, k_ref, v_ref, g_ref, cos_ref, s1_ref, s2_ref, lg_ref, st_ref, y_ref, so_ref, ybuf):
    nb = q_ref.shape[0]
    cos, s1, s2 = cos_ref[...], s1_ref[...], s2_ref[...]
    qq = _rotate(q_ref[...], cos, s1, s2)
    kk = _rotate(k_ref[...], cos, s1, s2) * (RET_DK ** -0.5)
    vv = v_ref[...]
    gamma = jnp.exp(lg_ref[0][0:1, 0:1])
    nk = RET_DK // LANES
    k_t = jnp.concatenate([_tr(kk[:, LANES * i:LANES * (i + 1)]) for i in range(nk)], axis=0)
    q_t = jnp.concatenate([_tr(qq[:, LANES * i:LANES * (i + 1)]) for i in range(nk)], axis=0)
    for bi in range(nb):
        s_new = st_ref[bi, 0] * gamma + k_t[:, bi:bi + 1] * vv[bi:bi + 1, :]
        so_ref[bi, 0] = s_new
        ybuf[bi:bi + 1, :] = jnp.sum(q_t[:, bi:bi + 1] * s_new, axis=0, keepdims=True)
    y_ref[...] = (_rms(ybuf[...]) * _silu(g_ref[...])).astype(BF16)


def _ret_step(proj, rope, lg, nheads, state):
    m = proj.shape[0]
    nb = _pick(m, (STEP_BATCH,))
    v0 = 2 * nheads * RET_DK // RET_DV
    g0 = v0 + nheads
    cos, s1, s2 = rope
    in_specs = [
        pl.BlockSpec((nb, RET_DK), lambda i, h: (i, h)),
        pl.BlockSpec((nb, RET_DK), lambda i, h: (i, nheads + h)),
        pl.BlockSpec((nb, RET_DV), lambda i, h: (i, v0 + h)),
        pl.BlockSpec((nb, RET_DV), lambda i, h: (i, g0 + h)),
        pl.BlockSpec((1, RET_DK), lambda i, h: (0, 0)),
        pl.BlockSpec((1, RET_DK), lambda i, h: (0, 0)),
        pl.BlockSpec((1, RET_DK), lambda i, h: (0, 0)),
        pl.BlockSpec((1, SUBLANES, LANES), lambda i, h: (h, 0, 0)),
        pl.BlockSpec((nb, 1, RET_DK, RET_DV), lambda i, h: (i, h, 0, 0)),
    ]
    y, st = pl.pallas_call(
        _ret_step_kernel,
        grid=(m // nb, nheads),
        in_specs=in_specs,
        out_specs=[pl.BlockSpec((nb, RET_DV), lambda i, h: (i, h)),
                   pl.BlockSpec((nb, 1, RET_DK, RET_DV), lambda i, h: (i, h, 0, 0))],
        out_shape=[jax.ShapeDtypeStruct((m, nheads * RET_DV), BF16),
                   jax.ShapeDtypeStruct(state.shape, F32)],
        scratch_shapes=[pltpu.VMEM((nb, RET_DV), F32)],
        compiler_params=_cparams(("parallel", "parallel")),
        name="ret_step",
    )(proj, proj, proj, proj, cos, s1, s2, lg, state)
    return y, st


def _pad_lanes(x):
    return jnp.pad(x, [(0, 0)] * (x.ndim - 1) + [(0, LANES - x.shape[-1])])


def _prep_ssm(w_in, conv_w, conv_b, dt_bias, a_log, d_skip, norm_w, w_out):
    d_model = w_in.shape[0]
    d_inner = w_out.shape[0]
    heads = dt_bias.shape[0]
    g = SSM_GROUPS
    hpg = heads // g
    conv_dim = conv_w.shape[1]
    main = d_inner + conv_dim
    per_group = lambda v: _pad_lanes(v.reshape(g, hpg)).reshape(1, g * LANES)
    w_dt = _pad_lanes(w_in[:, main:].reshape(d_model, g, hpg)).reshape(d_model, g * LANES)
    return dict(d_inner=d_inner, hpg=hpg, conv_dim=conv_dim,
                w_main=w_in[:, :main].astype(BF16), w_dt=w_dt.astype(BF16),
                conv_w=conv_w, conv_b=conv_b.reshape(1, conv_dim),
                dt_bias=per_group(dt_bias), a_log=per_group(a_log), d=per_group(d_skip),
                norm=norm_w.reshape(1, d_inner), w_out=w_out.astype(BF16))


def _prep_dn(w_in, conv_w, a_log, dt_bias, norm_w, w_out):
    d_model = w_in.shape[0]
    hv = a_log.shape[0]
    hk = hv // 2
    conv_dim = conv_w.shape[1]
    v_dim = hv * DN_HEAD
    main = conv_dim + v_dim
    wb = w_in[:, main:main + hv].reshape(d_model, hk, 2)
    wa = w_in[:, main + hv:main + 2 * hv].reshape(d_model, hk, 2)
    w_ba = _pad_lanes(jnp.concatenate([wb, wa], axis=-1)).reshape(d_model, hk * LANES)
    per_head = lambda v: _pad_lanes(jnp.concatenate([jnp.zeros((hk, 2), F32), v.reshape(hk, 2)], axis=-1)
                                    ).reshape(1, hk * LANES)
    return dict(hk=hk, conv_dim=conv_dim, w_main=w_in[:, :main].astype(BF16), w_ba=w_ba.astype(BF16),
                conv_w=conv_w, a_log=per_head(a_log), dt_bias=per_head(dt_bias),
                norm=norm_w.reshape(1, DN_HEAD), w_out=w_out.astype(BF16))


def _rope_tables(pos):
    half = RET_DK // 2
    inv = 1.0 / (RET_ROPE_BASE ** jnp.linspace(0.0, 1.0, half, dtype=F32))
    ang = pos.astype(F32)[:, None] * inv[None, :]
    cos, sin, zero = jnp.cos(ang), jnp.sin(ang), jnp.zeros_like(ang)
    inter = lambda a, b: jnp.stack([a, b], axis=-1).reshape(pos.shape[0], RET_DK)
    return inter(cos, cos), inter(-sin, zero), inter(zero, sin)


def _forward(x, p, prompt, pos, states, prm, wts):
    batch, seq, d_model = x.shape
    m = batch * seq
    h = x.reshape(m, d_model)
    depth = prm["norm_mix_pre"].shape[0]
    ssm_s, ssm_c, dn_s, dn_c, ret_s = states
    o_ssm, o_ssm_c, o_dn, o_dn_c, o_ret = [], [], [], [], []
    for i in range(depth):
        kind, j = i % 3, i // 3
        gain = prm["norm_mix_pre"][i]
        if kind == 0:
            w = wts["ssm"][j]
            d_inner, conv_dim = w["d_inner"], w["conv_dim"]
            proj = _norm_mm(h, gain, w["w_main"])
            dtp = _norm_mm(h, gain, w["w_dt"])
            raw = proj[:, d_inner:]
            if prompt:
                y, st = _ssd_prompt(proj, dtp, w, batch, seq)
                c_new = raw.reshape(batch, seq, conv_dim)[:, seq - (CONV_W - 1):]
            else:
                xbc = _conv_step(proj, d_inner, conv_dim, ssm_c[j], w["conv_w"], w["conv_b"])
                y, st = _ssd_step(proj, xbc, dtp, w, ssm_s[j])
                c_new = jnp.concatenate([ssm_c[j][:, 1:], raw[:, None, :]], axis=1)
            o_ssm.append(st)
            o_ssm_c.append(c_new)
        elif kind == 1:
            w = wts["dn"][j]
            conv_dim = w["conv_dim"]
            proj = _norm_mm(h, gain, w["w_main"])
            ba = _norm_mm(h, gain, w["w_ba"])
            raw = proj[:, :conv_dim]
            if prompt:
                y, st = _dn_prompt(proj, ba, w, batch, seq)
                c_new = raw.reshape(batch, seq, conv_dim)[:, seq - (CONV_W - 1):]
            else:
                qkv = _conv_step(proj, 0, conv_dim, dn_c[j], w["conv_w"], None)
                y, st = _dn_step(proj, qkv, ba, w, dn_s[j])
                c_new = jnp.concatenate([dn_c[j][:, 1:], raw[:, None, :]], axis=1)
            o_dn.append(st)
            o_dn_c.append(c_new)
        else:
            w = wts["ret"][j]
            proj = _norm_mm(h, gain, w["w_in"])
            if prompt:
                y, st = _ret_prompt(proj, wts["rope"], wts["lg"], w["heads"], batch, seq)
            else:
                y, st = _ret_step(proj, wts["rope"], wts["lg"], w["heads"], ret_s[j])
            o_ret.append(st)
        h = _mm_out(y, w["w_out"], h, prm["norm_mix_post"][i])
        act = _ffn_in(h, prm["norm_ffn_pre"][i], wts["ffn_gate"][i], wts["ffn_up"][i])
        h = _mm_out(act, wts["ffn_down"][i], h, prm["norm_ffn_post"][i])
        h = _ple(h, prm["norm_ple"][i], wts["ple_gate"][i], p[i].reshape(m, -1), wts["ple_proj"][i])
    return (h.reshape(batch, seq, d_model), jnp.stack(o_ssm), jnp.stack(o_ssm_c), jnp.stack(o_dn),
            jnp.stack(o_dn_c), jnp.stack(o_ret))


def kernel(x_prompt, x_sample, state_ssm, state_ssm_conv, state_delta, state_delta_conv, state_ret, p_prompt, p_sample, norm_mix_pre, norm_mix_post, norm_ffn_pre, norm_ffn_post, norm_ple, ffn_w_gate, ffn_w_up, ffn_w_down, ple_w_proj, ple_w_gate, ssm_w_in, ssm_conv_w, ssm_conv_b, ssm_dt_bias, ssm_a_log, ssm_d, ssm_norm, ssm_w_out, dn_w_in, dn_conv_w, dn_a_log, dn_dt_bias, dn_norm, dn_w_out, ret_w_in, ret_w_out):
    assert x_sample.shape[1] == 1, "the sample group advances one token per sequence"
    prm = dict(norm_mix_pre=norm_mix_pre, norm_mix_post=norm_mix_post, norm_ffn_pre=norm_ffn_pre,
               norm_ffn_post=norm_ffn_post, norm_ple=norm_ple)
    ret_heads = ret_w_out.shape[1] // RET_DV
    log_gamma = jnp.log(1.0 - 2.0 ** (-5.0 - jnp.arange(ret_heads, dtype=F32)))
    base = dict(
        ssm=[_prep_ssm(ssm_w_in[j], ssm_conv_w[j], ssm_conv_b[j], ssm_dt_bias[j], ssm_a_log[j], ssm_d[j],
                       ssm_norm[j], ssm_w_out[j]) for j in range(ssm_w_in.shape[0])],
        dn=[_prep_dn(dn_w_in[j], dn_conv_w[j], dn_a_log[j], dn_dt_bias[j], dn_norm[j], dn_w_out[j])
            for j in range(dn_w_in.shape[0])],
        ret=[dict(heads=ret_heads, w_in=ret_w_in[j].astype(BF16), w_out=ret_w_out[j].astype(BF16))
             for j in range(ret_w_in.shape[0])],
        ffn_gate=ffn_w_gate.astype(BF16), ffn_up=ffn_w_up.astype(BF16), ffn_down=ffn_w_down.astype(BF16),
        ple_gate=ple_w_gate.astype(BF16), ple_proj=ple_w_proj.astype(BF16),
        lg=jnp.broadcast_to(log_gamma[:, None, None], (ret_heads, SUBLANES, LANES)),
    )
    seq = x_prompt.shape[1]
    pos_prompt = jnp.arange(seq, dtype=jnp.int32)
    pos_sample = PAST_LEN + jnp.arange(1, dtype=jnp.int32)
    out_p = _forward(x_prompt, p_prompt, True, pos_prompt, (None,) * 5, prm,
                     dict(base, rope=_rope_tables(pos_prompt)))
    out_s = _forward(x_sample, p_sample, False, pos_sample,
                     (state_ssm, state_ssm_conv, state_delta, state_delta_conv, state_ret), prm,
                     dict(base, rope=_rope_tables(pos_sample)))
    return (out_p[0], out_s[0]) + out_p[1:] + out_s[1:]
```

```python
import functools

import jax
import jax.numpy as jnp
from jax import lax
from jax.experimental import pallas as pl
from jax.experimental.pallas import tpu as pltpu

F32 = jnp.float32
BF16 = jnp.bfloat16

NORM_EPS = 1e-6
PAST_LEN = 16384
CONV_W = 4
SSM_GROUPS = 8
SSM_HEAD_DIM = 64
SSM_D_STATE = 128
DN_HEAD = 128
RET_DK = 256
RET_DV = 512
RET_ROPE_BASE = 10000.0

LANES = 128
SUBLANES = 8
NEG_BIG = -1e30
SSD_CHUNK = 128
RET_CHUNK = 128
DN_CHUNK = 64
DN_HEADS_PER_STEP = 4
STEP_BATCH = 8
VMEM_LIMIT = 52 * 1024 * 1024


def _pick(n, prefs):
    for p in prefs:
        if n % p == 0:
            return p
    return n


def _cparams(sem):
    return pltpu.CompilerParams(dimension_semantics=sem, vmem_limit_bytes=VMEM_LIMIT)


def _sigmoid(x):
    return 1.0 / (1.0 + jnp.exp(-x))


def _silu(x):
    return x * _sigmoid(x)


def _softplus(x):
    return jnp.maximum(x, 0.0) + jnp.log1p(jnp.exp(-jnp.abs(x)))


def _dot(a, b):
    return jnp.dot(a.astype(BF16), b.astype(BF16), preferred_element_type=F32)


def _dot_nt(a, b):
    return lax.dot_general(a.astype(BF16), b.astype(BF16), (((1,), (1,)), ((), ())),
                           preferred_element_type=F32)


def _dot01(m01, x):
    hi = x.astype(BF16)
    r = x - hi.astype(F32)
    mid = r.astype(BF16)
    lo = (r - mid.astype(F32)).astype(BF16)
    out = jnp.dot(m01, hi, preferred_element_type=F32)
    out = out + jnp.dot(m01, mid, preferred_element_type=F32)
    return out + jnp.dot(m01, lo, preferred_element_type=F32)


def _tr(x):
    r, c = x.shape
    assert c == LANES and r <= LANES
    if r < LANES:
        x = jnp.concatenate([x, jnp.zeros((LANES - r, c), x.dtype)], axis=0)
    return x.T[:, :r]


def _expand64(x, nheads):
    rows = x.shape[0]
    lane = lax.broadcasted_iota(jnp.int32, (rows, LANES), 1)
    parts = []
    for j in range(nheads // 2):
        a = jnp.broadcast_to(x[:, 2 * j:2 * j + 1], (rows, LANES))
        b = jnp.broadcast_to(x[:, 2 * j + 1:2 * j + 2], (rows, LANES))
        parts.append(jnp.where(lane < 64, a, b))
    return parts[0] if len(parts) == 1 else jnp.concatenate(parts, axis=1)


def _rms(y):
    return y * lax.rsqrt(jnp.mean(y * y, axis=-1, keepdims=True) + NORM_EPS)


def _conv_chunk(buf_ref, col0, x, w, bias):
    q, width = x.shape
    cols = slice(col0, col0 + width)
    buf_ref[SUBLANES:SUBLANES + q, cols] = x
    acc = x * w[CONV_W - 1:CONV_W, :]
    for s in range(1, CONV_W):
        acc = acc + buf_ref[SUBLANES - s:SUBLANES - s + q, cols] * w[CONV_W - 1 - s:CONV_W - s, :]
    buf_ref[0:SUBLANES, cols] = x[q - SUBLANES:q, :]
    if bias is not None:
        acc = acc + bias
    return acc


def _bdot(a, b):
    return lax.dot_general(a.astype(BF16), b.astype(BF16), (((2,), (1,)), ((0,), (0,))),
                           preferred_element_type=F32)


def _bdot_nt(a, b):
    return lax.dot_general(a.astype(BF16), b.astype(BF16), (((2,), (2,)), ((0,), (0,))),
                           preferred_element_type=F32)


def _tri_inv(a, row, col):
    q = a.shape[-1]
    eye = (row == col).astype(F32)[None]
    d = eye - jnp.where((jnp.right_shift(row, 1) == jnp.right_shift(col, 1))[None], a, 0.0)
    sh = 1
    while (1 << sh) < q:
        same_big = jnp.right_shift(row, sh + 1) == jnp.right_shift(col, sh + 1)
        diff_small = jnp.right_shift(row, sh) != jnp.right_shift(col, sh)
        lb = jnp.where((same_big & diff_small)[None], a, 0.0)
        d = d - _bdot(d, _bdot(lb, d))
        sh += 1
    return d


def _norm_mm_kernel(x_ref, g_ref, w_ref, o_ref, xn_ref):
    @pl.when(pl.program_id(1) == 0)
    def _():
        xn_ref[...] = (_rms(x_ref[...]) * g_ref[...]).astype(BF16)

    o_ref[...] = jnp.dot(xn_ref[...], w_ref[0], preferred_element_type=F32)


def _norm_mm(x, gain, w, layer):
    m, k = x.shape
    n = w.shape[2]
    tm = _pick(m, (1024, 512, 256, 128))
    tn = _pick(n, (512, 256, 128))
    return pl.pallas_call(
        _norm_mm_kernel,
        grid=(m // tm, n // tn),
        in_specs=[pl.BlockSpec((tm, k), lambda i, j: (i, 0)),
                  pl.BlockSpec((1, k), lambda i, j: (0, 0)),
                  pl.BlockSpec((1, k, tn), lambda i, j: (layer, 0, j))],
        out_specs=pl.BlockSpec((tm, tn), lambda i, j: (i, j)),
        out_shape=jax.ShapeDtypeStruct((m, n), F32),
        scratch_shapes=[pltpu.VMEM((tm, k), BF16)],
        compiler_params=_cparams(("parallel", "arbitrary")),
        name="norm_mm",
    )(x, gain.reshape(1, k), w)


def _mm_out_kernel(a_ref, w_ref, h_ref, g_ref, o_ref, acc_ref):
    kk = pl.program_id(1)

    @pl.when(kk == 0)
    def _():
        acc_ref[...] = jnp.zeros_like(acc_ref)

    acc_ref[...] += jnp.dot(a_ref[...], w_ref[0], preferred_element_type=F32)

    @pl.when(kk == pl.num_programs(1) - 1)
    def _():
        o_ref[...] = h_ref[...] + _rms(acc_ref[...]) * g_ref[...]


def _mm_out(a, w, layer, h, gain):
    m, k = a.shape
    n = w.shape[2]
    tm = _pick(m, (512, 256, 128))
    tk = _pick(k, (512, 256, 128))
    return pl.pallas_call(
        _mm_out_kernel,
        grid=(m // tm, k // tk),
        in_specs=[pl.BlockSpec((tm, tk), lambda i, j: (i, j)),
                  pl.BlockSpec((1, tk, n), lambda i, j: (layer, j, 0)),
                  pl.BlockSpec((tm, n), lambda i, j: (i, 0)),
                  pl.BlockSpec((1, n), lambda i, j: (0, 0))],
        out_specs=pl.BlockSpec((tm, n), lambda i, j: (i, 0)),
        out_shape=jax.ShapeDtypeStruct((m, n), F32),
        scratch_shapes=[pltpu.VMEM((tm, n), F32)],
        compiler_params=_cparams(("parallel", "arbitrary")),
        name="mm_out",
    )(a, w, h, gain.reshape(1, n))


def _ffn_in_kernel(x_ref, g_ref, wg_ref, wu_ref, o_ref, xn_ref):
    @pl.when(pl.program_id(1) == 0)
    def _():
        xn_ref[...] = (_rms(x_ref[...]) * g_ref[...]).astype(BF16)

    xn = xn_ref[...]
    gate = jnp.dot(xn, wg_ref[0], preferred_element_type=F32)
    up = jnp.dot(xn, wu_ref[0], preferred_element_type=F32)
    o_ref[...] = (_silu(gate) * up).astype(BF16)


def _ffn_in(x, gain, wg, wu, layer):
    m, k = x.shape
    n = wg.shape[2]
    tm = _pick(m, (1024, 512, 256, 128))
    tn = _pick(n, (512, 256, 128))
    return pl.pallas_call(
        _ffn_in_kernel,
        grid=(m // tm, n // tn),
        in_specs=[pl.BlockSpec((tm, k), lambda i, j: (i, 0)),
                  pl.BlockSpec((1, k), lambda i, j: (0, 0)),
                  pl.BlockSpec((1, k, tn), lambda i, j: (layer, 0, j)),
                  pl.BlockSpec((1, k, tn), lambda i, j: (layer, 0, j))],
        out_specs=pl.BlockSpec((tm, tn), lambda i, j: (i, j)),
        out_shape=jax.ShapeDtypeStruct((m, n), BF16),
        scratch_shapes=[pltpu.VMEM((tm, k), BF16)],
        compiler_params=_cparams(("parallel", "arbitrary")),
        name="ffn_in",
    )(x, gain.reshape(1, k), wg, wu)


def _ple_kernel(h_ref, hc_ref, g_ref, wg_ref, p_ref, wp_ref, o_ref, xn_ref):
    @pl.when(pl.program_id(1) == 0)
    def _():
        xn_ref[...] = (_rms(h_ref[...]) * g_ref[...]).astype(BF16)

    gate = _sigmoid(jnp.dot(xn_ref[...], wg_ref[0], preferred_element_type=F32))
    proj = jnp.dot(p_ref[0].astype(BF16), wp_ref[0], preferred_element_type=F32)
    o_ref[...] = hc_ref[...] + proj * gate


def _ple(h, gain, wg, p, wp, layer):
    m, k = h.shape
    n = wg.shape[2]
    pd = p.shape[2]
    tm = _pick(m, (1024, 512, 256, 128))
    tn = _pick(n, (512, 256, 128))
    return pl.pallas_call(
        _ple_kernel,
        grid=(m // tm, n // tn),
        in_specs=[pl.BlockSpec((tm, k), lambda i, j: (i, 0)),
                  pl.BlockSpec((tm, tn), lambda i, j: (i, j)),
                  pl.BlockSpec((1, k), lambda i, j: (0, 0)),
                  pl.BlockSpec((1, k, tn), lambda i, j: (layer, 0, j)),
                  pl.BlockSpec((1, tm, pd), lambda i, j: (layer, i, 0)),
                  pl.BlockSpec((1, pd, tn), lambda i, j: (layer, 0, j))],
        out_specs=pl.BlockSpec((tm, tn), lambda i, j: (i, j)),
        out_shape=jax.ShapeDtypeStruct((m, n), F32),
        scratch_shapes=[pltpu.VMEM((tm, k), BF16)],
        compiler_params=_cparams(("parallel", "arbitrary")),
        name="ple",
    )(h, h, gain.reshape(1, k), wg, p, wp)


def _ssd_prompt_kernel(z_ref, xs_ref, b_ref, c_ref, dt_ref, wx_ref, wb_ref, wc_ref, bx_ref, bb_ref, bc_ref,
                       dtb_ref, alog_ref, d_ref, nw_ref, y_ref, st_ref, s_scr, cbuf, *, nheads):
    c = pl.program_id(2)
    q, width = xs_ref.shape
    n = b_ref.shape[1]
    hd = SSM_HEAD_DIM

    @pl.when(c == 0)
    def _():
        s_scr[...] = jnp.zeros_like(s_scr)
        cbuf[0:SUBLANES, :] = jnp.zeros((SUBLANES, cbuf.shape[1]), F32)

    xs = _silu(_conv_chunk(cbuf, 0, xs_ref[...], wx_ref[...], bx_ref[...]))
    bm = _silu(_conv_chunk(cbuf, width, b_ref[...], wb_ref[...], bb_ref[...]))
    cm = _silu(_conv_chunk(cbuf, width + n, c_ref[...], wc_ref[...], bc_ref[...]))

    dtv = _softplus(dt_ref[...] + dtb_ref[...])
    la = dtv * (-jnp.exp(alog_ref[...]))
    row = lax.broadcasted_iota(jnp.int32, (q, q), 0)
    col = lax.broadcasted_iota(jnp.int32, (q, q), 1)
    causal = row >= col
    acs = _dot01(causal.astype(BF16), la)
    acs_t = _tr(acs)
    scores = _dot_nt(cm, bm)
    v_all = xs * _expand64(dtv, nheads)
    s_old = s_scr[...]
    y = _dot_nt(cm, s_old) * _expand64(jnp.exp(acs), nheads)
    lane = lax.broadcasted_iota(jnp.int32, (q, LANES), 1)
    parts = []
    for j in range(nheads // 2):
        vp = v_all[:, LANES * j:LANES * (j + 1)].astype(BF16)
        ys = []
        for t in range(2):
            r = 2 * j + t
            seg = acs[:, r:r + 1] - acs_t[r:r + 1, :]
            decay = jnp.exp(jnp.where(causal, seg, NEG_BIG))
            ys.append(jnp.dot((scores * decay).astype(BF16), vp, preferred_element_type=F32))
        parts.append(jnp.where(lane < 64, ys[0], ys[1]))
    y = y + (parts[0] if len(parts) == 1 else jnp.concatenate(parts, axis=1))
    y = y + xs * _expand64(d_ref[...], nheads)

    last = acs[q - 1:q, :]
    wv = v_all * _expand64(jnp.exp(last - acs), nheads)
    wv_t = [_tr(wv[:, LANES * j:LANES * (j + 1)]) for j in range(width // LANES)]
    wv_t = wv_t[0] if len(wv_t) == 1 else jnp.concatenate(wv_t, axis=0)
    e_last = jnp.exp(last)
    e_rows = jnp.concatenate([jnp.broadcast_to(e_last[:, r:r + 1], (hd, n)) for r in range(nheads)], axis=0)
    s_new = s_old * e_rows + _dot(wv_t, bm)
    s_scr[...] = s_new

    y = y * _silu(z_ref[...])
    y_ref[...] = (_rms(y) * nw_ref[...]).astype(BF16)

    @pl.when(c == pl.num_programs(2) - 1)
    def _():
        for r in range(nheads):
            st_ref[0, r] = s_new[hd * r:hd * (r + 1), :]


def _ssd_prompt(proj, w, batch, seq):
    d_inner, nh, g, n = w["d_inner"], w["hpg"], SSM_GROUPS, SSM_D_STATE
    width = nh * SSM_HEAD_DIM
    q = _pick(seq, (SSD_CHUNK, 64, 32, 16, 8))
    nc = seq // q
    xs0, b0, c0 = d_inner // width, 2 * d_inner // n, 2 * d_inner // n + g
    dt0 = (d_inner + w["conv_dim"]) // LANES
    cw0 = d_inner // n
    tok = lambda b, gi, c: b * nc + c
    in_specs = [
        pl.BlockSpec((q, width), lambda b, gi, c: (tok(b, gi, c), gi)),
        pl.BlockSpec((q, width), lambda b, gi, c: (tok(b, gi, c), xs0 + gi)),
        pl.BlockSpec((q, n), lambda b, gi, c: (tok(b, gi, c), b0 + gi)),
        pl.BlockSpec((q, n), lambda b, gi, c: (tok(b, gi, c), c0 + gi)),
        pl.BlockSpec((q, LANES), lambda b, gi, c: (tok(b, gi, c), dt0 + gi)),
        pl.BlockSpec((CONV_W, width), lambda b, gi, c: (0, gi)),
        pl.BlockSpec((CONV_W, n), lambda b, gi, c: (0, cw0 + gi)),
        pl.BlockSpec((CONV_W, n), lambda b, gi, c: (0, cw0 + g + gi)),
        pl.BlockSpec((1, width), lambda b, gi, c: (0, gi)),
        pl.BlockSpec((1, n), lambda b, gi, c: (0, cw0 + gi)),
        pl.BlockSpec((1, n), lambda b, gi, c: (0, cw0 + g + gi)),
        pl.BlockSpec((1, LANES), lambda b, gi, c: (0, gi)),
        pl.BlockSpec((1, LANES), lambda b, gi, c: (0, gi)),
        pl.BlockSpec((1, LANES), lambda b, gi, c: (0, gi)),
        pl.BlockSpec((1, width), lambda b, gi, c: (0, gi)),
    ]
    y, st = pl.pallas_call(
        functools.partial(_ssd_prompt_kernel, nheads=nh),
        grid=(batch, g, nc),
        in_specs=in_specs,
        out_specs=[pl.BlockSpec((q, width), lambda b, gi, c: (tok(b, gi, c), gi)),
                   pl.BlockSpec((1, nh, SSM_HEAD_DIM, n), lambda b, gi, c: (b, gi, 0, 0))],
        out_shape=[jax.ShapeDtypeStruct((batch * seq, d_inner), BF16),
                   jax.ShapeDtypeStruct((batch, g * nh, SSM_HEAD_DIM, n), F32)],
        scratch_shapes=[pltpu.VMEM((width, n), F32),
                        pltpu.VMEM((q + SUBLANES, width + 2 * n), F32)],
        compiler_params=_cparams(("parallel", "parallel", "arbitrary")),
        name="ssd_prompt",
    )(proj, proj, proj, proj, proj, w["conv_w"], w["conv_w"], w["conv_w"], w["conv_b"], w["conv_b"], w["conv_b"],
      w["dt_bias"], w["a_log"], w["d"], w["norm"])
    return y, st


def _dn_prompt_kernel(q_ref, k_ref, v_ref, z_ref, ba_ref, wq_ref, wk_ref, wv_ref, alog_ref, dtb_ref, nw_ref,
                      o_ref, st_ref, s_scr, cbuf, *, hb):
    c = pl.program_id(2)
    q = q_ref.shape[0]
    hd = DN_HEAD

    @pl.when(c == 0)
    def _():
        s_scr[...] = jnp.zeros_like(s_scr)
        cbuf[0:SUBLANES, :] = jnp.zeros((SUBLANES, cbuf.shape[1]), F32)

    qa = _silu(_conv_chunk(cbuf, 0, q_ref[...], wq_ref[...], None))
    ka = _silu(_conv_chunk(cbuf, hb * hd, k_ref[...], wk_ref[...], None))
    va = _silu(_conv_chunk(cbuf, 2 * hb * hd, v_ref[...], wv_ref[...], None))

    ba = ba_ref[...]
    beta_all = _sigmoid(ba)
    gg = -jnp.exp(alog_ref[...]) * _softplus(ba + dtb_ref[...])
    row = lax.broadcasted_iota(jnp.int32, (q, q), 0)
    col = lax.broadcasted_iota(jnp.int32, (q, q), 1)
    incl = row >= col
    strict = row > col
    acs_all = _dot01(incl.astype(BF16), gg)

    heads = [(kh, j) for kh in range(hb) for j in range(2)]
    q3 = jnp.stack([qa[:, kh * hd:(kh + 1) * hd] for kh in range(hb)])
    k3 = jnp.stack([ka[:, kh * hd:(kh + 1) * hd] for kh in range(hb)])
    q3 = q3 * lax.rsqrt(jnp.sum(q3 * q3, axis=-1, keepdims=True) + NORM_EPS) * (hd ** -0.5)
    k3 = k3 * lax.rsqrt(jnp.sum(k3 * k3, axis=-1, keepdims=True) + NORM_EPS)
    acs_k = [acs_all[:, kh * LANES:(kh + 1) * LANES] for kh in range(hb)]
    acs_t = [_tr(a) for a in acs_k]
    kk3 = _bdot_nt(k3, k3)
    qk3 = _bdot_nt(q3, k3)
    k_t3 = jnp.stack([_tr(k3[kh]) for kh in range(hb)])

    a_col = jnp.stack([acs_k[kh][:, 2 + j:3 + j] for kh, j in heads])
    a_row = jnp.stack([acs_t[kh][2 + j:3 + j, :] for kh, j in heads])
    last = jnp.stack([acs_k[kh][q - 1:q, 2 + j:3 + j] for kh, j in heads])
    bcol = jnp.stack([beta_all[:, kh * LANES + j:kh * LANES + j + 1] for kh, j in heads])
    rep = lambda x: jnp.stack([x[kh] for kh, _ in heads])
    k_v, q_v = rep(k3), rep(q3)
    gam = jnp.exp(jnp.where(incl[None], a_col - a_row, NEG_BIG))
    a_mat = jnp.where(strict[None], rep(kk3) * gam * bcol, 0.0)
    t_mat = _tri_inv(a_mat, row, col)
    v3 = jnp.stack([va[:, i * hd:(i + 1) * hd] for i in range(len(heads))])
    e_col = jnp.exp(a_col)
    rhs = jnp.concatenate([bcol * v3, (bcol * e_col) * k_v], axis=2)
    sol = _bdot(t_mat, rhs)
    s_old = s_scr[...]
    u = sol[:, :, :hd] - _bdot(sol[:, :, hd:], s_old)
    o = _bdot(rep(qk3) * gam, u) + _bdot(q_v, s_old) * e_col
    s_scr[...] = s_old * jnp.exp(last) + _bdot(rep(k_t3), jnp.exp(last - a_col) * u)
    on = _rms(o) * nw_ref[...]
    for i in range(len(heads)):
        hs = slice(i * hd, (i + 1) * hd)
        o_ref[:, hs] = (on[i] * _silu(z_ref[:, hs])).astype(BF16)

    @pl.when(c == pl.num_programs(2) - 1)
    def _():
        st_ref[0] = s_scr[...]


def _dn_prompt(proj, w, batch, seq):
    hk, hd = w["hk"], DN_HEAD
    hb = _pick(hk, (DN_HEADS_PER_STEP, 2, 1))
    q = _pick(seq, (DN_CHUNK, 32, 16, 8))
    nc = seq // q
    nhb = hk // hb
    tok = lambda b, h, c: b * nc + c
    ba0 = (6 * hk * hd) // (hb * LANES)
    in_specs = [
        pl.BlockSpec((q, hb * hd), lambda b, h, c: (tok(b, h, c), h)),
        pl.BlockSpec((q, hb * hd), lambda b, h, c: (tok(b, h, c), nhb + h)),
        pl.BlockSpec((q, 2 * hb * hd), lambda b, h, c: (tok(b, h, c), nhb + h)),
        pl.BlockSpec((q, 2 * hb * hd), lambda b, h, c: (tok(b, h, c), 2 * nhb + h)),
        pl.BlockSpec((q, hb * LANES), lambda b, h, c: (tok(b, h, c), ba0 + h)),
        pl.BlockSpec((CONV_W, hb * hd), lambda b, h, c: (0, h)),
        pl.BlockSpec((CONV_W, hb * hd), lambda b, h, c: (0, nhb + h)),
        pl.BlockSpec((CONV_W, 2 * hb * hd), lambda b, h, c: (0, nhb + h)),
        pl.BlockSpec((1, hb * LANES), lambda b, h, c: (0, h)),
        pl.BlockSpec((1, hb * LANES), lambda b, h, c: (0, h)),
        pl.BlockSpec((1, hd), lambda b, h, c: (0, 0)),
    ]
    o, st = pl.pallas_call(
        functools.partial(_dn_prompt_kernel, hb=hb),
        grid=(batch, nhb, nc),
        in_specs=in_specs,
        out_specs=[pl.BlockSpec((q, 2 * hb * hd), lambda b, h, c: (tok(b, h, c), h)),
                   pl.BlockSpec((1, 2 * hb, hd, hd), lambda b, h, c: (b, h, 0, 0))],
        out_shape=[jax.ShapeDtypeStruct((batch * seq, 2 * hk * hd), BF16),
                   jax.ShapeDtypeStruct((batch, 2 * hk, hd, hd), F32)],
        scratch_shapes=[pltpu.VMEM((2 * hb, hd, hd), F32),
                        pltpu.VMEM((q + SUBLANES, 4 * hb * hd), F32)],
        compiler_params=_cparams(("parallel", "parallel", "arbitrary")),
        name="dn_prompt",
    )(proj, proj, proj, proj, proj, w["conv_w"], w["conv_w"], w["conv_w"], w["a_log"], w["dt_bias"], w["norm"])
    return o, st


def _rotate(x, cos, s1, s2):
    w = x.shape[1]
    return x * cos + pltpu.roll(x, w - 1, 1) * s1 + pltpu.roll(x, 1, 1) * s2


def _ret_prompt_kernel(q_ref, k_ref, v_ref, g_ref, cos_ref, s1_ref, s2_ref, lg_ref, y_ref, st_ref, s_scr):
    c = pl.program_id(2)
    q = q_ref.shape[0]

    @pl.when(c == 0)
    def _():
        s_scr[...] = jnp.zeros_like(s_scr)

    cos, s1, s2 = cos_ref[...], s1_ref[...], s2_ref[...]
    qq = _rotate(q_ref[...], cos, s1, s2)
    kk = _rotate(k_ref[...], cos, s1, s2) * (RET_DK ** -0.5)
    vv = v_ref[...]
    lg = lg_ref[0][0:1, 0:1]
    row = lax.broadcasted_iota(jnp.int32, (q, q), 0)
    col = lax.broadcasted_iota(jnp.int32, (q, q), 1)
    causal = row >= col
    decay = jnp.exp(jnp.where(causal, (row - col).astype(F32) * lg, NEG_BIG))
    pos = lax.broadcasted_iota(jnp.int32, (q, 1), 0).astype(F32)
    scores = _dot_nt(qq, kk)
    s_old = s_scr[...]
    y = _dot(scores * decay, vv) + _dot(qq, s_old) * jnp.exp((pos + 1.0) * lg)
    wv = jnp.exp((float(q - 1) - pos) * lg) * vv
    k_t = jnp.concatenate([_tr(kk[:, LANES * i:LANES * (i + 1)]) for i in range(RET_DK // LANES)], axis=0)
    s_new = s_old * jnp.exp(float(q) * lg) + _dot(k_t, wv)
    s_scr[...] = s_new
    y_ref[...] = (_rms(y) * _silu(g_ref[...])).astype(BF16)

    @pl.when(c == pl.num_programs(2) - 1)
    def _():
        st_ref[0, 0] = s_new


def _ret_prompt(proj, rope, lg, nheads, batch, seq):
    q = _pick(seq, (RET_CHUNK, 64, 32, 16, 8))
    nc = seq // q
    tok = lambda b, h, c: b * nc + c
    v0 = 2 * nheads * RET_DK // RET_DV
    g0 = v0 + nheads
    cos, s1, s2 = rope
    in_specs = [
        pl.BlockSpec((q, RET_DK), lambda b, h, c: (tok(b, h, c), h)),
        pl.BlockSpec((q, RET_DK), lambda b, h, c: (tok(b, h, c), nheads + h)),
        pl.BlockSpec((q, RET_DV), lambda b, h, c: (tok(b, h, c), v0 + h)),
        pl.BlockSpec((q, RET_DV), lambda b, h, c: (tok(b, h, c), g0 + h)),
        pl.BlockSpec((q, RET_DK), lambda b, h, c: (c, 0)),
        pl.BlockSpec((q, RET_DK), lambda b, h, c: (c, 0)),
        pl.BlockSpec((q, RET_DK), lambda b, h, c: (c, 0)),
        pl.BlockSpec((1, SUBLANES, LANES), lambda b, h, c: (h, 0, 0)),
    ]
    y, st = pl.pallas_call(
        _ret_prompt_kernel,
        grid=(batch, nheads, nc),
        in_specs=in_specs,
        out_specs=[pl.BlockSpec((q, RET_DV), lambda b, h, c: (tok(b, h, c), h)),
                   pl.BlockSpec((1, 1, RET_DK, RET_DV), lambda b, h, c: (b, h, 0, 0))],
        out_shape=[jax.ShapeDtypeStruct((batch * seq, nheads * RET_DV), BF16),
                   jax.ShapeDtypeStruct((batch, nheads, RET_DK, RET_DV), F32)],
        scratch_shapes=[pltpu.VMEM((RET_DK, RET_DV), F32)],
        compiler_params=_cparams(("parallel", "parallel", "arbitrary")),
        name="ret_prompt",
    )(proj, proj, proj, proj, cos, s1, s2, lg)
    return y, st


def _alias_args(prev, n_in, out_idx):
    if prev is None:
        return [], [], {}
    return [pl.BlockSpec(memory_space=pl.ANY)], [prev], {n_in: out_idx}


def _conv_step_kernel(*refs, has_bias, aliased):
    refs = list(refs)
    x_ref, cs_ref, w_ref = refs[:3]
    b_ref = refs[3] if has_bias else None
    o_ref, cn_ref = refs[3 + int(has_bias) + int(aliased):]
    w = w_ref[...]
    x = x_ref[...]
    c0, c1, c2 = cs_ref[0, 0], cs_ref[0, 1], cs_ref[0, 2]
    acc = c0 * w[0:1, :] + c1 * w[1:2, :] + c2 * w[2:3, :] + x * w[3:4, :]
    if has_bias:
        acc = acc + b_ref[...]
    o_ref[...] = _silu(acc)
    cn_ref[0, 0] = c1
    cn_ref[0, 1] = c2
    cn_ref[0, 2] = x


def _conv_step(proj, col0, cstate, layer, conv_w, conv_b, prev):
    m = proj.shape[0]
    cdim = cstate.shape[3]
    cb = _pick(cdim, (512, 256, 128))
    x0 = col0 // cb
    taps = CONV_W - 1
    in_specs = [pl.BlockSpec((m, cb), lambda j: (0, x0 + j)),
                pl.BlockSpec((1, taps, m, cb), lambda j: (layer, 0, 0, j)),
                pl.BlockSpec((CONV_W, cb), lambda j: (0, j))]
    args = [proj, cstate, conv_w]
    if conv_b is not None:
        in_specs.append(pl.BlockSpec((1, cb), lambda j: (0, j)))
        args.append(conv_b)
    a_specs, a_args, aliases = _alias_args(prev, len(args), 1)
    return pl.pallas_call(
        functools.partial(_conv_step_kernel, has_bias=conv_b is not None, aliased=prev is not None),
        grid=(cdim // cb,),
        in_specs=in_specs + a_specs,
        out_specs=[pl.BlockSpec((m, cb), lambda j: (0, j)),
                   pl.BlockSpec((1, taps, m, cb), lambda j: (layer, 0, 0, j))],
        out_shape=[jax.ShapeDtypeStruct((m, cdim), F32),
                   jax.ShapeDtypeStruct(cstate.shape, F32)],
        input_output_aliases=aliases,
        compiler_params=_cparams(("parallel",)),
        name="conv_step",
    )(*args, *a_args)


def _ssd_step_kernel(*refs, nheads):
    z_ref, xs_ref, b_ref, c_ref, dt_ref, dtb_ref, alog_ref, d_ref, nw_ref, st_ref = refs[:10]
    y_ref, so_ref, ytb = refs[-3:]
    nb = xs_ref.shape[0]
    hd = SSM_HEAD_DIM
    xs = xs_ref[...]
    dtv = _softplus(dt_ref[...] + dtb_ref[...])
    decay = jnp.exp(dtv * (-jnp.exp(alog_ref[...])))
    pad = jnp.zeros((LANES - nb, LANES), F32)
    kmat = jnp.concatenate([b_ref[...], pad], axis=0).astype(BF16)
    qmat = jnp.concatenate([c_ref[...], pad], axis=0).T.astype(BF16)
    v_all = xs * _expand64(dtv, nheads)
    v_t = [jnp.concatenate([v_all[:, LANES * j:LANES * (j + 1)], pad], axis=0).T
           for j in range(nheads // 2)]
    lane = lax.broadcasted_iota(jnp.int32, (hd, LANES), 1)
    for r in range(nheads):
        vt = v_t[r // 2][hd * (r % 2):hd * (r % 2 + 1), :]
        for bi in range(nb):
            outer = jnp.dot(jnp.where(lane == bi, vt, 0.0).astype(BF16), kmat, preferred_element_type=F32)
            so_ref[0, bi, r] = st_ref[0, bi, r] * decay[bi:bi + 1, r:r + 1] + outer
    for r in range(nheads):
        acc = jnp.zeros((hd, LANES), F32)
        for bi in range(nb):
            yb = jnp.dot(so_ref[0, bi, r].astype(BF16), qmat, preferred_element_type=F32)
            acc = jnp.where(lane == bi, yb, acc)
        ytb[hd * r:hd * (r + 1), :] = acc
    y = [ytb[LANES * j:LANES * (j + 1), :].T[:nb, :] for j in range(nheads // 2)]
    y = y[0] if len(y) == 1 else jnp.concatenate(y, axis=1)
    y = y + xs * _expand64(d_ref[...], nheads)
    y = y * _silu(z_ref[...])
    y_ref[...] = (_rms(y) * nw_ref[...]).astype(BF16)


def _ssd_step(proj, xbc, w, state, layer, prev):
    m = proj.shape[0]
    d_inner, nh, g, n = w["d_inner"], w["hpg"], SSM_GROUPS, SSM_D_STATE
    width = nh * SSM_HEAD_DIM
    nb = _pick(m, (STEP_BATCH,))
    b0 = d_inner // n
    dt0 = (d_inner + w["conv_dim"]) // LANES
    st_spec = pl.BlockSpec((1, nb, nh, SSM_HEAD_DIM, n), lambda i, gi: (layer, i, gi, 0, 0))
    in_specs = [
        pl.BlockSpec((nb, width), lambda i, gi: (i, gi)),
        pl.BlockSpec((nb, width), lambda i, gi: (i, gi)),
        pl.BlockSpec((nb, n), lambda i, gi: (i, b0 + gi)),
        pl.BlockSpec((nb, n), lambda i, gi: (i, b0 + g + gi)),
        pl.BlockSpec((nb, LANES), lambda i, gi: (i, dt0 + gi)),
        pl.BlockSpec((1, LANES), lambda i, gi: (0, gi)),
        pl.BlockSpec((1, LANES), lambda i, gi: (0, gi)),
        pl.BlockSpec((1, LANES), lambda i, gi: (0, gi)),
        pl.BlockSpec((1, width), lambda i, gi: (0, gi)),
        st_spec,
    ]
    args = [proj, xbc, xbc, xbc, proj, w["dt_bias"], w["a_log"], w["d"], w["norm"], state]
    a_specs, a_args, aliases = _alias_args(prev, len(args), 1)
    y, st = pl.pallas_call(
        functools.partial(_ssd_step_kernel, nheads=nh),
        grid=(m // nb, g),
        in_specs=in_specs + a_specs,
        out_specs=[pl.BlockSpec((nb, width), lambda i, gi: (i, gi)), st_spec],
        out_shape=[jax.ShapeDtypeStruct((m, d_inner), BF16),
                   jax.ShapeDtypeStruct(state.shape, F32)],
        scratch_shapes=[pltpu.VMEM((width, LANES), F32)],
        input_output_aliases=aliases,
        compiler_params=_cparams(("parallel", "parallel")),
        name="ssd_step",
    )(*args, *a_args)
    return y, st


def _dn_step_kernel(q_ref, k_ref, v_ref, z_ref, ba_ref, alog_ref, dtb_ref, nw_ref, st_ref,
                    o_ref, so_ref, obuf):
    nb = q_ref.shape[0]
    hd = DN_HEAD
    qq = q_ref[...]
    kk = k_ref[...]
    qq = qq * lax.rsqrt(jnp.sum(qq * qq, axis=-1, keepdims=True) + NORM_EPS) * (hd ** -0.5)
    kk = kk * lax.rsqrt(jnp.sum(kk * kk, axis=-1, keepdims=True) + NORM_EPS)
    vv = v_ref[...]
    ba = ba_ref[...]
    beta = _sigmoid(ba)
    eg = jnp.exp(-jnp.exp(alog_ref[...]) * _softplus(ba + dtb_ref[...]))
    k_t = _tr(kk)
    q_t = _tr(qq)
    for bi in range(nb):
        kcol = k_t[:, bi:bi + 1]
        qcol = q_t[:, bi:bi + 1]
        for j in range(2):
            s_old = st_ref[0, bi, j]
            b = beta[bi:bi + 1, j:j + 1]
            e = eg[bi:bi + 1, 2 + j:3 + j]
            ks = jnp.sum(kcol * s_old, axis=0, keepdims=True)
            u = b * vv[bi:bi + 1, hd * j:hd * (j + 1)] - (b * e) * ks
            s_new = s_old * e + kcol * u
            so_ref[0, bi, j] = s_new
            obuf[bi:bi + 1, hd * j:hd * (j + 1)] = jnp.sum(qcol * s_new, axis=0, keepdims=True)
    for j in range(2):
        o = obuf[:, hd * j:hd * (j + 1)]
        on = _rms(o) * nw_ref[...]
        o_ref[:, hd * j:hd * (j + 1)] = (on * _silu(z_ref[:, hd * j:hd * (j + 1)])).astype(BF16)


def _dn_step(proj, qkv, w, state, layer):
    m = proj.shape[0]
    hk, hd = w["hk"], DN_HEAD
    nb = _pick(m, (2 * STEP_BATCH, STEP_BATCH))
    ba0 = (6 * hk * hd) // LANES
    st_spec = pl.BlockSpec((1, nb, 2, hd, hd), lambda i, h: (layer, i, h, 0, 0))
    in_specs = [
        pl.BlockSpec((nb, hd), lambda i, h: (i, h)),
        pl.BlockSpec((nb, hd), lambda i, h: (i, hk + h)),
        pl.BlockSpec((nb, 2 * hd), lambda i, h: (i, hk + h)),
        pl.BlockSpec((nb, 2 * hd), lambda i, h: (i, 2 * hk + h)),
        pl.BlockSpec((nb, LANES), lambda i, h: (i, ba0 + h)),
        pl.BlockSpec((1, LANES), lambda i, h: (0, h)),
        pl.BlockSpec((1, LANES), lambda i, h: (0, h)),
        pl.BlockSpec((1, hd), lambda i, h: (0, 0)),
        st_spec,
    ]
    o, st = pl.pallas_call(
        _dn_step_kernel,
        grid=(m // nb, hk),
        in_specs=in_specs,
        out_specs=[pl.BlockSpec((nb, 2 * hd), lambda i, h: (i, h)), st_spec],
        out_shape=[jax.ShapeDtypeStruct((m, 2 * hk * hd), BF16),
                   jax.ShapeDtypeStruct(state.shape, F32)],
        scratch_shapes=[pltpu.VMEM((nb, 2 * hd), F32)],
        compiler_params=_cparams(("parallel", "parallel")),
        name="dn_step",
    )(qkv, qkv, qkv, proj, proj, w["a_log"], w["dt_bias"], w["norm"], state)
    return o, st


def _ret_step_kernel(q_ref, k_ref, v_ref, g_ref, cos_ref, s1_ref, s2_ref, lg_ref, st_ref, y_ref, so_ref, ybuf):
    nb = q_ref.shape[0]
    cos, s1, s2 = cos_ref[...], s1_ref[...], s2_ref[...]
    qq = _rotate(q_ref[...], cos, s1, s2)
    kk = _rotate(k_ref[...], cos, s1, s2) * (RET_DK ** -0.5)
    vv = v_ref[...]
    gamma = jnp.exp(lg_ref[0][0:1, 0:1])
    nk = RET_DK // LANES
    k_t = jnp.concatenate([_tr(kk[:, LANES * i:LANES * (i + 1)]) for i in range(nk)], axis=0)
    q_t = jnp.concatenate([_tr(qq[:, LANES * i:LANES * (i + 1)]) for i in range(nk)], axis=0)
    for bi in range(nb):
        s_new = st_ref[0, bi, 0] * gamma + k_t[:, bi:bi + 1] * vv[bi:bi + 1, :]
        so_ref[0, bi, 0] = s_new
        ybuf[bi:bi + 1, :] = jnp.sum(q_t[:, bi:bi + 1] * s_new, axis=0, keepdims=True)
    y_ref[...] = (_rms(ybuf[...]) * _silu(g_ref[...])).astype(BF16)


def _ret_step(proj, rope, lg, nheads, state, layer):
    m = proj.shape[0]
    nb = _pick(m, (STEP_BATCH,))
    v0 = 2 * nheads * RET_DK // RET_DV
    g0 = v0 + nheads
    cos, s1, s2 = rope
    st_spec = pl.BlockSpec((1, nb, 1, RET_DK, RET_DV), lambda i, h: (layer, i, h, 0, 0))
    in_specs = [
        pl.BlockSpec((nb, RET_DK), lambda i, h: (i, h)),
        pl.BlockSpec((nb, RET_DK), lambda i, h: (i, nheads + h)),
        pl.BlockSpec((nb, RET_DV), lambda i, h: (i, v0 + h)),
        pl.BlockSpec((nb, RET_DV), lambda i, h: (i, g0 + h)),
        pl.BlockSpec((1, RET_DK), lambda i, h: (0, 0)),
        pl.BlockSpec((1, RET_DK), lambda i, h: (0, 0)),
        pl.BlockSpec((1, RET_DK), lambda i, h: (0, 0)),
        pl.BlockSpec((1, SUBLANES, LANES), lambda i, h: (h, 0, 0)),
        st_spec,
    ]
    y, st = pl.pallas_call(
        _ret_step_kernel,
        grid=(m // nb, nheads),
        in_specs=in_specs,
        out_specs=[pl.BlockSpec((nb, RET_DV), lambda i, h: (i, h)), st_spec],
        out_shape=[jax.ShapeDtypeStruct((m, nheads * RET_DV), BF16),
                   jax.ShapeDtypeStruct(state.shape, F32)],
        scratch_shapes=[pltpu.VMEM((nb, RET_DV), F32)],
        compiler_params=_cparams(("parallel", "parallel")),
        name="ret_step",
    )(proj, proj, proj, proj, cos, s1, s2, lg, state)
    return y, st


def _pad_lanes(x):
    return jnp.pad(x, [(0, 0)] * (x.ndim - 1) + [(0, LANES - x.shape[-1])])


def _prep_ssm(w_in, conv_w, conv_b, dt_bias, a_log, d_skip, norm_w):
    d_model = w_in.shape[0]
    heads = dt_bias.shape[0]
    g = SSM_GROUPS
    hpg = heads // g
    d_inner = heads * SSM_HEAD_DIM
    conv_dim = conv_w.shape[1]
    main = d_inner + conv_dim
    per_group = lambda v: _pad_lanes(v.reshape(g, hpg)).reshape(1, g * LANES)
    w_dt = _pad_lanes(w_in[:, main:].reshape(d_model, g, hpg)).reshape(d_model, g * LANES)
    w_all = jnp.concatenate([w_in[:, :main].astype(BF16), w_dt.astype(BF16)], axis=1)
    return dict(d_inner=d_inner, hpg=hpg, conv_dim=conv_dim, w_in=w_all[None],
                conv_w=conv_w, conv_b=conv_b.reshape(1, conv_dim),
                dt_bias=per_group(dt_bias), a_log=per_group(a_log), d=per_group(d_skip),
                norm=norm_w.reshape(1, d_inner))


def _prep_dn(w_in, conv_w, a_log, dt_bias, norm_w):
    d_model = w_in.shape[0]
    hv = a_log.shape[0]
    hk = hv // 2
    conv_dim = conv_w.shape[1]
    main = conv_dim + hv * DN_HEAD
    wb = w_in[:, main:main + hv].reshape(d_model, hk, 2)
    wa = w_in[:, main + hv:main + 2 * hv].reshape(d_model, hk, 2)
    w_ba = _pad_lanes(jnp.concatenate([wb, wa], axis=-1)).reshape(d_model, hk * LANES)
    w_all = jnp.concatenate([w_in[:, :main].astype(BF16), w_ba.astype(BF16)], axis=1)
    per_head = lambda v: _pad_lanes(jnp.concatenate([jnp.zeros((hk, 2), F32), v.reshape(hk, 2)], axis=-1)
                                    ).reshape(1, hk * LANES)
    return dict(hk=hk, conv_dim=conv_dim, w_in=w_all[None], conv_w=conv_w,
                a_log=per_head(a_log), dt_bias=per_head(dt_bias), norm=norm_w.reshape(1, DN_HEAD))


def _rope_tables(pos):
    half = RET_DK // 2
    inv = 1.0 / (RET_ROPE_BASE ** jnp.linspace(0.0, 1.0, half, dtype=F32))
    ang = pos.astype(F32)[:, None] * inv[None, :]
    cos, sin, zero = jnp.cos(ang), jnp.sin(ang), jnp.zeros_like(ang)
    inter = lambda a, b: jnp.stack([a, b], axis=-1).reshape(pos.shape[0], RET_DK)
    return inter(cos, cos), inter(-sin, zero), inter(zero, sin)


def _forward(x, p, prompt, states, prm, wts):
    batch, seq, d_model = x.shape
    m = batch * seq
    h = x.reshape(m, d_model)
    p = p.reshape(p.shape[0], m, p.shape[-1])
    depth = prm["norm_mix_pre"].shape[0]
    ssm_s, ssm_c, dn_s, dn_c, ret_s = states
    o_ssm = o_ssm_c = o_dn = o_dn_c = o_ret = None
    p_ssm, p_ssm_c, p_dn, p_dn_c, p_ret = [], [], [], [], []
    tail = slice(seq - (CONV_W - 1), seq)
    for i in range(depth):
        kind, j = i % 3, i // 3
        gain = prm["norm_mix_pre"][i]
        if kind == 0:
            w = wts["ssm"][j]
            d_inner, conv_dim = w["d_inner"], w["conv_dim"]
            proj = _norm_mm(h, gain, w["w_in"], 0)
            if prompt:
                y, st = _ssd_prompt(proj, w, batch, seq)
                p_ssm.append(st)
                p_ssm_c.append(proj.reshape(batch, seq, -1)[:, tail, d_inner:d_inner + conv_dim])
            else:
                xbc, o_ssm_c = _conv_step(proj, d_inner, ssm_c, j, w["conv_w"], w["conv_b"], o_ssm_c)
                y, o_ssm = _ssd_step(proj, xbc, w, ssm_s, j, o_ssm)
            w_out = wts["ssm_out"]
        elif kind == 1:
            w = wts["dn"][j]
            conv_dim = w["conv_dim"]
            proj = _norm_mm(h, gain, w["w_in"], 0)
            if prompt:
                y, st = _dn_prompt(proj, w, batch, seq)
                p_dn.append(st)
                p_dn_c.append(proj.reshape(batch, seq, -1)[:, tail, :conv_dim])
            else:
                qkv, o_dn_c = _conv_step(proj, 0, dn_c, j, w["conv_w"], None, o_dn_c)
                y, o_dn = _dn_step(proj, qkv, w, dn_s, j)
            w_out = wts["dn_out"]
        else:
            nheads = wts["ret_heads"]
            proj = _norm_mm(h, gain, wts["ret_in"], j)
            if prompt:
                y, st = _ret_prompt(proj, wts["rope"], wts["lg"], nheads, batch, seq)
                p_ret.append(st)
            else:
                y, o_ret = _ret_step(proj, wts["rope"], wts["lg"], nheads, ret_s, j)
            w_out = wts["ret_out"]
        h = _mm_out(y, w_out, j, h, prm["norm_mix_post"][i])
        act = _ffn_in(h, prm["norm_ffn_pre"][i], wts["ffn_gate"], wts["ffn_up"], i)
        h = _mm_out(act, wts["ffn_down"], i, h, prm["norm_ffn_post"][i])
        h = _ple(h, prm["norm_ple"][i], wts["ple_gate"], p, wts["ple_proj"], i)
    h = h.reshape(batch, seq, d_model)
    if prompt:
        return (h, jnp.swapaxes(jnp.stack(p_ssm), -1, -2), jnp.stack(p_ssm_c), jnp.stack(p_dn),
                jnp.stack(p_dn_c), jnp.stack(p_ret))
    return (h, jnp.swapaxes(o_ssm, -1, -2), jnp.swapaxes(o_ssm_c, 1, 2), o_dn, jnp.swapaxes(o_dn_c, 1, 2), o_ret)


def kernel(x_prompt, x_sample, state_ssm, state_ssm_conv, state_delta, state_delta_conv, state_ret, p_prompt, p_sample, norm_mix_pre, norm_mix_post, norm_ffn_pre, norm_ffn_post, norm_ple, ffn_w_gate, ffn_w_up, ffn_w_down, ple_w_proj, ple_w_gate, ssm_w_in, ssm_conv_w, ssm_conv_b, ssm_dt_bias, ssm_a_log, ssm_d, ssm_norm, ssm_w_out, dn_w_in, dn_conv_w, dn_a_log, dn_dt_bias, dn_norm, dn_w_out, ret_w_in, ret_w_out):
    assert x_sample.shape[1] == 1, "the sample group advances one token per sequence"
    prm = dict(norm_mix_pre=norm_mix_pre, norm_mix_post=norm_mix_post, norm_ffn_pre=norm_ffn_pre,
               norm_ffn_post=norm_ffn_post, norm_ple=norm_ple)
    ret_heads = ret_w_out.shape[1] // RET_DV
    log_gamma = jnp.log(1.0 - 2.0 ** (-5.0 - jnp.arange(ret_heads, dtype=F32)))
    base = dict(
        ssm=[_prep_ssm(ssm_w_in[j], ssm_conv_w[j], ssm_conv_b[j], ssm_dt_bias[j], ssm_a_log[j], ssm_d[j],
                       ssm_norm[j]) for j in range(ssm_w_in.shape[0])],
        dn=[_prep_dn(dn_w_in[j], dn_conv_w[j], dn_a_log[j], dn_dt_bias[j], dn_norm[j])
            for j in range(dn_w_in.shape[0])],
        ssm_out=ssm_w_out.astype(BF16), dn_out=dn_w_out.astype(BF16),
        ret_in=ret_w_in.astype(BF16), ret_out=ret_w_out.astype(BF16), ret_heads=ret_heads,
        ffn_gate=ffn_w_gate.astype(BF16), ffn_up=ffn_w_up.astype(BF16), ffn_down=ffn_w_down.astype(BF16),
        ple_gate=ple_w_gate.astype(BF16), ple_proj=ple_w_proj.astype(BF16),
        lg=jnp.broadcast_to(log_gamma[:, None, None], (ret_heads, SUBLANES, LANES)),
    )
    seq = x_prompt.shape[1]
    pos_prompt = jnp.arange(seq, dtype=jnp.int32)
    pos_sample = PAST_LEN + jnp.arange(1, dtype=jnp.int32)
    out_p = _forward(x_prompt, p_prompt, True, (None,) * 5, prm, dict(base, rope=_rope_tables(pos_prompt)))
    states = (jnp.swapaxes(state_ssm, -1, -2), jnp.swapaxes(state_ssm_conv, 1, 2), state_delta,
              jnp.swapaxes(state_delta_conv, 1, 2), state_ret)
    out_s = _forward(x_sample, p_sample, False, states, prm, dict(base, rope=_rope_tables(pos_sample)))
    return (out_p[0], out_s[0]) + out_p[1:] + out_s[1:]
```

```python
import functools

import jax
import jax.numpy as jnp
from jax import lax
from jax.experimental import pallas as pl
from jax.experimental.pallas import tpu as pltpu

F32 = jnp.float32
BF16 = jnp.bfloat16

NORM_EPS = 1e-6
PAST_LEN = 16384
CONV_W = 4
SSM_GROUPS = 8
SSM_HEAD_DIM = 64
SSM_D_STATE = 128
DN_HEAD = 128
RET_DK = 256
RET_DV = 512
RET_ROPE_BASE = 10000.0

LANES = 128
SUBLANES = 8
NEG_BIG = -1e30
SSD_CHUNK = 128
RET_CHUNK = 128
DN_CHUNK = 64
DN_HEADS_PER_STEP = 8
RET_HEADS_PER_STEP = 4
STEP_BATCH = 8
VMEM_LIMIT = 52 * 1024 * 1024


def _pick(n, prefs):
    for p in prefs:
        if n % p == 0:
            return p
    return n


def _cparams(sem):
    return pltpu.CompilerParams(dimension_semantics=sem, vmem_limit_bytes=VMEM_LIMIT)


def _sigmoid(x):
    return 1.0 / (1.0 + jnp.exp(-x))


def _silu(x):
    return x * _sigmoid(x)


def _softplus(x):
    return jnp.maximum(x, 0.0) + jnp.log1p(jnp.exp(-jnp.abs(x)))


def _dot(a, b):
    return jnp.dot(a.astype(BF16), b.astype(BF16), preferred_element_type=F32)


def _dot_nt(a, b):
    return lax.dot_general(a.astype(BF16), b.astype(BF16), (((1,), (1,)), ((), ())),
                           preferred_element_type=F32)


def _dot01(m01, x):
    hi = x.astype(BF16)
    r = x - hi.astype(F32)
    mid = r.astype(BF16)
    lo = (r - mid.astype(F32)).astype(BF16)
    out = jnp.dot(m01, hi, preferred_element_type=F32)
    out = out + jnp.dot(m01, mid, preferred_element_type=F32)
    return out + jnp.dot(m01, lo, preferred_element_type=F32)


def _tr(x):
    r, c = x.shape
    assert c == LANES and r <= LANES
    if r < LANES:
        x = jnp.concatenate([x, jnp.zeros((LANES - r, c), x.dtype)], axis=0)
    return x.T[:, :r]


def _expand64(x, nheads):
    rows = x.shape[0]
    lane = lax.broadcasted_iota(jnp.int32, (rows, LANES), 1)
    parts = []
    for j in range(nheads // 2):
        a = jnp.broadcast_to(x[:, 2 * j:2 * j + 1], (rows, LANES))
        b = jnp.broadcast_to(x[:, 2 * j + 1:2 * j + 2], (rows, LANES))
        parts.append(jnp.where(lane < 64, a, b))
    return parts[0] if len(parts) == 1 else jnp.concatenate(parts, axis=1)


def _rms(y):
    return y * lax.rsqrt(jnp.mean(y * y, axis=-1, keepdims=True) + NORM_EPS)


def _conv_chunk(buf_ref, col0, x, w, bias):
    q, width = x.shape
    cols = slice(col0, col0 + width)
    buf_ref[SUBLANES:SUBLANES + q, cols] = x
    acc = x * w[CONV_W - 1:CONV_W, :]
    for s in range(1, CONV_W):
        acc = acc + buf_ref[SUBLANES - s:SUBLANES - s + q, cols] * w[CONV_W - 1 - s:CONV_W - s, :]
    buf_ref[0:SUBLANES, cols] = x[q - SUBLANES:q, :]
    if bias is not None:
        acc = acc + bias
    return acc


def _bdot(a, b):
    return lax.dot_general(a.astype(BF16), b.astype(BF16), (((2,), (1,)), ((0,), (0,))),
                           preferred_element_type=F32)


def _bdot_nt(a, b):
    return lax.dot_general(a.astype(BF16), b.astype(BF16), (((2,), (2,)), ((0,), (0,))),
                           preferred_element_type=F32)


def _tri_inv(a, row, col):
    q = a.shape[-1]
    eye = (row == col).astype(F32)[None]
    d = eye - jnp.where((jnp.right_shift(row, 1) == jnp.right_shift(col, 1))[None], a, 0.0)
    sh = 1
    while (1 << sh) < q:
        same_big = jnp.right_shift(row, sh + 1) == jnp.right_shift(col, sh + 1)
        diff_small = jnp.right_shift(row, sh) != jnp.right_shift(col, sh)
        lb = jnp.where((same_big & diff_small)[None], a, 0.0)
        d = d - _bdot(d, _bdot(lb, d))
        sh += 1
    return d


def _norm_mm_kernel(x_ref, g_ref, w_ref, o_ref, xn_ref):
    @pl.when(pl.program_id(1) == 0)
    def _():
        xn_ref[...] = (_rms(x_ref[...]) * g_ref[...]).astype(BF16)

    o_ref[...] = jnp.dot(xn_ref[...], w_ref[0], preferred_element_type=F32)


def _norm_mm(x, gain, w, layer):
    m, k = x.shape
    n = w.shape[2]
    tm = _pick(m, (1024, 512, 256, 128))
    tn = _pick(n, (1024, 512, 256, 128))
    return pl.pallas_call(
        _norm_mm_kernel,
        grid=(m // tm, n // tn),
        in_specs=[pl.BlockSpec((tm, k), lambda i, j: (i, 0)),
                  pl.BlockSpec((1, k), lambda i, j: (0, 0)),
                  pl.BlockSpec((1, k, tn), lambda i, j: (layer, 0, j))],
        out_specs=pl.BlockSpec((tm, tn), lambda i, j: (i, j)),
        out_shape=jax.ShapeDtypeStruct((m, n), F32),
        scratch_shapes=[pltpu.VMEM((tm, k), BF16)],
        compiler_params=_cparams(("parallel", "arbitrary")),
        name="norm_mm",
    )(x, gain.reshape(1, k), w)


def _mm_out_kernel(a_ref, w_ref, h_ref, g_ref, o_ref):
    y = jnp.dot(a_ref[...], w_ref[0], preferred_element_type=F32)
    o_ref[...] = h_ref[...] + _rms(y) * g_ref[...]


def _mm_out(a, w, layer, h, gain):
    m, k = a.shape
    n = w.shape[2]
    tm = _pick(m, (256, 128))
    return pl.pallas_call(
        _mm_out_kernel,
        grid=(m // tm,),
        in_specs=[pl.BlockSpec((tm, k), lambda i: (i, 0)),
                  pl.BlockSpec((1, k, n), lambda i: (layer, 0, 0), pipeline_mode=pl.Buffered(1)),
                  pl.BlockSpec((tm, n), lambda i: (i, 0)),
                  pl.BlockSpec((1, n), lambda i: (0, 0))],
        out_specs=pl.BlockSpec((tm, n), lambda i: (i, 0)),
        out_shape=jax.ShapeDtypeStruct((m, n), F32),
        compiler_params=_cparams(("parallel",)),
        name="mm_out",
    )(a, w, h, gain.reshape(1, n))


def _ffn_in_kernel(x_ref, g_ref, wg_ref, wu_ref, o_ref, xn_ref):
    @pl.when(pl.program_id(1) == 0)
    def _():
        xn_ref[...] = (_rms(x_ref[...]) * g_ref[...]).astype(BF16)

    xn = xn_ref[...]
    gate = jnp.dot(xn, wg_ref[0], preferred_element_type=F32)
    up = jnp.dot(xn, wu_ref[0], preferred_element_type=F32)
    o_ref[...] = (_silu(gate) * up).astype(BF16)


def _ffn_in(x, gain, wg, wu, layer):
    m, k = x.shape
    n = wg.shape[2]
    tm = _pick(m, (1024, 512, 256, 128))
    tn = _pick(n, (512, 256, 128))
    return pl.pallas_call(
        _ffn_in_kernel,
        grid=(m // tm, n // tn),
        in_specs=[pl.BlockSpec((tm, k), lambda i, j: (i, 0)),
                  pl.BlockSpec((1, k), lambda i, j: (0, 0)),
                  pl.BlockSpec((1, k, tn), lambda i, j: (layer, 0, j)),
                  pl.BlockSpec((1, k, tn), lambda i, j: (layer, 0, j))],
        out_specs=pl.BlockSpec((tm, tn), lambda i, j: (i, j)),
        out_shape=jax.ShapeDtypeStruct((m, n), BF16),
        scratch_shapes=[pltpu.VMEM((tm, k), BF16)],
        compiler_params=_cparams(("parallel", "arbitrary")),
        name="ffn_in",
    )(x, gain.reshape(1, k), wg, wu)


def _ple_kernel(h_ref, hc_ref, g_ref, wg_ref, p_ref, wp_ref, o_ref, xn_ref):
    @pl.when(pl.program_id(1) == 0)
    def _():
        xn_ref[...] = (_rms(h_ref[...]) * g_ref[...]).astype(BF16)

    gate = _sigmoid(jnp.dot(xn_ref[...], wg_ref[0], preferred_element_type=F32))
    proj = jnp.dot(p_ref[0].astype(BF16), wp_ref[0], preferred_element_type=F32)
    o_ref[...] = hc_ref[...] + proj * gate


def _ple(h, gain, wg, p, wp, layer):
    m, k = h.shape
    n = wg.shape[2]
    pd = p.shape[2]
    tm = _pick(m, (1024, 512, 256, 128))
    tn = _pick(n, (512, 256, 128))
    return pl.pallas_call(
        _ple_kernel,
        grid=(m // tm, n // tn),
        in_specs=[pl.BlockSpec((tm, k), lambda i, j: (i, 0)),
                  pl.BlockSpec((tm, tn), lambda i, j: (i, j)),
                  pl.BlockSpec((1, k), lambda i, j: (0, 0)),
                  pl.BlockSpec((1, k, tn), lambda i, j: (layer, 0, j)),
                  pl.BlockSpec((1, tm, pd), lambda i, j: (layer, i, 0)),
                  pl.BlockSpec((1, pd, tn), lambda i, j: (layer, 0, j))],
        out_specs=pl.BlockSpec((tm, tn), lambda i, j: (i, j)),
        out_shape=jax.ShapeDtypeStruct((m, n), F32),
        scratch_shapes=[pltpu.VMEM((tm, k), BF16)],
        compiler_params=_cparams(("parallel", "arbitrary")),
        name="ple",
    )(h, h, gain.reshape(1, k), wg, p, wp)


def _ssd_prompt_kernel(z_ref, xs_ref, b_ref, c_ref, dt_ref, wx_ref, wb_ref, wc_ref, bx_ref, bb_ref, bc_ref,
                       dtb_ref, alog_ref, d_ref, nw_ref, y_ref, st_ref, s_scr, cbuf, *, nheads):
    c = pl.program_id(2)
    q, width = xs_ref.shape
    n = b_ref.shape[1]
    hd = SSM_HEAD_DIM

    @pl.when(c == 0)
    def _():
        s_scr[...] = jnp.zeros_like(s_scr)
        cbuf[0:SUBLANES, :] = jnp.zeros((SUBLANES, cbuf.shape[1]), F32)

    xs = _silu(_conv_chunk(cbuf, 0, xs_ref[...], wx_ref[...], bx_ref[...]))
    bm = _silu(_conv_chunk(cbuf, width, b_ref[...], wb_ref[...], bb_ref[...]))
    cm = _silu(_conv_chunk(cbuf, width + n, c_ref[...], wc_ref[...], bc_ref[...]))

    dtv = _softplus(dt_ref[...] + dtb_ref[...])
    la = dtv * (-jnp.exp(alog_ref[...]))
    row = lax.broadcasted_iota(jnp.int32, (q, q), 0)
    col = lax.broadcasted_iota(jnp.int32, (q, q), 1)
    causal = row >= col
    acs = _dot01(causal.astype(BF16), la)
    acs_t = _tr(acs)
    scores = _dot_nt(cm, bm)
    dt_t = _tr(dtv)
    s_old = s_scr[...]
    y = _dot_nt(cm, s_old) * _expand64(jnp.exp(acs), nheads)
    lane = lax.broadcasted_iota(jnp.int32, (q, LANES), 1)
    parts = []
    for j in range(nheads // 2):
        xp = xs[:, LANES * j:LANES * (j + 1)].astype(BF16)
        ys = []
        for t in range(2):
            r = 2 * j + t
            seg = acs[:, r:r + 1] - acs_t[r:r + 1, :]
            decay = jnp.exp(jnp.where(causal, seg, NEG_BIG))
            ys.append(jnp.dot((scores * decay * dt_t[r:r + 1, :]).astype(BF16), xp, preferred_element_type=F32))
        parts.append(jnp.where(lane < 64, ys[0], ys[1]))
    y = y + (parts[0] if len(parts) == 1 else jnp.concatenate(parts, axis=1))
    y = y + xs * _expand64(d_ref[...], nheads)

    last = acs[q - 1:q, :]
    wsc_t = _tr(jnp.exp(last - acs) * dtv)
    sub = lax.broadcasted_iota(jnp.int32, (LANES, q), 0)
    wv_t = []
    for j in range(width // LANES):
        scale = jnp.where(sub < hd, wsc_t[2 * j:2 * j + 1, :], wsc_t[2 * j + 1:2 * j + 2, :])
        wv_t.append(_tr(xs[:, LANES * j:LANES * (j + 1)]) * scale)
    wv_t = wv_t[0] if len(wv_t) == 1 else jnp.concatenate(wv_t, axis=0)
    e_last = jnp.exp(last)
    e_rows = jnp.concatenate([jnp.broadcast_to(e_last[:, r:r + 1], (hd, n)) for r in range(nheads)], axis=0)
    s_new = s_old * e_rows + _dot(wv_t, bm)
    s_scr[...] = s_new

    y = y * _silu(z_ref[...])
    y_ref[...] = (_rms(y) * nw_ref[...]).astype(BF16)

    @pl.when(c == pl.num_programs(2) - 1)
    def _():
        for r in range(nheads):
            st_ref[0, r] = s_new[hd * r:hd * (r + 1), :]


def _ssd_prompt(proj, w, batch, seq):
    d_inner, nh, g, n = w["d_inner"], w["hpg"], SSM_GROUPS, SSM_D_STATE
    width = nh * SSM_HEAD_DIM
    q = _pick(seq, (SSD_CHUNK, 64, 32, 16, 8))
    nc = seq // q
    xs0, b0, c0 = d_inner // width, 2 * d_inner // n, 2 * d_inner // n + g
    dt0 = (d_inner + w["conv_dim"]) // LANES
    cw0 = d_inner // n
    tok = lambda b, gi, c: b * nc + c
    in_specs = [
        pl.BlockSpec((q, width), lambda b, gi, c: (tok(b, gi, c), gi)),
        pl.BlockSpec((q, width), lambda b, gi, c: (tok(b, gi, c), xs0 + gi)),
        pl.BlockSpec((q, n), lambda b, gi, c: (tok(b, gi, c), b0 + gi)),
        pl.BlockSpec((q, n), lambda b, gi, c: (tok(b, gi, c), c0 + gi)),
        pl.BlockSpec((q, LANES), lambda b, gi, c: (tok(b, gi, c), dt0 + gi)),
        pl.BlockSpec((CONV_W, width), lambda b, gi, c: (0, gi)),
        pl.BlockSpec((CONV_W, n), lambda b, gi, c: (0, cw0 + gi)),
        pl.BlockSpec((CONV_W, n), lambda b, gi, c: (0, cw0 + g + gi)),
        pl.BlockSpec((1, width), lambda b, gi, c: (0, gi)),
        pl.BlockSpec((1, n), lambda b, gi, c: (0, cw0 + gi)),
        pl.BlockSpec((1, n), lambda b, gi, c: (0, cw0 + g + gi)),
        pl.BlockSpec((1, LANES), lambda b, gi, c: (0, gi)),
        pl.BlockSpec((1, LANES), lambda b, gi, c: (0, gi)),
        pl.BlockSpec((1, LANES), lambda b, gi, c: (0, gi)),
        pl.BlockSpec((1, width), lambda b, gi, c: (0, gi)),
    ]
    y, st = pl.pallas_call(
        functools.partial(_ssd_prompt_kernel, nheads=nh),
        grid=(batch, g, nc),
        in_specs=in_specs,
        out_specs=[pl.BlockSpec((q, width), lambda b, gi, c: (tok(b, gi, c), gi)),
                   pl.BlockSpec((1, nh, SSM_HEAD_DIM, n), lambda b, gi, c: (b, gi, 0, 0))],
        out_shape=[jax.ShapeDtypeStruct((batch * seq, d_inner), BF16),
                   jax.ShapeDtypeStruct((batch, g * nh, SSM_HEAD_DIM, n), F32)],
        scratch_shapes=[pltpu.VMEM((width, n), F32),
                        pltpu.VMEM((q + SUBLANES, width + 2 * n), F32)],
        compiler_params=_cparams(("parallel", "parallel", "arbitrary")),
        name="ssd_prompt",
    )(proj, proj, proj, proj, proj, w["conv_w"], w["conv_w"], w["conv_w"], w["conv_b"], w["conv_b"], w["conv_b"],
      w["dt_bias"], w["a_log"], w["d"], w["norm"])
    return y, st


def _dn_prompt_kernel(q_ref, k_ref, v_ref, z_ref, ba_ref, wq_ref, wk_ref, wv_ref, alog_ref, dtb_ref, nw_ref,
                      o_ref, st_ref, s_scr, cbuf, *, hb):
    c = pl.program_id(2)
    q = q_ref.shape[0]
    hd = DN_HEAD

    @pl.when(c == 0)
    def _():
        s_scr[...] = jnp.zeros_like(s_scr)
        cbuf[0:SUBLANES, :] = jnp.zeros((SUBLANES, cbuf.shape[1]), F32)

    qa = _silu(_conv_chunk(cbuf, 0, q_ref[...], wq_ref[...], None))
    ka = _silu(_conv_chunk(cbuf, hb * hd, k_ref[...], wk_ref[...], None))
    va = _silu(_conv_chunk(cbuf, 2 * hb * hd, v_ref[...], wv_ref[...], None))

    ba = ba_ref[...]
    beta_all = _sigmoid(ba)
    gg = -jnp.exp(alog_ref[...]) * _softplus(ba + dtb_ref[...])
    row = lax.broadcasted_iota(jnp.int32, (q, q), 0)
    col = lax.broadcasted_iota(jnp.int32, (q, q), 1)
    incl = row >= col
    strict = row > col
    acs_all = _dot01(incl.astype(BF16), gg)

    heads = [(kh, j) for kh in range(hb) for j in range(2)]
    q3 = jnp.stack([qa[:, kh * hd:(kh + 1) * hd] for kh in range(hb)])
    k3 = jnp.stack([ka[:, kh * hd:(kh + 1) * hd] for kh in range(hb)])
    q3 = q3 * lax.rsqrt(jnp.sum(q3 * q3, axis=-1, keepdims=True) + NORM_EPS) * (hd ** -0.5)
    k3 = k3 * lax.rsqrt(jnp.sum(k3 * k3, axis=-1, keepdims=True) + NORM_EPS)
    acs_k = [acs_all[:, kh * LANES:(kh + 1) * LANES] for kh in range(hb)]
    acs_t = [_tr(a) for a in acs_k]
    kk3 = _bdot_nt(k3, k3)
    qk3 = _bdot_nt(q3, k3)
    k_t3 = jnp.stack([_tr(k3[kh]) for kh in range(hb)])

    a_col = jnp.stack([acs_k[kh][:, 2 + j:3 + j] for kh, j in heads])
    a_row = jnp.stack([acs_t[kh][2 + j:3 + j, :] for kh, j in heads])
    last = jnp.stack([acs_k[kh][q - 1:q, 2 + j:3 + j] for kh, j in heads])
    bcol = jnp.stack([beta_all[:, kh * LANES + j:kh * LANES + j + 1] for kh, j in heads])
    rep = lambda x: jnp.stack([x[kh] for kh, _ in heads])
    k_v, q_v = rep(k3), rep(q3)
    gam = jnp.exp(jnp.where(incl[None], a_col - a_row, NEG_BIG))
    a_mat = jnp.where(strict[None], rep(kk3) * gam * bcol, 0.0)
    t_mat = _tri_inv(a_mat, row, col)
    v3 = jnp.stack([va[:, i * hd:(i + 1) * hd] for i in range(len(heads))])
    e_col = jnp.exp(a_col)
    rhs = jnp.concatenate([bcol * v3, (bcol * e_col) * k_v], axis=2)
    sol = _bdot(t_mat, rhs)
    s_old = s_scr[...]
    u = sol[:, :, :hd] - _bdot(sol[:, :, hd:], s_old)
    o = _bdot(rep(qk3) * gam, u) + _bdot(q_v, s_old) * e_col
    s_scr[...] = s_old * jnp.exp(last) + _bdot(rep(k_t3), jnp.exp(last - a_col) * u)
    on = _rms(o) * nw_ref[...]
    for i in range(len(heads)):
        hs = slice(i * hd, (i + 1) * hd)
        o_ref[:, hs] = (on[i] * _silu(z_ref[:, hs])).astype(BF16)

    @pl.when(c == pl.num_programs(2) - 1)
    def _():
        st_ref[0] = s_scr[...]


def _dn_prompt(proj, w, batch, seq):
    hk, hd = w["hk"], DN_HEAD
    hb = _pick(hk, (DN_HEADS_PER_STEP, 2, 1))
    q = _pick(seq, (DN_CHUNK, 32, 16, 8))
    nc = seq // q
    nhb = hk // hb
    tok = lambda b, h, c: b * nc + c
    ba0 = (6 * hk * hd) // (hb * LANES)
    in_specs = [
        pl.BlockSpec((q, hb * hd), lambda b, h, c: (tok(b, h, c), h)),
        pl.BlockSpec((q, hb * hd), lambda b, h, c: (tok(b, h, c), nhb + h)),
        pl.BlockSpec((q, 2 * hb * hd), lambda b, h, c: (tok(b, h, c), nhb + h)),
        pl.BlockSpec((q, 2 * hb * hd), lambda b, h, c: (tok(b, h, c), 2 * nhb + h)),
        pl.BlockSpec((q, hb * LANES), lambda b, h, c: (tok(b, h, c), ba0 + h)),
        pl.BlockSpec((CONV_W, hb * hd), lambda b, h, c: (0, h)),
        pl.BlockSpec((CONV_W, hb * hd), lambda b, h, c: (0, nhb + h)),
        pl.BlockSpec((CONV_W, 2 * hb * hd), lambda b, h, c: (0, nhb + h)),
        pl.BlockSpec((1, hb * LANES), lambda b, h, c: (0, h)),
        pl.BlockSpec((1, hb * LANES), lambda b, h, c: (0, h)),
        pl.BlockSpec((1, hd), lambda b, h, c: (0, 0)),
    ]
    o, st = pl.pallas_call(
        functools.partial(_dn_prompt_kernel, hb=hb),
        grid=(batch, nhb, nc),
        in_specs=in_specs,
        out_specs=[pl.BlockSpec((q, 2 * hb * hd), lambda b, h, c: (tok(b, h, c), h)),
                   pl.BlockSpec((1, 2 * hb, hd, hd), lambda b, h, c: (b, h, 0, 0))],
        out_shape=[jax.ShapeDtypeStruct((batch * seq, 2 * hk * hd), BF16),
                   jax.ShapeDtypeStruct((batch, 2 * hk, hd, hd), F32)],
        scratch_shapes=[pltpu.VMEM((2 * hb, hd, hd), F32),
                        pltpu.VMEM((q + SUBLANES, 4 * hb * hd), F32)],
        compiler_params=_cparams(("parallel", "parallel", "arbitrary")),
        name="dn_prompt",
    )(proj, proj, proj, proj, proj, w["conv_w"], w["conv_w"], w["conv_w"], w["a_log"], w["dt_bias"], w["norm"])
    return o, st


def _rotate(x, cos, s1, s2):
    w = x.shape[1]
    return x * cos + pltpu.roll(x, w - 1, 1) * s1 + pltpu.roll(x, 1, 1) * s2


def _ret_prompt_kernel(q_ref, k_ref, v_ref, g_ref, cos_ref, s1_ref, s2_ref, lg_ref, y_ref, st_ref, s_scr, *, hb):
    c = pl.program_id(2)
    q = q_ref.shape[0]

    @pl.when(c == 0)
    def _():
        s_scr[...] = jnp.zeros_like(s_scr)

    tile = lambda t: t if hb == 1 else jnp.concatenate([t] * hb, axis=1)
    cos, s1, s2 = tile(cos_ref[...]), tile(s1_ref[...]), tile(s2_ref[...])
    qa = _rotate(q_ref[...], cos, s1, s2)
    ka = _rotate(k_ref[...], cos, s1, s2) * (RET_DK ** -0.5)
    q3 = jnp.stack([qa[:, RET_DK * h:RET_DK * (h + 1)] for h in range(hb)])
    k3 = jnp.stack([ka[:, RET_DK * h:RET_DK * (h + 1)] for h in range(hb)])
    v3 = jnp.stack([v_ref[:, RET_DV * h:RET_DV * (h + 1)] for h in range(hb)])
    lg = lg_ref[...][:, 0:1, 0:1]
    row = lax.broadcasted_iota(jnp.int32, (q, q), 0)
    col = lax.broadcasted_iota(jnp.int32, (q, q), 1)
    dist = (row - col).astype(F32)[None]
    decay = jnp.exp(jnp.where((row >= col)[None], dist * lg, NEG_BIG))
    pos = lax.broadcasted_iota(jnp.int32, (1, q, 1), 1).astype(F32)
    scores = _bdot_nt(q3, k3)
    s_old = s_scr[...]
    y = _bdot(scores * decay, v3) + _bdot(q3, s_old) * jnp.exp((pos + 1.0) * lg)
    wv = jnp.exp((float(q - 1) - pos) * lg) * v3
    k_t = jnp.stack([jnp.concatenate([_tr(k3[h][:, LANES * i:LANES * (i + 1)]) for i in range(RET_DK // LANES)],
                                     axis=0) for h in range(hb)])
    s_new = s_old * jnp.exp(float(q) * lg) + _bdot(k_t, wv)
    s_scr[...] = s_new
    yn = _rms(y)
    for h in range(hb):
        hs = slice(RET_DV * h, RET_DV * (h + 1))
        y_ref[:, hs] = (yn[h] * _silu(g_ref[:, hs])).astype(BF16)

    @pl.when(c == pl.num_programs(2) - 1)
    def _():
        st_ref[0] = s_new


def _ret_prompt(proj, rope, lg, nheads, batch, seq):
    q = _pick(seq, (RET_CHUNK, 64, 32, 16, 8))
    hb = _pick(nheads, (RET_HEADS_PER_STEP, 2, 1))
    nc = seq // q
    nhb = nheads // hb
    tok = lambda b, h, c: b * nc + c
    v0 = 2 * nheads * RET_DK // (hb * RET_DV)
    g0 = v0 + nhb
    cos, s1, s2 = rope
    in_specs = [
        pl.BlockSpec((q, hb * RET_DK), lambda b, h, c: (tok(b, h, c), h)),
        pl.BlockSpec((q, hb * RET_DK), lambda b, h, c: (tok(b, h, c), nhb + h)),
        pl.BlockSpec((q, hb * RET_DV), lambda b, h, c: (tok(b, h, c), v0 + h)),
        pl.BlockSpec((q, hb * RET_DV), lambda b, h, c: (tok(b, h, c), g0 + h)),
        pl.BlockSpec((q, RET_DK), lambda b, h, c: (c, 0)),
        pl.BlockSpec((q, RET_DK), lambda b, h, c: (c, 0)),
        pl.BlockSpec((q, RET_DK), lambda b, h, c: (c, 0)),
        pl.BlockSpec((hb, SUBLANES, LANES), lambda b, h, c: (h, 0, 0)),
    ]
    y, st = pl.pallas_call(
        functools.partial(_ret_prompt_kernel, hb=hb),
        grid=(batch, nhb, nc),
        in_specs=in_specs,
        out_specs=[pl.BlockSpec((q, hb * RET_DV), lambda b, h, c: (tok(b, h, c), h)),
                   pl.BlockSpec((1, hb, RET_DK, RET_DV), lambda b, h, c: (b, h, 0, 0))],
        out_shape=[jax.ShapeDtypeStruct((batch * seq, nheads * RET_DV), BF16),
                   jax.ShapeDtypeStruct((batch, nheads, RET_DK, RET_DV), F32)],
        scratch_shapes=[pltpu.VMEM((hb, RET_DK, RET_DV), F32)],
        compiler_params=_cparams(("parallel", "parallel", "arbitrary")),
        name="ret_prompt",
    )(proj, proj, proj, proj, cos, s1, s2, lg)
    return y, st


def _alias_args(prev, n_in, out_idx):
    if prev is None:
        return [], [], {}
    return [pl.BlockSpec(memory_space=pl.ANY)], [prev], {n_in: out_idx}


def _conv_step_kernel(*refs, has_bias, aliased):
    refs = list(refs)
    x_ref, cs_ref, w_ref = refs[:3]
    b_ref = refs[3] if has_bias else None
    o_ref, cn_ref = refs[3 + int(has_bias) + int(aliased):]
    w = w_ref[...]
    x = x_ref[...]
    c0, c1, c2 = cs_ref[0, 0], cs_ref[0, 1], cs_ref[0, 2]
    acc = c0 * w[0:1, :] + c1 * w[1:2, :] + c2 * w[2:3, :] + x * w[3:4, :]
    if has_bias:
        acc = acc + b_ref[...]
    o_ref[...] = _silu(acc)
    cn_ref[0, 0] = c1
    cn_ref[0, 1] = c2
    cn_ref[0, 2] = x


def _conv_step(proj, col0, cstate, layer, conv_w, conv_b, prev):
    m = proj.shape[0]
    cdim = cstate.shape[3]
    cb = _pick(cdim, (512, 256, 128))
    x0 = col0 // cb
    taps = CONV_W - 1
    in_specs = [pl.BlockSpec((m, cb), lambda j: (0, x0 + j)),
                pl.BlockSpec((1, taps, m, cb), lambda j: (layer, 0, 0, j)),
                pl.BlockSpec((CONV_W, cb), lambda j: (0, j))]
    args = [proj, cstate, conv_w]
    if conv_b is not None:
        in_specs.append(pl.BlockSpec((1, cb), lambda j: (0, j)))
        args.append(conv_b)
    a_specs, a_args, aliases = _alias_args(prev, len(args), 1)
    return pl.pallas_call(
        functools.partial(_conv_step_kernel, has_bias=conv_b is not None, aliased=prev is not None),
        grid=(cdim // cb,),
        in_specs=in_specs + a_specs,
        out_specs=[pl.BlockSpec((m, cb), lambda j: (0, j)),
                   pl.BlockSpec((1, taps, m, cb), lambda j: (layer, 0, 0, j))],
        out_shape=[jax.ShapeDtypeStruct((m, cdim), F32),
                   jax.ShapeDtypeStruct(cstate.shape, F32)],
        input_output_aliases=aliases,
        compiler_params=_cparams(("parallel",)),
        name="conv_step",
    )(*args, *a_args)


def _ssd_step_kernel(*refs, nheads):
    z_ref, xs_ref, b_ref, c_ref, dt_ref, dtb_ref, alog_ref, d_ref, nw_ref, st_ref = refs[:10]
    y_ref, so_ref, ytb = refs[-3:]
    nb = xs_ref.shape[0]
    hd = SSM_HEAD_DIM
    xs = xs_ref[...]
    dtv = _softplus(dt_ref[...] + dtb_ref[...])
    decay = jnp.exp(dtv * (-jnp.exp(alog_ref[...])))
    pad = jnp.zeros((LANES - nb, LANES), F32)
    kmat = jnp.concatenate([b_ref[...], pad], axis=0).astype(BF16)
    qmat = jnp.concatenate([c_ref[...], pad], axis=0).T.astype(BF16)
    v_all = xs * _expand64(dtv, nheads)
    v_t = [jnp.concatenate([v_all[:, LANES * j:LANES * (j + 1)], pad], axis=0).T
           for j in range(nheads // 2)]
    lane = lax.broadcasted_iota(jnp.int32, (hd, LANES), 1)
    for r in range(nheads):
        vt = v_t[r // 2][hd * (r % 2):hd * (r % 2 + 1), :]
        for bi in range(nb):
            outer = jnp.dot(jnp.where(lane == bi, vt, 0.0).astype(BF16), kmat, preferred_element_type=F32)
            so_ref[0, bi, r] = st_ref[0, bi, r] * decay[bi:bi + 1, r:r + 1] + outer
    for r in range(nheads):
        acc = jnp.zeros((hd, LANES), F32)
        for bi in range(nb):
            yb = jnp.dot(so_ref[0, bi, r].astype(BF16), qmat, preferred_element_type=F32)
            acc = jnp.where(lane == bi, yb, acc)
        ytb[hd * r:hd * (r + 1), :] = acc
    y = [ytb[LANES * j:LANES * (j + 1), :].T[:nb, :] for j in range(nheads // 2)]
    y = y[0] if len(y) == 1 else jnp.concatenate(y, axis=1)
    y = y + xs * _expand64(d_ref[...], nheads)
    y = y * _silu(z_ref[...])
    y_ref[...] = (_rms(y) * nw_ref[...]).astype(BF16)


def _ssd_step(proj, xbc, w, state, layer, prev):
    m = proj.shape[0]
    d_inner, nh, g, n = w["d_inner"], w["hpg"], SSM_GROUPS, SSM_D_STATE
    width = nh * SSM_HEAD_DIM
    nb = _pick(m, (STEP_BATCH,))
    b0 = d_inner // n
    dt0 = (d_inner + w["conv_dim"]) // LANES
    st_spec = pl.BlockSpec((1, nb, nh, SSM_HEAD_DIM, n), lambda i, gi: (layer, i, gi, 0, 0))
    in_specs = [
        pl.BlockSpec((nb, width), lambda i, gi: (i, gi)),
        pl.BlockSpec((nb, width), lambda i, gi: (i, gi)),
        pl.BlockSpec((nb, n), lambda i, gi: (i, b0 + gi)),
        pl.BlockSpec((nb, n), lambda i, gi: (i, b0 + g + gi)),
        pl.BlockSpec((nb, LANES), lambda i, gi: (i, dt0 + gi)),
        pl.BlockSpec((1, LANES), lambda i, gi: (0, gi)),
        pl.BlockSpec((1, LANES), lambda i, gi: (0, gi)),
        pl.BlockSpec((1, LANES), lambda i, gi: (0, gi)),
        pl.BlockSpec((1, width), lambda i, gi: (0, gi)),
        st_spec,
    ]
    args = [proj, xbc, xbc, xbc, proj, w["dt_bias"], w["a_log"], w["d"], w["norm"], state]
    a_specs, a_args, aliases = _alias_args(prev, len(args), 1)
    y, st = pl.pallas_call(
        functools.partial(_ssd_step_kernel, nheads=nh),
        grid=(m // nb, g),
        in_specs=in_specs + a_specs,
        out_specs=[pl.BlockSpec((nb, width), lambda i, gi: (i, gi)), st_spec],
        out_shape=[jax.ShapeDtypeStruct((m, d_inner), BF16),
                   jax.ShapeDtypeStruct(state.shape, F32)],
        scratch_shapes=[pltpu.VMEM((width, LANES), F32)],
        input_output_aliases=aliases,
        compiler_params=_cparams(("parallel", "parallel")),
        name="ssd_step",
    )(*args, *a_args)
    return y, st


def _dn_step_kernel(q_ref, k_ref, v_ref, z_ref, ba_ref, alog_ref, dtb_ref, nw_ref, st_ref,
                    o_ref, so_ref, obuf):
    nb = q_ref.shape[0]
    hd = DN_HEAD
    qq = q_ref[...]
    kk = k_ref[...]
    qq = qq * lax.rsqrt(jnp.sum(qq * qq, axis=-1, keepdims=True) + NORM_EPS) * (hd ** -0.5)
    kk = kk * lax.rsqrt(jnp.sum(kk * kk, axis=-1, keepdims=True) + NORM_EPS)
    vv = v_ref[...]
    ba = ba_ref[...]
    beta = _sigmoid(ba)
    eg = jnp.exp(-jnp.exp(alog_ref[...]) * _softplus(ba + dtb_ref[...]))
    k_t = _tr(kk)
    q_t = _tr(qq)
    for bi in range(nb):
        kcol = k_t[:, bi:bi + 1]
        qcol = q_t[:, bi:bi + 1]
        for j in range(2):
            s_old = st_ref[0, bi, j]
            b = beta[bi:bi + 1, j:j + 1]
            e = eg[bi:bi + 1, 2 + j:3 + j]
            ks = jnp.sum(kcol * s_old, axis=0, keepdims=True)
            u = b * vv[bi:bi + 1, hd * j:hd * (j + 1)] - (b * e) * ks
            s_new = s_old * e + kcol * u
            so_ref[0, bi, j] = s_new
            obuf[bi:bi + 1, hd * j:hd * (j + 1)] = jnp.sum(qcol * s_new, axis=0, keepdims=True)
    for j in range(2):
        o = obuf[:, hd * j:hd * (j + 1)]
        on = _rms(o) * nw_ref[...]
        o_ref[:, hd * j:hd * (j + 1)] = (on * _silu(z_ref[:, hd * j:hd * (j + 1)])).astype(BF16)


def _dn_step(proj, qkv, w, state, layer):
    m = proj.shape[0]
    hk, hd = w["hk"], DN_HEAD
    nb = _pick(m, (2 * STEP_BATCH, STEP_BATCH))
    ba0 = (6 * hk * hd) // LANES
    st_spec = pl.BlockSpec((1, nb, 2, hd, hd), lambda i, h: (layer, i, h, 0, 0))
    in_specs = [
        pl.BlockSpec((nb, hd), lambda i, h: (i, h)),
        pl.BlockSpec((nb, hd), lambda i, h: (i, hk + h)),
        pl.BlockSpec((nb, 2 * hd), lambda i, h: (i, hk + h)),
        pl.BlockSpec((nb, 2 * hd), lambda i, h: (i, 2 * hk + h)),
        pl.BlockSpec((nb, LANES), lambda i, h: (i, ba0 + h)),
        pl.BlockSpec((1, LANES), lambda i, h: (0, h)),
        pl.BlockSpec((1, LANES), lambda i, h: (0, h)),
        pl.BlockSpec((1, hd), lambda i, h: (0, 0)),
        st_spec,
    ]
    o, st = pl.pallas_call(
        _dn_step_kernel,
        grid=(m // nb, hk),
        in_specs=in_specs,
        out_specs=[pl.BlockSpec((nb, 2 * hd), lambda i, h: (i, h)), st_spec],
        out_shape=[jax.ShapeDtypeStruct((m, 2 * hk * hd), BF16),
                   jax.ShapeDtypeStruct(state.shape, F32)],
        scratch_shapes=[pltpu.VMEM((nb, 2 * hd), F32)],
        compiler_params=_cparams(("parallel", "parallel")),
        name="dn_step",
    )(qkv, qkv, qkv, proj, proj, w["a_log"], w["dt_bias"], w["norm"], state)
    return o, st


def _ret_step_kernel(q_ref, k_ref, v_ref, g_ref, cos_ref, s1_ref, s2_ref, lg_ref, st_ref, y_ref, so_ref, ybuf):
    nb = q_ref.shape[0]
    cos, s1, s2 = cos_ref[...], s1_ref[...], s2_ref[...]
    qq = _rotate(q_ref[...], cos, s1, s2)
    kk = _rotate(k_ref[...], cos, s1, s2) * (RET_DK ** -0.5)
    vv = v_ref[...]
    gamma = jnp.exp(lg_ref[0][0:1, 0:1])
    nk = RET_DK // LANES
    k_t = jnp.concatenate([_tr(kk[:, LANES * i:LANES * (i + 1)]) for i in range(nk)], axis=0)
    q_t = jnp.concatenate([_tr(qq[:, LANES * i:LANES * (i + 1)]) for i in range(nk)], axis=0)
    for bi in range(nb):
        s_new = st_ref[0, bi, 0] * gamma + k_t[:, bi:bi + 1] * vv[bi:bi + 1, :]
        so_ref[0, bi, 0] = s_new
        ybuf[bi:bi + 1, :] = jnp.sum(q_t[:, bi:bi + 1] * s_new, axis=0, keepdims=True)
    y_ref[...] = (_rms(ybuf[...]) * _silu(g_ref[...])).astype(BF16)


def _ret_step(proj, rope, lg, nheads, state, layer):
    m = proj.shape[0]
    nb = _pick(m, (STEP_BATCH,))
    v0 = 2 * nheads * RET_DK // RET_DV
    g0 = v0 + nheads
    cos, s1, s2 = rope
    st_spec = pl.BlockSpec((1, nb, 1, RET_DK, RET_DV), lambda i, h: (layer, i, h, 0, 0))
    in_specs = [
        pl.BlockSpec((nb, RET_DK), lambda i, h: (i, h)),
        pl.BlockSpec((nb, RET_DK), lambda i, h: (i, nheads + h)),
        pl.BlockSpec((nb, RET_DV), lambda i, h: (i, v0 + h)),
        pl.BlockSpec((nb, RET_DV), lambda i, h: (i, g0 + h)),
        pl.BlockSpec((1, RET_DK), lambda i, h: (0, 0)),
        pl.BlockSpec((1, RET_DK), lambda i, h: (0, 0)),
        pl.BlockSpec((1, RET_DK), lambda i, h: (0, 0)),
        pl.BlockSpec((1, SUBLANES, LANES), lambda i, h: (h, 0, 0)),
        st_spec,
    ]
    y, st = pl.pallas_call(
        _ret_step_kernel,
        grid=(m // nb, nheads),
        in_specs=in_specs,
        out_specs=[pl.BlockSpec((nb, RET_DV), lambda i, h: (i, h)), st_spec],
        out_shape=[jax.ShapeDtypeStruct((m, nheads * RET_DV), BF16),
                   jax.ShapeDtypeStruct(state.shape, F32)],
        scratch_shapes=[pltpu.VMEM((nb, RET_DV), F32)],
        compiler_params=_cparams(("parallel", "parallel")),
        name="ret_step",
    )(proj, proj, proj, proj, cos, s1, s2, lg, state)
    return y, st


def _pad_lanes(x):
    return jnp.pad(x, [(0, 0)] * (x.ndim - 1) + [(0, LANES - x.shape[-1])])


def _prep_ssm(w_in, conv_w, conv_b, dt_bias, a_log, d_skip, norm_w):
    d_model = w_in.shape[0]
    heads = dt_bias.shape[0]
    g = SSM_GROUPS
    hpg = heads // g
    d_inner = heads * SSM_HEAD_DIM
    conv_dim = conv_w.shape[1]
    main = d_inner + conv_dim
    per_group = lambda v: _pad_lanes(v.reshape(g, hpg)).reshape(1, g * LANES)
    w_dt = _pad_lanes(w_in[:, main:].reshape(d_model, g, hpg)).reshape(d_model, g * LANES)
    w_all = jnp.concatenate([w_in[:, :main].astype(BF16), w_dt.astype(BF16)], axis=1)
    return dict(d_inner=d_inner, hpg=hpg, conv_dim=conv_dim, w_in=w_all[None],
                conv_w=conv_w, conv_b=conv_b.reshape(1, conv_dim),
                dt_bias=per_group(dt_bias), a_log=per_group(a_log), d=per_group(d_skip),
                norm=norm_w.reshape(1, d_inner))


def _prep_dn(w_in, conv_w, a_log, dt_bias, norm_w):
    d_model = w_in.shape[0]
    hv = a_log.shape[0]
    hk = hv // 2
    conv_dim = conv_w.shape[1]
    main = conv_dim + hv * DN_HEAD
    wb = w_in[:, main:main + hv].reshape(d_model, hk, 2)
    wa = w_in[:, main + hv:main + 2 * hv].reshape(d_model, hk, 2)
    w_ba = _pad_lanes(jnp.concatenate([wb, wa], axis=-1)).reshape(d_model, hk * LANES)
    w_all = jnp.concatenate([w_in[:, :main].astype(BF16), w_ba.astype(BF16)], axis=1)
    per_head = lambda v: _pad_lanes(jnp.concatenate([jnp.zeros((hk, 2), F32), v.reshape(hk, 2)], axis=-1)
                                    ).reshape(1, hk * LANES)
    return dict(hk=hk, conv_dim=conv_dim, w_in=w_all[None], conv_w=conv_w,
                a_log=per_head(a_log), dt_bias=per_head(dt_bias), norm=norm_w.reshape(1, DN_HEAD))


def _rope_tables(pos):
    half = RET_DK // 2
    inv = 1.0 / (RET_ROPE_BASE ** jnp.linspace(0.0, 1.0, half, dtype=F32))
    ang = pos.astype(F32)[:, None] * inv[None, :]
    cos, sin, zero = jnp.cos(ang), jnp.sin(ang), jnp.zeros_like(ang)
    inter = lambda a, b: jnp.stack([a, b], axis=-1).reshape(pos.shape[0], RET_DK)
    return inter(cos, cos), inter(-sin, zero), inter(zero, sin)


def _forward(x, p, prompt, states, prm, wts):
    batch, seq, d_model = x.shape
    m = batch * seq
    h = x.reshape(m, d_model)
    p = p.reshape(p.shape[0], m, p.shape[-1])
    depth = prm["norm_mix_pre"].shape[0]
    ssm_s, ssm_c, dn_s, dn_c, ret_s = states
    o_ssm = o_ssm_c = o_dn = o_dn_c = o_ret = None
    p_ssm, p_ssm_c, p_dn, p_dn_c, p_ret = [], [], [], [], []
    tail = slice(seq - (CONV_W - 1), seq)
    for i in range(depth):
        kind, j = i % 3, i // 3
        gain = prm["norm_mix_pre"][i]
        if kind == 0:
            w = wts["ssm"][j]
            d_inner, conv_dim = w["d_inner"], w["conv_dim"]
            proj = _norm_mm(h, gain, w["w_in"], 0)
            if prompt:
                y, st = _ssd_prompt(proj, w, batch, seq)
                p_ssm.append(st)
                p_ssm_c.append(proj.reshape(batch, seq, -1)[:, tail, d_inner:d_inner + conv_dim])
            else:
                xbc, o_ssm_c = _conv_step(proj, d_inner, ssm_c, j, w["conv_w"], w["conv_b"], o_ssm_c)
                y, o_ssm = _ssd_step(proj, xbc, w, ssm_s, j, o_ssm)
            w_out = wts["ssm_out"]
        elif kind == 1:
            w = wts["dn"][j]
            conv_dim = w["conv_dim"]
            proj = _norm_mm(h, gain, w["w_in"], 0)
            if prompt:
                y, st = _dn_prompt(proj, w, batch, seq)
                p_dn.append(st)
                p_dn_c.append(proj.reshape(batch, seq, -1)[:, tail, :conv_dim])
            else:
                qkv, o_dn_c = _conv_step(proj, 0, dn_c, j, w["conv_w"], None, o_dn_c)
                y, o_dn = _dn_step(proj, qkv, w, dn_s, j)
            w_out = wts["dn_out"]
        else:
            nheads = wts["ret_heads"]
            proj = _norm_mm(h, gain, wts["ret_in"], j)
            if prompt:
                y, st = _ret_prompt(proj, wts["rope"], wts["lg"], nheads, batch, seq)
                p_ret.append(st)
            else:
                y, o_ret = _ret_step(proj, wts["rope"], wts["lg"], nheads, ret_s, j)
            w_out = wts["ret_out"]
        h = _mm_out(y, w_out, j, h, prm["norm_mix_post"][i])
        act = _ffn_in(h, prm["norm_ffn_pre"][i], wts["ffn_gate"], wts["ffn_up"], i)
        h = _mm_out(act, wts["ffn_down"], i, h, prm["norm_ffn_post"][i])
        h = _ple(h, prm["norm_ple"][i], wts["ple_gate"], p, wts["ple_proj"], i)
    h = h.reshape(batch, seq, d_model)
    if prompt:
        return (h, jnp.swapaxes(jnp.stack(p_ssm), -1, -2), jnp.stack(p_ssm_c), jnp.stack(p_dn),
                jnp.stack(p_dn_c), jnp.stack(p_ret))
    return (h, jnp.swapaxes(o_ssm, -1, -2), jnp.swapaxes(o_ssm_c, 1, 2), o_dn, jnp.swapaxes(o_dn_c, 1, 2), o_ret)


def kernel(x_prompt, x_sample, state_ssm, state_ssm_conv, state_delta, state_delta_conv, state_ret, p_prompt, p_sample, norm_mix_pre, norm_mix_post, norm_ffn_pre, norm_ffn_post, norm_ple, ffn_w_gate, ffn_w_up, ffn_w_down, ple_w_proj, ple_w_gate, ssm_w_in, ssm_conv_w, ssm_conv_b, ssm_dt_bias, ssm_a_log, ssm_d, ssm_norm, ssm_w_out, dn_w_in, dn_conv_w, dn_a_log, dn_dt_bias, dn_norm, dn_w_out, ret_w_in, ret_w_out):
    assert x_sample.shape[1] == 1, "the sample group advances one token per sequence"
    prm = dict(norm_mix_pre=norm_mix_pre, norm_mix_post=norm_mix_post, norm_ffn_pre=norm_ffn_pre,
               norm_ffn_post=norm_ffn_post, norm_ple=norm_ple)
    ret_heads = ret_w_out.shape[1] // RET_DV
    log_gamma = jnp.log(1.0 - 2.0 ** (-5.0 - jnp.arange(ret_heads, dtype=F32)))
    base = dict(
        ssm=[_prep_ssm(ssm_w_in[j], ssm_conv_w[j], ssm_conv_b[j], ssm_dt_bias[j], ssm_a_log[j], ssm_d[j],
                       ssm_norm[j]) for j in range(ssm_w_in.shape[0])],
        dn=[_prep_dn(dn_w_in[j], dn_conv_w[j], dn_a_log[j], dn_dt_bias[j], dn_norm[j])
            for j in range(dn_w_in.shape[0])],
        ssm_out=ssm_w_out.astype(BF16), dn_out=dn_w_out.astype(BF16),
        ret_in=ret_w_in.astype(BF16), ret_out=ret_w_out.astype(BF16), ret_heads=ret_heads,
        ffn_gate=ffn_w_gate.astype(BF16), ffn_up=ffn_w_up.astype(BF16), ffn_down=ffn_w_down.astype(BF16),
        ple_gate=ple_w_gate.astype(BF16), ple_proj=ple_w_proj.astype(BF16),
        lg=jnp.broadcast_to(log_gamma[:, None, None], (ret_heads, SUBLANES, LANES)),
    )
    seq = x_prompt.shape[1]
    pos_prompt = jnp.arange(seq, dtype=jnp.int32)
    pos_sample = PAST_LEN + jnp.arange(1, dtype=jnp.int32)
    out_p = _forward(x_prompt, p_prompt, True, (None,) * 5, prm, dict(base, rope=_rope_tables(pos_prompt)))
    states = (jnp.swapaxes(state_ssm, -1, -2), jnp.swapaxes(state_ssm_conv, 1, 2), state_delta,
              jnp.swapaxes(state_delta_conv, 1, 2), state_ret)
    out_s = _forward(x_sample, p_sample, False, states, prm, dict(base, rope=_rope_tables(pos_sample)))
    return (out_p[0], out_s[0]) + out_p[1:] + out_s[1:]
```

```python
import functools

import jax
import jax.numpy as jnp
from jax import lax
from jax.experimental import pallas as pl
from jax.experimental.pallas import tpu as pltpu

F32 = jnp.float32
BF16 = jnp.bfloat16

NORM_EPS = 1e-6
PAST_LEN = 16384
CONV_W = 4
SSM_GROUPS = 8
SSM_HEAD_DIM = 64
SSM_D_STATE = 128
DN_HEAD = 128
RET_DK = 256
RET_DV = 512
RET_ROPE_BASE = 10000.0

LANES = 128
SUBLANES = 8
NEG_BIG = -1e30
SSD_CHUNK = 128
RET_CHUNK = 128
DN_CHUNK = 64
DN_HEADS_PER_STEP = 8
RET_HEADS_PER_STEP = 4
STEP_BATCH = 8
VMEM_LIMIT = 52 * 1024 * 1024


def _pick(n, prefs):
    for p in prefs:
        if n % p == 0:
            return p
    return n


def _cparams(sem):
    return pltpu.CompilerParams(dimension_semantics=sem, vmem_limit_bytes=VMEM_LIMIT)


def _sigmoid(x):
    return 1.0 / (1.0 + jnp.exp(-x))


def _silu(x):
    h = 0.5 * x
    return h + h * jnp.tanh(h)


def _softplus(x):
    return jnp.maximum(x, 0.0) + jnp.log1p(jnp.exp(-jnp.abs(x)))


def _dot(a, b):
    return jnp.dot(a.astype(BF16), b.astype(BF16), preferred_element_type=F32)


def _dot_nt(a, b):
    return lax.dot_general(a.astype(BF16), b.astype(BF16), (((1,), (1,)), ((), ())),
                           preferred_element_type=F32)


def _dot01(m01, x):
    hi = x.astype(BF16)
    r = x - hi.astype(F32)
    mid = r.astype(BF16)
    lo = (r - mid.astype(F32)).astype(BF16)
    out = jnp.dot(m01, hi, preferred_element_type=F32)
    out = out + jnp.dot(m01, mid, preferred_element_type=F32)
    return out + jnp.dot(m01, lo, preferred_element_type=F32)


def _tr(x):
    r, c = x.shape
    assert c == LANES and r <= LANES
    if r < LANES:
        x = jnp.concatenate([x, jnp.zeros((LANES - r, c), x.dtype)], axis=0)
    return x.T[:, :r]


def _expand64(x, nheads):
    rows = x.shape[0]
    lane = lax.broadcasted_iota(jnp.int32, (rows, LANES), 1)
    parts = []
    for j in range(nheads // 2):
        a = jnp.broadcast_to(x[:, 2 * j:2 * j + 1], (rows, LANES))
        b = jnp.broadcast_to(x[:, 2 * j + 1:2 * j + 2], (rows, LANES))
        parts.append(jnp.where(lane < 64, a, b))
    return parts[0] if len(parts) == 1 else jnp.concatenate(parts, axis=1)


def _rms(y):
    return y * lax.rsqrt(jnp.mean(y * y, axis=-1, keepdims=True) + NORM_EPS)


def _conv_chunk(buf_ref, col0, x, w, bias):
    q, width = x.shape
    cols = slice(col0, col0 + width)
    buf_ref[SUBLANES:SUBLANES + q, cols] = x
    acc = x * w[CONV_W - 1:CONV_W, :]
    for s in range(1, CONV_W):
        acc = acc + buf_ref[SUBLANES - s:SUBLANES - s + q, cols] * w[CONV_W - 1 - s:CONV_W - s, :]
    buf_ref[0:SUBLANES, cols] = x[q - SUBLANES:q, :]
    if bias is not None:
        acc = acc + bias
    return acc


def _bdot(a, b):
    return lax.dot_general(a.astype(BF16), b.astype(BF16), (((2,), (1,)), ((0,), (0,))),
                           preferred_element_type=F32)


def _bdot_nt(a, b):
    return lax.dot_general(a.astype(BF16), b.astype(BF16), (((2,), (2,)), ((0,), (0,))),
                           preferred_element_type=F32)


def _tri_inv(a, row, col):
    q = a.shape[-1]
    eye = (row == col).astype(F32)[None]
    d = eye - jnp.where((jnp.right_shift(row, 1) == jnp.right_shift(col, 1))[None], a, 0.0)
    sh = 1
    while (1 << sh) < q:
        same_big = jnp.right_shift(row, sh + 1) == jnp.right_shift(col, sh + 1)
        diff_small = jnp.right_shift(row, sh) != jnp.right_shift(col, sh)
        lb = jnp.where((same_big & diff_small)[None], a, 0.0)
        d = d - _bdot(d, _bdot(lb, d))
        sh += 1
    return d


def _norm_kernel(x_ref, g_ref, o_ref):
    o_ref[...] = (_rms(x_ref[...]) * g_ref[...]).astype(BF16)


def _norm(x, gain):
    m, k = x.shape
    tm = _pick(m, (512, 256, 128))
    return pl.pallas_call(
        _norm_kernel,
        grid=(m // tm,),
        in_specs=[pl.BlockSpec((tm, k), lambda i: (i, 0)),
                  pl.BlockSpec((1, k), lambda i: (0, 0))],
        out_specs=pl.BlockSpec((tm, k), lambda i: (i, 0)),
        out_shape=jax.ShapeDtypeStruct((m, k), BF16),
        compiler_params=_cparams(("parallel",)),
        name="norm",
    )(x, gain.reshape(1, k))


def _mm_kernel(x_ref, w_ref, o_ref):
    o_ref[...] = jnp.dot(x_ref[...], w_ref[0], preferred_element_type=F32)


def _mm(xn, w, layer):
    m, k = xn.shape
    n = w.shape[2]
    tm = _pick(m, (1024, 512, 256, 128))
    tn = _pick(n, (1024, 512, 256, 128))
    return pl.pallas_call(
        _mm_kernel,
        grid=(m // tm, n // tn),
        in_specs=[pl.BlockSpec((tm, k), lambda i, j: (i, 0)),
                  pl.BlockSpec((1, k, tn), lambda i, j: (layer, 0, j))],
        out_specs=pl.BlockSpec((tm, tn), lambda i, j: (i, j)),
        out_shape=jax.ShapeDtypeStruct((m, n), F32),
        compiler_params=_cparams(("parallel", "arbitrary")),
        name="mm",
    )(xn, w)


def _mm_out_kernel(a_ref, w_ref, h_ref, g_ref, gn_ref, o_ref, xn_ref):
    y = jnp.dot(a_ref[...], w_ref[0], preferred_element_type=F32)
    h = h_ref[...] + _rms(y) * g_ref[...]
    o_ref[...] = h
    xn_ref[...] = (_rms(h) * gn_ref[...]).astype(BF16)


def _mm_out(a, w, layer, h, gain, gain_next):
    m, k = a.shape
    n = w.shape[2]
    tm = _pick(m, (256, 128))
    return pl.pallas_call(
        _mm_out_kernel,
        grid=(m // tm,),
        in_specs=[pl.BlockSpec((tm, k), lambda i: (i, 0)),
                  pl.BlockSpec((1, k, n), lambda i: (layer, 0, 0), pipeline_mode=pl.Buffered(1)),
                  pl.BlockSpec((tm, n), lambda i: (i, 0)),
                  pl.BlockSpec((1, n), lambda i: (0, 0)),
                  pl.BlockSpec((1, n), lambda i: (0, 0))],
        out_specs=[pl.BlockSpec((tm, n), lambda i: (i, 0)),
                   pl.BlockSpec((tm, n), lambda i: (i, 0))],
        out_shape=[jax.ShapeDtypeStruct((m, n), F32),
                   jax.ShapeDtypeStruct((m, n), BF16)],
        compiler_params=_cparams(("parallel",)),
        name="mm_out",
    )(a, w, h, gain.reshape(1, n), gain_next.reshape(1, n))


def _ffn_in_kernel(x_ref, wg_ref, wu_ref, o_ref):
    xn = x_ref[...]
    gate = jnp.dot(xn, wg_ref[0], preferred_element_type=F32)
    up = jnp.dot(xn, wu_ref[0], preferred_element_type=F32)
    o_ref[...] = (_silu(gate) * up).astype(BF16)


def _ffn_in(xn, wg, wu, layer):
    m, k = xn.shape
    n = wg.shape[2]
    tm = _pick(m, (2048, 1024, 512, 256, 128))
    tn = _pick(n, (512, 256, 128))
    return pl.pallas_call(
        _ffn_in_kernel,
        grid=(m // tm, n // tn),
        in_specs=[pl.BlockSpec((tm, k), lambda i, j: (i, 0)),
                  pl.BlockSpec((1, k, tn), lambda i, j: (layer, 0, j)),
                  pl.BlockSpec((1, k, tn), lambda i, j: (layer, 0, j))],
        out_specs=pl.BlockSpec((tm, tn), lambda i, j: (i, j)),
        out_shape=jax.ShapeDtypeStruct((m, n), BF16),
        compiler_params=_cparams(("parallel", "arbitrary")),
        name="ffn_in",
    )(xn, wg, wu)


def _ple_kernel(*refs, has_next):
    h_ref, xn_ref, wg_ref, p_ref, wp_ref = refs[:5]
    gate = _sigmoid(jnp.dot(xn_ref[...], wg_ref[0], preferred_element_type=F32))
    proj = jnp.dot(p_ref[0].astype(BF16), wp_ref[0], preferred_element_type=F32)
    h = h_ref[...] + proj * gate
    if has_next:
        gn_ref, o_ref, on_ref = refs[5:]
        on_ref[...] = (_rms(h) * gn_ref[...]).astype(BF16)
    else:
        o_ref, = refs[5:]
    o_ref[...] = h


def _ple(h, xn, wg, p, wp, layer, gain_next):
    m, k = h.shape
    n = wg.shape[2]
    pd = p.shape[2]
    tm = _pick(m, (512, 256, 128))
    has_next = gain_next is not None
    row = lambda i: (i, 0)
    in_specs = [pl.BlockSpec((tm, k), row),
                pl.BlockSpec((tm, k), row),
                pl.BlockSpec((1, k, n), lambda i: (layer, 0, 0), pipeline_mode=pl.Buffered(1)),
                pl.BlockSpec((1, tm, pd), lambda i: (layer, i, 0)),
                pl.BlockSpec((1, pd, n), lambda i: (layer, 0, 0), pipeline_mode=pl.Buffered(1))]
    args = [h, xn, wg, p, wp]
    out_specs = [pl.BlockSpec((tm, n), row)]
    out_shape = [jax.ShapeDtypeStruct((m, n), F32)]
    if has_next:
        in_specs.append(pl.BlockSpec((1, n), lambda i: (0, 0)))
        args.append(gain_next.reshape(1, n))
        out_specs.append(pl.BlockSpec((tm, n), row))
        out_shape.append(jax.ShapeDtypeStruct((m, n), BF16))
    out = pl.pallas_call(
        functools.partial(_ple_kernel, has_next=has_next),
        grid=(m // tm,),
        in_specs=in_specs,
        out_specs=out_specs,
        out_shape=out_shape,
        compiler_params=_cparams(("parallel",)),
        name="ple",
    )(*args)
    return (out[0], out[1]) if has_next else (out[0], None)


def _ssd_prompt_kernel(z_ref, xs_ref, b_ref, c_ref, dt_ref, wx_ref, wb_ref, wc_ref, bx_ref, bb_ref, bc_ref,
                       dtb_ref, alog_ref, d_ref, nw_ref, y_ref, st_ref, s_scr, cbuf, *, nheads):
    c = pl.program_id(2)
    q, width = xs_ref.shape
    n = b_ref.shape[1]
    hd = SSM_HEAD_DIM

    @pl.when(c == 0)
    def _():
        s_scr[...] = jnp.zeros_like(s_scr)
        cbuf[0:SUBLANES, :] = jnp.zeros((SUBLANES, cbuf.shape[1]), F32)

    xs = _silu(_conv_chunk(cbuf, 0, xs_ref[...], wx_ref[...], bx_ref[...]))
    bm = _silu(_conv_chunk(cbuf, width, b_ref[...], wb_ref[...], bb_ref[...]))
    cm = _silu(_conv_chunk(cbuf, width + n, c_ref[...], wc_ref[...], bc_ref[...]))

    dtv = _softplus(dt_ref[...] + dtb_ref[...])
    la = dtv * (-jnp.exp(alog_ref[...]))
    row = lax.broadcasted_iota(jnp.int32, (q, q), 0)
    col = lax.broadcasted_iota(jnp.int32, (q, q), 1)
    causal = row >= col
    acs = _dot01(causal.astype(BF16), la)
    acs_t = _tr(acs)
    scores = _dot_nt(cm, bm)
    dt_t = _tr(dtv)
    s_old = s_scr[...]
    y = _dot_nt(cm, s_old) * _expand64(jnp.exp(acs), nheads)
    lane = lax.broadcasted_iota(jnp.int32, (q, LANES), 1)
    parts = []
    for j in range(nheads // 2):
        xp = xs[:, LANES * j:LANES * (j + 1)].astype(BF16)
        ys = []
        for t in range(2):
            r = 2 * j + t
            seg = acs[:, r:r + 1] - acs_t[r:r + 1, :]
            decay = jnp.exp(jnp.where(causal, seg, NEG_BIG))
            ys.append(jnp.dot((scores * decay * dt_t[r:r + 1, :]).astype(BF16), xp, preferred_element_type=F32))
        parts.append(jnp.where(lane < 64, ys[0], ys[1]))
    y = y + (parts[0] if len(parts) == 1 else jnp.concatenate(parts, axis=1))
    y = y + xs * _expand64(d_ref[...], nheads)

    last = acs[q - 1:q, :]
    wsc_t = _tr(jnp.exp(last - acs) * dtv)
    sub = lax.broadcasted_iota(jnp.int32, (LANES, q), 0)
    wv_t = []
    for j in range(width // LANES):
        scale = jnp.where(sub < hd, wsc_t[2 * j:2 * j + 1, :], wsc_t[2 * j + 1:2 * j + 2, :])
        wv_t.append(_tr(xs[:, LANES * j:LANES * (j + 1)]) * scale)
    wv_t = wv_t[0] if len(wv_t) == 1 else jnp.concatenate(wv_t, axis=0)
    e_last = jnp.exp(last)
    e_rows = jnp.concatenate([jnp.broadcast_to(e_last[:, r:r + 1], (hd, n)) for r in range(nheads)], axis=0)
    s_new = s_old * e_rows + _dot(wv_t, bm)
    s_scr[...] = s_new

    y = y * _silu(z_ref[...])
    y_ref[...] = (_rms(y) * nw_ref[...]).astype(BF16)

    @pl.when(c == pl.num_programs(2) - 1)
    def _():
        for r in range(nheads):
            st_ref[0, r] = s_new[hd * r:hd * (r + 1), :]


def _ssd_prompt(proj, w, batch, seq):
    d_inner, nh, g, n = w["d_inner"], w["hpg"], SSM_GROUPS, SSM_D_STATE
    width = nh * SSM_HEAD_DIM
    q = _pick(seq, (SSD_CHUNK, 64, 32, 16, 8))
    nc = seq // q
    xs0, b0, c0 = d_inner // width, 2 * d_inner // n, 2 * d_inner // n + g
    dt0 = (d_inner + w["conv_dim"]) // LANES
    cw0 = d_inner // n
    tok = lambda b, gi, c: b * nc + c
    in_specs = [
        pl.BlockSpec((q, width), lambda b, gi, c: (tok(b, gi, c), gi)),
        pl.BlockSpec((q, width), lambda b, gi, c: (tok(b, gi, c), xs0 + gi)),
        pl.BlockSpec((q, n), lambda b, gi, c: (tok(b, gi, c), b0 + gi)),
        pl.BlockSpec((q, n), lambda b, gi, c: (tok(b, gi, c), c0 + gi)),
        pl.BlockSpec((q, LANES), lambda b, gi, c: (tok(b, gi, c), dt0 + gi)),
        pl.BlockSpec((CONV_W, width), lambda b, gi, c: (0, gi)),
        pl.BlockSpec((CONV_W, n), lambda b, gi, c: (0, cw0 + gi)),
        pl.BlockSpec((CONV_W, n), lambda b, gi, c: (0, cw0 + g + gi)),
        pl.BlockSpec((1, width), lambda b, gi, c: (0, gi)),
        pl.BlockSpec((1, n), lambda b, gi, c: (0, cw0 + gi)),
        pl.BlockSpec((1, n), lambda b, gi, c: (0, cw0 + g + gi)),
        pl.BlockSpec((1, LANES), lambda b, gi, c: (0, gi)),
        pl.BlockSpec((1, LANES), lambda b, gi, c: (0, gi)),
        pl.BlockSpec((1, LANES), lambda b, gi, c: (0, gi)),
        pl.BlockSpec((1, width), lambda b, gi, c: (0, gi)),
    ]
    y, st = pl.pallas_call(
        functools.partial(_ssd_prompt_kernel, nheads=nh),
        grid=(batch, g, nc),
        in_specs=in_specs,
        out_specs=[pl.BlockSpec((q, width), lambda b, gi, c: (tok(b, gi, c), gi)),
                   pl.BlockSpec((1, nh, SSM_HEAD_DIM, n), lambda b, gi, c: (b, gi, 0, 0))],
        out_shape=[jax.ShapeDtypeStruct((batch * seq, d_inner), BF16),
                   jax.ShapeDtypeStruct((batch, g * nh, SSM_HEAD_DIM, n), F32)],
        scratch_shapes=[pltpu.VMEM((width, n), F32),
                        pltpu.VMEM((q + SUBLANES, width + 2 * n), F32)],
        compiler_params=_cparams(("parallel", "parallel", "arbitrary")),
        name="ssd_prompt",
    )(proj, proj, proj, proj, proj, w["conv_w"], w["conv_w"], w["conv_w"], w["conv_b"], w["conv_b"], w["conv_b"],
      w["dt_bias"], w["a_log"], w["d"], w["norm"])
    return y, st


def _dn_prompt_kernel(q_ref, k_ref, v_ref, z_ref, ba_ref, wq_ref, wk_ref, wv_ref, alog_ref, dtb_ref, nw_ref,
                      o_ref, st_ref, s_scr, cbuf, *, hb):
    c = pl.program_id(2)
    q = q_ref.shape[0]
    hd = DN_HEAD

    @pl.when(c == 0)
    def _():
        s_scr[...] = jnp.zeros_like(s_scr)
        cbuf[0:SUBLANES, :] = jnp.zeros((SUBLANES, cbuf.shape[1]), F32)

    qa = _silu(_conv_chunk(cbuf, 0, q_ref[...], wq_ref[...], None))
    ka = _silu(_conv_chunk(cbuf, hb * hd, k_ref[...], wk_ref[...], None))
    va = _silu(_conv_chunk(cbuf, 2 * hb * hd, v_ref[...], wv_ref[...], None))

    ba = ba_ref[...]
    beta_all = _sigmoid(ba)
    gg = -jnp.exp(alog_ref[...]) * _softplus(ba + dtb_ref[...])
    row = lax.broadcasted_iota(jnp.int32, (q, q), 0)
    col = lax.broadcasted_iota(jnp.int32, (q, q), 1)
    incl = row >= col
    strict = row > col
    acs_all = _dot01(incl.astype(BF16), gg)

    heads = [(kh, j) for kh in range(hb) for j in range(2)]
    q3 = jnp.stack([qa[:, kh * hd:(kh + 1) * hd] for kh in range(hb)])
    k3 = jnp.stack([ka[:, kh * hd:(kh + 1) * hd] for kh in range(hb)])
    q3 = q3 * lax.rsqrt(jnp.sum(q3 * q3, axis=-1, keepdims=True) + NORM_EPS) * (hd ** -0.5)
    k3 = k3 * lax.rsqrt(jnp.sum(k3 * k3, axis=-1, keepdims=True) + NORM_EPS)
    acs_k = [acs_all[:, kh * LANES:(kh + 1) * LANES] for kh in range(hb)]
    acs_t = [_tr(a) for a in acs_k]
    kk3 = _bdot_nt(k3, k3)
    qk3 = _bdot_nt(q3, k3)
    k_t3 = jnp.stack([_tr(k3[kh]) for kh in range(hb)])

    a_col = jnp.stack([acs_k[kh][:, 2 + j:3 + j] for kh, j in heads])
    a_row = jnp.stack([acs_t[kh][2 + j:3 + j, :] for kh, j in heads])
    last = jnp.stack([acs_k[kh][q - 1:q, 2 + j:3 + j] for kh, j in heads])
    bcol = jnp.stack([beta_all[:, kh * LANES + j:kh * LANES + j + 1] for kh, j in heads])
    rep = lambda x: jnp.stack([x[kh] for kh, _ in heads])
    k_v, q_v = rep(k3), rep(q3)
    gam = jnp.exp(jnp.where(incl[None], a_col - a_row, NEG_BIG))
    a_mat = jnp.where(strict[None], rep(kk3) * gam * bcol, 0.0)
    t_mat = _tri_inv(a_mat, row, col)
    v3 = jnp.stack([va[:, i * hd:(i + 1) * hd] for i in range(len(heads))])
    e_col = jnp.exp(a_col)
    rhs = jnp.concatenate([bcol * v3, (bcol * e_col) * k_v], axis=2)
    sol = _bdot(t_mat, rhs)
    s_old = s_scr[...]
    u = sol[:, :, :hd] - _bdot(sol[:, :, hd:], s_old)
    o = _bdot(rep(qk3) * gam, u) + _bdot(q_v, s_old) * e_col
    s_scr[...] = s_old * jnp.exp(last) + _bdot(rep(k_t3), jnp.exp(last - a_col) * u)
    on = _rms(o) * nw_ref[...]
    for i in range(len(heads)):
        hs = slice(i * hd, (i + 1) * hd)
        o_ref[:, hs] = (on[i] * _silu(z_ref[:, hs])).astype(BF16)

    @pl.when(c == pl.num_programs(2) - 1)
    def _():
        st_ref[0] = s_scr[...]


def _dn_prompt(proj, w, batch, seq):
    hk, hd = w["hk"], DN_HEAD
    hb = _pick(hk, (DN_HEADS_PER_STEP, 2, 1))
    q = _pick(seq, (DN_CHUNK, 32, 16, 8))
    nc = seq // q
    nhb = hk // hb
    tok = lambda b, h, c: b * nc + c
    ba0 = (6 * hk * hd) // (hb * LANES)
    in_specs = [
        pl.BlockSpec((q, hb * hd), lambda b, h, c: (tok(b, h, c), h)),
        pl.BlockSpec((q, hb * hd), lambda b, h, c: (tok(b, h, c), nhb + h)),
        pl.BlockSpec((q, 2 * hb * hd), lambda b, h, c: (tok(b, h, c), nhb + h)),
        pl.BlockSpec((q, 2 * hb * hd), lambda b, h, c: (tok(b, h, c), 2 * nhb + h)),
        pl.BlockSpec((q, hb * LANES), lambda b, h, c: (tok(b, h, c), ba0 + h)),
        pl.BlockSpec((CONV_W, hb * hd), lambda b, h, c: (0, h)),
        pl.BlockSpec((CONV_W, hb * hd), lambda b, h, c: (0, nhb + h)),
        pl.BlockSpec((CONV_W, 2 * hb * hd), lambda b, h, c: (0, nhb + h)),
        pl.BlockSpec((1, hb * LANES), lambda b, h, c: (0, h)),
        pl.BlockSpec((1, hb * LANES), lambda b, h, c: (0, h)),
        pl.BlockSpec((1, hd), lambda b, h, c: (0, 0)),
    ]
    o, st = pl.pallas_call(
        functools.partial(_dn_prompt_kernel, hb=hb),
        grid=(batch, nhb, nc),
        in_specs=in_specs,
        out_specs=[pl.BlockSpec((q, 2 * hb * hd), lambda b, h, c: (tok(b, h, c), h)),
                   pl.BlockSpec((1, 2 * hb, hd, hd), lambda b, h, c: (b, h, 0, 0))],
        out_shape=[jax.ShapeDtypeStruct((batch * seq, 2 * hk * hd), BF16),
                   jax.ShapeDtypeStruct((batch, 2 * hk, hd, hd), F32)],
        scratch_shapes=[pltpu.VMEM((2 * hb, hd, hd), F32),
                        pltpu.VMEM((q + SUBLANES, 4 * hb * hd), F32)],
        compiler_params=_cparams(("parallel", "parallel", "arbitrary")),
        name="dn_prompt",
    )(proj, proj, proj, proj, proj, w["conv_w"], w["conv_w"], w["conv_w"], w["a_log"], w["dt_bias"], w["norm"])
    return o, st


def _rotate(x, cos, s1, s2):
    w = x.shape[1]
    return x * cos + pltpu.roll(x, w - 1, 1) * s1 + pltpu.roll(x, 1, 1) * s2


def _ret_prompt_kernel(q_ref, k_ref, v_ref, g_ref, cos_ref, s1_ref, s2_ref, lg_ref, y_ref, st_ref, s_scr, *, hb):
    c = pl.program_id(2)
    q = q_ref.shape[0]

    @pl.when(c == 0)
    def _():
        s_scr[...] = jnp.zeros_like(s_scr)

    tile = lambda t: t if hb == 1 else jnp.concatenate([t] * hb, axis=1)
    cos, s1, s2 = tile(cos_ref[...]), tile(s1_ref[...]), tile(s2_ref[...])
    qa = _rotate(q_ref[...], cos, s1, s2)
    ka = _rotate(k_ref[...], cos, s1, s2) * (RET_DK ** -0.5)
    q3 = jnp.stack([qa[:, RET_DK * h:RET_DK * (h + 1)] for h in range(hb)])
    k3 = jnp.stack([ka[:, RET_DK * h:RET_DK * (h + 1)] for h in range(hb)])
    v3 = jnp.stack([v_ref[:, RET_DV * h:RET_DV * (h + 1)] for h in range(hb)])
    lg = lg_ref[...][:, 0:1, 0:1]
    row = lax.broadcasted_iota(jnp.int32, (q, q), 0)
    col = lax.broadcasted_iota(jnp.int32, (q, q), 1)
    dist = (row - col).astype(F32)[None]
    decay = jnp.exp(jnp.where((row >= col)[None], dist * lg, NEG_BIG))
    pos = lax.broadcasted_iota(jnp.int32, (1, q, 1), 1).astype(F32)
    scores = _bdot_nt(q3, k3)
    s_old = s_scr[...]
    y = _bdot(scores * decay, v3) + _bdot(q3, s_old) * jnp.exp((pos + 1.0) * lg)
    wv = jnp.exp((float(q - 1) - pos) * lg) * v3
    k_t = jnp.stack([jnp.concatenate([_tr(k3[h][:, LANES * i:LANES * (i + 1)]) for i in range(RET_DK // LANES)],
                                     axis=0) for h in range(hb)])
    s_new = s_old * jnp.exp(float(q) * lg) + _bdot(k_t, wv)
    s_scr[...] = s_new
    yn = _rms(y)
    for h in range(hb):
        hs = slice(RET_DV * h, RET_DV * (h + 1))
        y_ref[:, hs] = (yn[h] * _silu(g_ref[:, hs])).astype(BF16)

    @pl.when(c == pl.num_programs(2) - 1)
    def _():
        st_ref[0] = s_new


def _ret_prompt(proj, rope, lg, nheads, batch, seq):
    q = _pick(seq, (RET_CHUNK, 64, 32, 16, 8))
    hb = _pick(nheads, (RET_HEADS_PER_STEP, 2, 1))
    nc = seq // q
    nhb = nheads // hb
    tok = lambda b, h, c: b * nc + c
    v0 = 2 * nheads * RET_DK // (hb * RET_DV)
    g0 = v0 + nhb
    cos, s1, s2 = rope
    in_specs = [
        pl.BlockSpec((q, hb * RET_DK), lambda b, h, c: (tok(b, h, c), h)),
        pl.BlockSpec((q, hb * RET_DK), lambda b, h, c: (tok(b, h, c), nhb + h)),
        pl.BlockSpec((q, hb * RET_DV), lambda b, h, c: (tok(b, h, c), v0 + h)),
        pl.BlockSpec((q, hb * RET_DV), lambda b, h, c: (tok(b, h, c), g0 + h)),
        pl.BlockSpec((q, RET_DK), lambda b, h, c: (c, 0)),
        pl.BlockSpec((q, RET_DK), lambda b, h, c: (c, 0)),
        pl.BlockSpec((q, RET_DK), lambda b, h, c: (c, 0)),
        pl.BlockSpec((hb, SUBLANES, LANES), lambda b, h, c: (h, 0, 0)),
    ]
    y, st = pl.pallas_call(
        functools.partial(_ret_prompt_kernel, hb=hb),
        grid=(batch, nhb, nc),
        in_specs=in_specs,
        out_specs=[pl.BlockSpec((q, hb * RET_DV), lambda b, h, c: (tok(b, h, c), h)),
                   pl.BlockSpec((1, hb, RET_DK, RET_DV), lambda b, h, c: (b, h, 0, 0))],
        out_shape=[jax.ShapeDtypeStruct((batch * seq, nheads * RET_DV), BF16),
                   jax.ShapeDtypeStruct((batch, nheads, RET_DK, RET_DV), F32)],
        scratch_shapes=[pltpu.VMEM((hb, RET_DK, RET_DV), F32)],
        compiler_params=_cparams(("parallel", "parallel", "arbitrary")),
        name="ret_prompt",
    )(proj, proj, proj, proj, cos, s1, s2, lg)
    return y, st


def _alias_args(prev, n_in, out_idx):
    if prev is None:
        return [], [], {}
    return [pl.BlockSpec(memory_space=pl.ANY)], [prev], {n_in: out_idx}


def _conv_step_kernel(*refs, has_bias, aliased):
    refs = list(refs)
    x_ref, cs_ref, w_ref = refs[:3]
    b_ref = refs[3] if has_bias else None
    o_ref, cn_ref = refs[3 + int(has_bias) + int(aliased):]
    w = w_ref[...]
    x = x_ref[...]
    c0, c1, c2 = cs_ref[0, 0], cs_ref[0, 1], cs_ref[0, 2]
    acc = c0 * w[0:1, :] + c1 * w[1:2, :] + c2 * w[2:3, :] + x * w[3:4, :]
    if has_bias:
        acc = acc + b_ref[...]
    o_ref[...] = _silu(acc)
    cn_ref[0, 0] = c1
    cn_ref[0, 1] = c2
    cn_ref[0, 2] = x


def _conv_step(proj, col0, cstate, layer, conv_w, conv_b, prev):
    m = proj.shape[0]
    cdim = cstate.shape[3]
    cb = _pick(cdim, (512, 256, 128))
    x0 = col0 // cb
    taps = CONV_W - 1
    in_specs = [pl.BlockSpec((m, cb), lambda j: (0, x0 + j)),
                pl.BlockSpec((1, taps, m, cb), lambda j: (layer, 0, 0, j)),
                pl.BlockSpec((CONV_W, cb), lambda j: (0, j))]
    args = [proj, cstate, conv_w]
    if conv_b is not None:
        in_specs.append(pl.BlockSpec((1, cb), lambda j: (0, j)))
        args.append(conv_b)
    a_specs, a_args, aliases = _alias_args(prev, len(args), 1)
    return pl.pallas_call(
        functools.partial(_conv_step_kernel, has_bias=conv_b is not None, aliased=prev is not None),
        grid=(cdim // cb,),
        in_specs=in_specs + a_specs,
        out_specs=[pl.BlockSpec((m, cb), lambda j: (0, j)),
                   pl.BlockSpec((1, taps, m, cb), lambda j: (layer, 0, 0, j))],
        out_shape=[jax.ShapeDtypeStruct((m, cdim), F32),
                   jax.ShapeDtypeStruct(cstate.shape, F32)],
        input_output_aliases=aliases,
        compiler_params=_cparams(("parallel",)),
        name="conv_step",
    )(*args, *a_args)


def _ssd_step_kernel(*refs, nheads):
    z_ref, xs_ref, b_ref, c_ref, dt_ref, dtb_ref, alog_ref, d_ref, nw_ref, st_ref = refs[:10]
    y_ref, so_ref, ytb = refs[-3:]
    nb = xs_ref.shape[0]
    hd = SSM_HEAD_DIM
    xs = xs_ref[...]
    dtv = _softplus(dt_ref[...] + dtb_ref[...])
    decay = jnp.exp(dtv * (-jnp.exp(alog_ref[...])))
    pad = jnp.zeros((LANES - nb, LANES), F32)
    kmat = jnp.concatenate([b_ref[...], pad], axis=0).astype(BF16)
    qmat = jnp.concatenate([c_ref[...], pad], axis=0).T.astype(BF16)
    v_all = xs * _expand64(dtv, nheads)
    v_t = [jnp.concatenate([v_all[:, LANES * j:LANES * (j + 1)], pad], axis=0).T
           for j in range(nheads // 2)]
    lane = lax.broadcasted_iota(jnp.int32, (hd, LANES), 1)
    for r in range(nheads):
        vt = v_t[r // 2][hd * (r % 2):hd * (r % 2 + 1), :]
        for bi in range(nb):
            outer = jnp.dot(jnp.where(lane == bi, vt, 0.0).astype(BF16), kmat, preferred_element_type=F32)
            so_ref[0, bi, r] = st_ref[0, bi, r] * decay[bi:bi + 1, r:r + 1] + outer
    for r in range(nheads):
        acc = jnp.zeros((hd, LANES), F32)
        for bi in range(nb):
            yb = jnp.dot(so_ref[0, bi, r].astype(BF16), qmat, preferred_element_type=F32)
            acc = jnp.where(lane == bi, yb, acc)
        ytb[hd * r:hd * (r + 1), :] = acc
    y = [ytb[LANES * j:LANES * (j + 1), :].T[:nb, :] for j in range(nheads // 2)]
    y = y[0] if len(y) == 1 else jnp.concatenate(y, axis=1)
    y = y + xs * _expand64(d_ref[...], nheads)
    y = y * _silu(z_ref[...])
    y_ref[...] = (_rms(y) * nw_ref[...]).astype(BF16)


def _ssd_step(proj, xbc, w, state, layer, prev):
    m = proj.shape[0]
    d_inner, nh, g, n = w["d_inner"], w["hpg"], SSM_GROUPS, SSM_D_STATE
    width = nh * SSM_HEAD_DIM
    nb = _pick(m, (2 * STEP_BATCH, STEP_BATCH))
    b0 = d_inner // n
    dt0 = (d_inner + w["conv_dim"]) // LANES
    st_spec = pl.BlockSpec((1, nb, nh, SSM_HEAD_DIM, n), lambda i, gi: (layer, i, gi, 0, 0))
    in_specs = [
        pl.BlockSpec((nb, width), lambda i, gi: (i, gi)),
        pl.BlockSpec((nb, width), lambda i, gi: (i, gi)),
        pl.BlockSpec((nb, n), lambda i, gi: (i, b0 + gi)),
        pl.BlockSpec((nb, n), lambda i, gi: (i, b0 + g + gi)),
        pl.BlockSpec((nb, LANES), lambda i, gi: (i, dt0 + gi)),
        pl.BlockSpec((1, LANES), lambda i, gi: (0, gi)),
        pl.BlockSpec((1, LANES), lambda i, gi: (0, gi)),
        pl.BlockSpec((1, LANES), lambda i, gi: (0, gi)),
        pl.BlockSpec((1, width), lambda i, gi: (0, gi)),
        st_spec,
    ]
    args = [proj, xbc, xbc, xbc, proj, w["dt_bias"], w["a_log"], w["d"], w["norm"], state]
    a_specs, a_args, aliases = _alias_args(prev, len(args), 1)
    y, st = pl.pallas_call(
        functools.partial(_ssd_step_kernel, nheads=nh),
        grid=(m // nb, g),
        in_specs=in_specs + a_specs,
        out_specs=[pl.BlockSpec((nb, width), lambda i, gi: (i, gi)), st_spec],
        out_shape=[jax.ShapeDtypeStruct((m, d_inner), BF16),
                   jax.ShapeDtypeStruct(state.shape, F32)],
        scratch_shapes=[pltpu.VMEM((width, LANES), F32)],
        input_output_aliases=aliases,
        compiler_params=_cparams(("parallel", "parallel")),
        name="ssd_step",
    )(*args, *a_args)
    return y, st


def _dn_step_kernel(q_ref, k_ref, v_ref, z_ref, ba_ref, alog_ref, dtb_ref, nw_ref, st_ref,
                    o_ref, so_ref, obuf):
    nb = q_ref.shape[0]
    hd = DN_HEAD
    qq = q_ref[...]
    kk = k_ref[...]
    qq = qq * lax.rsqrt(jnp.sum(qq * qq, axis=-1, keepdims=True) + NORM_EPS) * (hd ** -0.5)
    kk = kk * lax.rsqrt(jnp.sum(kk * kk, axis=-1, keepdims=True) + NORM_EPS)
    vv = v_ref[...]
    ba = ba_ref[...]
    beta = _sigmoid(ba)
    eg = jnp.exp(-jnp.exp(alog_ref[...]) * _softplus(ba + dtb_ref[...]))
    k_t = _tr(kk)
    q_t = _tr(qq)
    for bi in range(nb):
        kcol = k_t[:, bi:bi + 1]
        qcol = q_t[:, bi:bi + 1]
        for j in range(2):
            s_old = st_ref[0, bi, j]
            b = beta[bi:bi + 1, j:j + 1]
            e = eg[bi:bi + 1, 2 + j:3 + j]
            ks = jnp.sum(kcol * s_old, axis=0, keepdims=True)
            u = b * vv[bi:bi + 1, hd * j:hd * (j + 1)] - (b * e) * ks
            s_new = s_old * e + kcol * u
            so_ref[0, bi, j] = s_new
            obuf[bi:bi + 1, hd * j:hd * (j + 1)] = jnp.sum(qcol * s_new, axis=0, keepdims=True)
    for j in range(2):
        o = obuf[:, hd * j:hd * (j + 1)]
        on = _rms(o) * nw_ref[...]
        o_ref[:, hd * j:hd * (j + 1)] = (on * _silu(z_ref[:, hd * j:hd * (j + 1)])).astype(BF16)


def _dn_step(proj, qkv, w, state, layer):
    m = proj.shape[0]
    hk, hd = w["hk"], DN_HEAD
    nb = _pick(m, (2 * STEP_BATCH, STEP_BATCH))
    ba0 = (6 * hk * hd) // LANES
    st_spec = pl.BlockSpec((1, nb, 2, hd, hd), lambda i, h: (layer, i, h, 0, 0))
    in_specs = [
        pl.BlockSpec((nb, hd), lambda i, h: (i, h)),
        pl.BlockSpec((nb, hd), lambda i, h: (i, hk + h)),
        pl.BlockSpec((nb, 2 * hd), lambda i, h: (i, hk + h)),
        pl.BlockSpec((nb, 2 * hd), lambda i, h: (i, 2 * hk + h)),
        pl.BlockSpec((nb, LANES), lambda i, h: (i, ba0 + h)),
        pl.BlockSpec((1, LANES), lambda i, h: (0, h)),
        pl.BlockSpec((1, LANES), lambda i, h: (0, h)),
        pl.BlockSpec((1, hd), lambda i, h: (0, 0)),
        st_spec,
    ]
    o, st = pl.pallas_call(
        _dn_step_kernel,
        grid=(m // nb, hk),
        in_specs=in_specs,
        out_specs=[pl.BlockSpec((nb, 2 * hd), lambda i, h: (i, h)), st_spec],
        out_shape=[jax.ShapeDtypeStruct((m, 2 * hk * hd), BF16),
                   jax.ShapeDtypeStruct(state.shape, F32)],
        scratch_shapes=[pltpu.VMEM((nb, 2 * hd), F32)],
        compiler_params=_cparams(("parallel", "parallel")),
        name="dn_step",
    )(qkv, qkv, qkv, proj, proj, w["a_log"], w["dt_bias"], w["norm"], state)
    return o, st


def _ret_step_kernel(q_ref, k_ref, v_ref, g_ref, cos_ref, s1_ref, s2_ref, lg_ref, st_ref, y_ref, so_ref, ybuf):
    nb = q_ref.shape[0]
    cos, s1, s2 = cos_ref[...], s1_ref[...], s2_ref[...]
    qq = _rotate(q_ref[...], cos, s1, s2)
    kk = _rotate(k_ref[...], cos, s1, s2) * (RET_DK ** -0.5)
    vv = v_ref[...]
    gamma = jnp.exp(lg_ref[0][0:1, 0:1])
    nk = RET_DK // LANES
    k_t = jnp.concatenate([_tr(kk[:, LANES * i:LANES * (i + 1)]) for i in range(nk)], axis=0)
    q_t = jnp.concatenate([_tr(qq[:, LANES * i:LANES * (i + 1)]) for i in range(nk)], axis=0)
    for bi in range(nb):
        s_new = st_ref[0, bi, 0] * gamma + k_t[:, bi:bi + 1] * vv[bi:bi + 1, :]
        so_ref[0, bi, 0] = s_new
        ybuf[bi:bi + 1, :] = jnp.sum(q_t[:, bi:bi + 1] * s_new, axis=0, keepdims=True)
    y_ref[...] = (_rms(ybuf[...]) * _silu(g_ref[...])).astype(BF16)


def _ret_step(proj, rope, lg, nheads, state, layer):
    m = proj.shape[0]
    nb = _pick(m, (STEP_BATCH,))
    v0 = 2 * nheads * RET_DK // RET_DV
    g0 = v0 + nheads
    cos, s1, s2 = rope
    st_spec = pl.BlockSpec((1, nb, 1, RET_DK, RET_DV), lambda i, h: (layer, i, h, 0, 0))
    in_specs = [
        pl.BlockSpec((nb, RET_DK), lambda i, h: (i, h)),
        pl.BlockSpec((nb, RET_DK), lambda i, h: (i, nheads + h)),
        pl.BlockSpec((nb, RET_DV), lambda i, h: (i, v0 + h)),
        pl.BlockSpec((nb, RET_DV), lambda i, h: (i, g0 + h)),
        pl.BlockSpec((1, RET_DK), lambda i, h: (0, 0)),
        pl.BlockSpec((1, RET_DK), lambda i, h: (0, 0)),
        pl.BlockSpec((1, RET_DK), lambda i, h: (0, 0)),
        pl.BlockSpec((1, SUBLANES, LANES), lambda i, h: (h, 0, 0)),
        st_spec,
    ]
    y, st = pl.pallas_call(
        _ret_step_kernel,
        grid=(m // nb, nheads),
        in_specs=in_specs,
        out_specs=[pl.BlockSpec((nb, RET_DV), lambda i, h: (i, h)), st_spec],
        out_shape=[jax.ShapeDtypeStruct((m, nheads * RET_DV), BF16),
                   jax.ShapeDtypeStruct(state.shape, F32)],
        scratch_shapes=[pltpu.VMEM((nb, RET_DV), F32)],
        compiler_params=_cparams(("parallel", "parallel")),
        name="ret_step",
    )(proj, proj, proj, proj, cos, s1, s2, lg, state)
    return y, st


def _pad_lanes(x):
    return jnp.pad(x, [(0, 0)] * (x.ndim - 1) + [(0, LANES - x.shape[-1])])


def _prep_ssm(w_in, conv_w, conv_b, dt_bias, a_log, d_skip, norm_w):
    d_model = w_in.shape[0]
    heads = dt_bias.shape[0]
    g = SSM_GROUPS
    hpg = heads // g
    d_inner = heads * SSM_HEAD_DIM
    conv_dim = conv_w.shape[1]
    main = d_inner + conv_dim
    per_group = lambda v: _pad_lanes(v.reshape(g, hpg)).reshape(1, g * LANES)
    w_dt = _pad_lanes(w_in[:, main:].reshape(d_model, g, hpg)).reshape(d_model, g * LANES)
    w_all = jnp.concatenate([w_in[:, :main].astype(BF16), w_dt.astype(BF16)], axis=1)
    return dict(d_inner=d_inner, hpg=hpg, conv_dim=conv_dim, w_in=w_all[None],
                conv_w=conv_w, conv_b=conv_b.reshape(1, conv_dim),
                dt_bias=per_group(dt_bias), a_log=per_group(a_log), d=per_group(d_skip),
                norm=norm_w.reshape(1, d_inner))


def _prep_dn(w_in, conv_w, a_log, dt_bias, norm_w):
    d_model = w_in.shape[0]
    hv = a_log.shape[0]
    hk = hv // 2
    conv_dim = conv_w.shape[1]
    main = conv_dim + hv * DN_HEAD
    wb = w_in[:, main:main + hv].reshape(d_model, hk, 2)
    wa = w_in[:, main + hv:main + 2 * hv].reshape(d_model, hk, 2)
    w_ba = _pad_lanes(jnp.concatenate([wb, wa], axis=-1)).reshape(d_model, hk * LANES)
    w_all = jnp.concatenate([w_in[:, :main].astype(BF16), w_ba.astype(BF16)], axis=1)
    per_head = lambda v: _pad_lanes(jnp.concatenate([jnp.zeros((hk, 2), F32), v.reshape(hk, 2)], axis=-1)
                                    ).reshape(1, hk * LANES)
    return dict(hk=hk, conv_dim=conv_dim, w_in=w_all[None], conv_w=conv_w,
                a_log=per_head(a_log), dt_bias=per_head(dt_bias), norm=norm_w.reshape(1, DN_HEAD))


def _rope_tables(pos):
    half = RET_DK // 2
    inv = 1.0 / (RET_ROPE_BASE ** jnp.linspace(0.0, 1.0, half, dtype=F32))
    ang = pos.astype(F32)[:, None] * inv[None, :]
    cos, sin, zero = jnp.cos(ang), jnp.sin(ang), jnp.zeros_like(ang)
    inter = lambda a, b: jnp.stack([a, b], axis=-1).reshape(pos.shape[0], RET_DK)
    return inter(cos, cos), inter(-sin, zero), inter(zero, sin)


def _forward(x, p, prompt, states, prm, wts):
    batch, seq, d_model = x.shape
    m = batch * seq
    h = x.reshape(m, d_model)
    p = p.reshape(p.shape[0], m, p.shape[-1])
    depth = prm["norm_mix_pre"].shape[0]
    ssm_s, ssm_c, dn_s, dn_c, ret_s = states
    o_ssm = o_ssm_c = o_dn = o_dn_c = o_ret = None
    p_ssm, p_ssm_c, p_dn, p_dn_c, p_ret = [], [], [], [], []
    tail = slice(seq - (CONV_W - 1), seq)
    xn = _norm(h, prm["norm_mix_pre"][0])
    for i in range(depth):
        kind, j = i % 3, i // 3
        if kind == 0:
            w = wts["ssm"][j]
            d_inner, conv_dim = w["d_inner"], w["conv_dim"]
            proj = _mm(xn, w["w_in"], 0)
            if prompt:
                y, st = _ssd_prompt(proj, w, batch, seq)
                p_ssm.append(st)
                p_ssm_c.append(proj.reshape(batch, seq, -1)[:, tail, d_inner:d_inner + conv_dim])
            else:
                xbc, o_ssm_c = _conv_step(proj, d_inner, ssm_c, j, w["conv_w"], w["conv_b"], o_ssm_c)
                y, o_ssm = _ssd_step(proj, xbc, w, ssm_s, j, o_ssm)
            w_out = wts["ssm_out"]
        elif kind == 1:
            w = wts["dn"][j]
            conv_dim = w["conv_dim"]
            proj = _mm(xn, w["w_in"], 0)
            if prompt:
                y, st = _dn_prompt(proj, w, batch, seq)
                p_dn.append(st)
                p_dn_c.append(proj.reshape(batch, seq, -1)[:, tail, :conv_dim])
            else:
                qkv, o_dn_c = _conv_step(proj, 0, dn_c, j, w["conv_w"], None, o_dn_c)
                y, o_dn = _dn_step(proj, qkv, w, dn_s, j)
            w_out = wts["dn_out"]
        else:
            nheads = wts["ret_heads"]
            proj = _mm(xn, wts["ret_in"], j)
            if prompt:
                y, st = _ret_prompt(proj, wts["rope"], wts["lg"], nheads, batch, seq)
                p_ret.append(st)
            else:
                y, o_ret = _ret_step(proj, wts["rope"], wts["lg"], nheads, ret_s, j)
            w_out = wts["ret_out"]
        h, xn = _mm_out(y, w_out, j, h, prm["norm_mix_post"][i], prm["norm_ffn_pre"][i])
        act = _ffn_in(xn, wts["ffn_gate"], wts["ffn_up"], i)
        h, xn = _mm_out(act, wts["ffn_down"], i, h, prm["norm_ffn_post"][i], prm["norm_ple"][i])
        gain_next = prm["norm_mix_pre"][i + 1] if i + 1 < depth else None
        h, xn = _ple(h, xn, wts["ple_gate"], p, wts["ple_proj"], i, gain_next)
    h = h.reshape(batch, seq, d_model)
    if prompt:
        return (h, jnp.swapaxes(jnp.stack(p_ssm), -1, -2), jnp.stack(p_ssm_c), jnp.stack(p_dn),
                jnp.stack(p_dn_c), jnp.stack(p_ret))
    return (h, jnp.swapaxes(o_ssm, -1, -2), jnp.swapaxes(o_ssm_c, 1, 2), o_dn, jnp.swapaxes(o_dn_c, 1, 2), o_ret)


def kernel(x_prompt, x_sample, state_ssm, state_ssm_conv, state_delta, state_delta_conv, state_ret, p_prompt, p_sample, norm_mix_pre, norm_mix_post, norm_ffn_pre, norm_ffn_post, norm_ple, ffn_w_gate, ffn_w_up, ffn_w_down, ple_w_proj, ple_w_gate, ssm_w_in, ssm_conv_w, ssm_conv_b, ssm_dt_bias, ssm_a_log, ssm_d, ssm_norm, ssm_w_out, dn_w_in, dn_conv_w, dn_a_log, dn_dt_bias, dn_norm, dn_w_out, ret_w_in, ret_w_out):
    assert x_sample.shape[1] == 1, "the sample group advances one token per sequence"
    prm = dict(norm_mix_pre=norm_mix_pre, norm_mix_post=norm_mix_post, norm_ffn_pre=norm_ffn_pre,
               norm_ffn_post=norm_ffn_post, norm_ple=norm_ple)
    ret_heads = ret_w_out.shape[1] // RET_DV
    log_gamma = jnp.log(1.0 - 2.0 ** (-5.0 - jnp.arange(ret_heads, dtype=F32)))
    base = dict(
        ssm=[_prep_ssm(ssm_w_in[j], ssm_conv_w[j], ssm_conv_b[j], ssm_dt_bias[j], ssm_a_log[j], ssm_d[j],
                       ssm_norm[j]) for j in range(ssm_w_in.shape[0])],
        dn=[_prep_dn(dn_w_in[j], dn_conv_w[j], dn_a_log[j], dn_dt_bias[j], dn_norm[j])
            for j in range(dn_w_in.shape[0])],
        ssm_out=ssm_w_out.astype(BF16), dn_out=dn_w_out.astype(BF16),
        ret_in=ret_w_in.astype(BF16), ret_out=ret_w_out.astype(BF16), ret_heads=ret_heads,
        ffn_gate=ffn_w_gate.astype(BF16), ffn_up=ffn_w_up.astype(BF16), ffn_down=ffn_w_down.astype(BF16),
        ple_gate=ple_w_gate.astype(BF16), ple_proj=ple_w_proj.astype(BF16),
        lg=jnp.broadcast_to(log_gamma[:, None, None], (ret_heads, SUBLANES, LANES)),
    )
    seq = x_prompt.shape[1]
    pos_prompt = jnp.arange(seq, dtype=jnp.int32)
    pos_sample = PAST_LEN + jnp.arange(1, dtype=jnp.int32)
    out_p = _forward(x_prompt, p_prompt, True, (None,) * 5, prm, dict(base, rope=_rope_tables(pos_prompt)))
    states = (jnp.swapaxes(state_ssm, -1, -2), jnp.swapaxes(state_ssm_conv, 1, 2), state_delta,
              jnp.swapaxes(state_delta_conv, 1, 2), state_ret)
    out_s = _forward(x_sample, p_sample, False, states, prm, dict(base, rope=_rope_tables(pos_sample)))
    return (out_p[0], out_s[0]) + out_p[1:] + out_s[1:]
```

```python
import functools

import jax
import jax.numpy as jnp
from jax import lax
from jax.experimental import pallas as pl
from jax.experimental.pallas import tpu as pltpu

F32 = jnp.float32
BF16 = jnp.bfloat16

NORM_EPS = 1e-6
PAST_LEN = 16384
CONV_W = 4
SSM_GROUPS = 8
SSM_HEAD_DIM = 64
SSM_D_STATE = 128
DN_HEAD = 128
RET_DK = 256
RET_DV = 512
RET_ROPE_BASE = 10000.0

LANES = 128
SUBLANES = 8
NEG_BIG = -1e30
SSD_CHUNK = 128
RET_CHUNK = 128
DN_CHUNK = 64
DN_HEADS_PER_STEP = 8
RET_HEADS_PER_STEP = 4
STEP_BATCH = 8
VMEM_LIMIT = 52 * 1024 * 1024


def _pick(n, prefs):
    for p in prefs:
        if n % p == 0:
            return p
    return n


def _cparams(sem):
    return pltpu.CompilerParams(dimension_semantics=sem, vmem_limit_bytes=VMEM_LIMIT)


def _sigmoid(x):
    return 1.0 / (1.0 + jnp.exp(-x))


def _silu(x):
    h = 0.5 * x
    return h + h * jnp.tanh(h)


def _softplus(x):
    return jnp.maximum(x, 0.0) + jnp.log1p(jnp.exp(-jnp.abs(x)))


def _dot(a, b):
    return jnp.dot(a.astype(BF16), b.astype(BF16), preferred_element_type=F32)


def _dot_nt(a, b):
    return lax.dot_general(a.astype(BF16), b.astype(BF16), (((1,), (1,)), ((), ())),
                           preferred_element_type=F32)


def _dot01(m01, x):
    hi = x.astype(BF16)
    r = x - hi.astype(F32)
    mid = r.astype(BF16)
    lo = (r - mid.astype(F32)).astype(BF16)
    out = jnp.dot(m01, hi, preferred_element_type=F32)
    out = out + jnp.dot(m01, mid, preferred_element_type=F32)
    return out + jnp.dot(m01, lo, preferred_element_type=F32)


def _tr(x):
    r, c = x.shape
    assert c == LANES and r <= LANES
    if r < LANES:
        x = jnp.concatenate([x, jnp.zeros((LANES - r, c), x.dtype)], axis=0)
    return x.T[:, :r]


def _group_lanes(x, g, width):
    return pltpu.roll(x, lax.rem(LANES - g * width, LANES), 1)


def _expand64(x, nheads):
    rows = x.shape[0]
    lane = lax.broadcasted_iota(jnp.int32, (rows, LANES), 1)
    parts = []
    for j in range(nheads // 2):
        a = jnp.broadcast_to(x[:, 2 * j:2 * j + 1], (rows, LANES))
        b = jnp.broadcast_to(x[:, 2 * j + 1:2 * j + 2], (rows, LANES))
        parts.append(jnp.where(lane < 64, a, b))
    return parts[0] if len(parts) == 1 else jnp.concatenate(parts, axis=1)


def _rms(y):
    return y * lax.rsqrt(jnp.mean(y * y, axis=-1, keepdims=True) + NORM_EPS)


def _conv_chunk(buf_ref, col0, x, w, bias):
    q, width = x.shape
    cols = slice(col0, col0 + width)
    buf_ref[SUBLANES:SUBLANES + q, cols] = x
    acc = x * w[CONV_W - 1:CONV_W, :]
    for s in range(1, CONV_W):
        acc = acc + buf_ref[SUBLANES - s:SUBLANES - s + q, cols] * w[CONV_W - 1 - s:CONV_W - s, :]
    buf_ref[0:SUBLANES, cols] = x[q - SUBLANES:q, :]
    if bias is not None:
        acc = acc + bias
    return acc


def _bdot(a, b):
    return lax.dot_general(a.astype(BF16), b.astype(BF16), (((2,), (1,)), ((0,), (0,))),
                           preferred_element_type=F32)


def _bdot_nt(a, b):
    return lax.dot_general(a.astype(BF16), b.astype(BF16), (((2,), (2,)), ((0,), (0,))),
                           preferred_element_type=F32)


def _tri_inv(a, row, col):
    q = a.shape[-1]
    eye = (row == col).astype(F32)[None]
    d = eye - jnp.where((jnp.right_shift(row, 1) == jnp.right_shift(col, 1))[None], a, 0.0)
    sh = 1
    while (1 << sh) < q:
        same_big = jnp.right_shift(row, sh + 1) == jnp.right_shift(col, sh + 1)
        diff_small = jnp.right_shift(row, sh) != jnp.right_shift(col, sh)
        lb = jnp.where((same_big & diff_small)[None], a, 0.0)
        d = d - _bdot(d, _bdot(lb, d))
        sh += 1
    return d


def _norm_kernel(x_ref, g_ref, o_ref):
    o_ref[...] = (_rms(x_ref[...]) * g_ref[...]).astype(BF16)


def _norm(x, gain):
    m, k = x.shape
    tm = _pick(m, (512, 256, 128))
    return pl.pallas_call(
        _norm_kernel,
        grid=(m // tm,),
        in_specs=[pl.BlockSpec((tm, k), lambda i: (i, 0)),
                  pl.BlockSpec((1, k), lambda i: (0, 0))],
        out_specs=pl.BlockSpec((tm, k), lambda i: (i, 0)),
        out_shape=jax.ShapeDtypeStruct((m, k), BF16),
        compiler_params=_cparams(("parallel",)),
        name="norm",
    )(x, gain.reshape(1, k))


def _mm_kernel(x_ref, w_ref, o_ref, wb_ref, *, transposed):
    @pl.when(pl.program_id(1) == 0)
    def _():
        wb_ref[...] = w_ref[0].astype(BF16)

    if transposed:
        o_ref[...] = lax.dot_general(x_ref[...], wb_ref[...], (((1,), (1,)), ((), ())),
                                     preferred_element_type=F32)
    else:
        o_ref[...] = jnp.dot(x_ref[...], wb_ref[...], preferred_element_type=F32)


def _mm(xn, w, layer, n, transposed):
    m, k = xn.shape
    tm = _pick(m, (1024, 512, 256, 128))
    tn = _pick(n, (1024, 512, 256, 128))
    if transposed:
        w_spec = pl.BlockSpec((1, tn, k), lambda j, i: (layer, j, 0))
        w_scratch = pltpu.VMEM((tn, k), BF16)
    else:
        w_spec = pl.BlockSpec((1, k, tn), lambda j, i: (layer, 0, j))
        w_scratch = pltpu.VMEM((k, tn), BF16)
    return pl.pallas_call(
        functools.partial(_mm_kernel, transposed=transposed),
        grid=(n // tn, m // tm),
        in_specs=[pl.BlockSpec((tm, k), lambda j, i: (i, 0)), w_spec],
        out_specs=pl.BlockSpec((tm, tn), lambda j, i: (i, j)),
        out_shape=jax.ShapeDtypeStruct((m, n), F32),
        scratch_shapes=[w_scratch],
        compiler_params=_cparams(("parallel", "arbitrary")),
        name="mm",
    )(xn, w)


def _mm_small_kernel(x_ref, w_ref, o_ref):
    o_ref[...] = jnp.dot(x_ref[...], w_ref[...], preferred_element_type=F32)


def _mm_small(xn, w):
    m, k = xn.shape
    n = w.shape[1]
    tm = _pick(m, (1024, 512, 256, 128))
    return pl.pallas_call(
        _mm_small_kernel,
        grid=(m // tm,),
        in_specs=[pl.BlockSpec((tm, k), lambda i: (i, 0)),
                  pl.BlockSpec((k, n), lambda i: (0, 0))],
        out_specs=pl.BlockSpec((tm, n), lambda i: (i, 0)),
        out_shape=jax.ShapeDtypeStruct((m, n), F32),
        compiler_params=_cparams(("parallel",)),
        name="mm_small",
    )(xn, w)


def _mm_out_kernel(a_ref, w_ref, h_ref, g_ref, gn_ref, o_ref, xn_ref):
    y = jnp.dot(a_ref[...], w_ref[0], preferred_element_type=F32)
    h = h_ref[...] + _rms(y) * g_ref[...]
    o_ref[...] = h
    xn_ref[...] = (_rms(h) * gn_ref[...]).astype(BF16)


def _mm_out(a, w, layer, h, gain, gain_next):
    m, k = a.shape
    n = w.shape[2]
    tm = _pick(m, (256, 128))
    return pl.pallas_call(
        _mm_out_kernel,
        grid=(m // tm,),
        in_specs=[pl.BlockSpec((tm, k), lambda i: (i, 0)),
                  pl.BlockSpec((1, k, n), lambda i: (layer, 0, 0), pipeline_mode=pl.Buffered(1)),
                  pl.BlockSpec((tm, n), lambda i: (i, 0)),
                  pl.BlockSpec((1, n), lambda i: (0, 0)),
                  pl.BlockSpec((1, n), lambda i: (0, 0))],
        out_specs=[pl.BlockSpec((tm, n), lambda i: (i, 0)),
                   pl.BlockSpec((tm, n), lambda i: (i, 0))],
        out_shape=[jax.ShapeDtypeStruct((m, n), F32),
                   jax.ShapeDtypeStruct((m, n), BF16)],
        compiler_params=_cparams(("parallel",)),
        name="mm_out",
    )(a, w, h, gain.reshape(1, n), gain_next.reshape(1, n))


def _ffn_in_kernel(x_ref, wg_ref, wu_ref, o_ref, wgb_ref, wub_ref):
    @pl.when(pl.program_id(1) == 0)
    def _():
        wgb_ref[...] = wg_ref[0].astype(BF16)
        wub_ref[...] = wu_ref[0].astype(BF16)

    xn = x_ref[...]
    gate = jnp.dot(xn, wgb_ref[...], preferred_element_type=F32)
    up = jnp.dot(xn, wub_ref[...], preferred_element_type=F32)
    o_ref[...] = (_silu(gate) * up).astype(BF16)


def _ffn_in(xn, wg, wu, layer):
    m, k = xn.shape
    n = wg.shape[2]
    tm = _pick(m, (1024, 512, 256, 128))
    tn = _pick(n, (512, 256, 128))
    return pl.pallas_call(
        _ffn_in_kernel,
        grid=(n // tn, m // tm),
        in_specs=[pl.BlockSpec((tm, k), lambda j, i: (i, 0)),
                  pl.BlockSpec((1, k, tn), lambda j, i: (layer, 0, j)),
                  pl.BlockSpec((1, k, tn), lambda j, i: (layer, 0, j))],
        out_specs=pl.BlockSpec((tm, tn), lambda j, i: (i, j)),
        out_shape=jax.ShapeDtypeStruct((m, n), BF16),
        scratch_shapes=[pltpu.VMEM((k, tn), BF16), pltpu.VMEM((k, tn), BF16)],
        compiler_params=_cparams(("parallel", "arbitrary")),
        name="ffn_in",
    )(xn, wg, wu)


def _ple_kernel(*refs, has_next):
    h_ref, xn_ref, wg_ref, p_ref, wp_ref = refs[:5]
    gate = _sigmoid(jnp.dot(xn_ref[...], wg_ref[0], preferred_element_type=F32))
    proj = jnp.dot(p_ref[0].astype(BF16), wp_ref[0], preferred_element_type=F32)
    h = h_ref[...] + proj * gate
    if has_next:
        gn_ref, o_ref, on_ref = refs[5:]
        on_ref[...] = (_rms(h) * gn_ref[...]).astype(BF16)
    else:
        o_ref, = refs[5:]
    o_ref[...] = h


def _ple(h, xn, wg, p, wp, layer, gain_next):
    m, k = h.shape
    n = wg.shape[2]
    pd = p.shape[2]
    tm = _pick(m, (512, 256, 128))
    has_next = gain_next is not None
    row = lambda i: (i, 0)
    in_specs = [pl.BlockSpec((tm, k), row),
                pl.BlockSpec((tm, k), row),
                pl.BlockSpec((1, k, n), lambda i: (layer, 0, 0), pipeline_mode=pl.Buffered(1)),
                pl.BlockSpec((1, tm, pd), lambda i: (layer, i, 0)),
                pl.BlockSpec((1, pd, n), lambda i: (layer, 0, 0), pipeline_mode=pl.Buffered(1))]
    args = [h, xn, wg, p, wp]
    out_specs = [pl.BlockSpec((tm, n), row)]
    out_shape = [jax.ShapeDtypeStruct((m, n), F32)]
    if has_next:
        in_specs.append(pl.BlockSpec((1, n), lambda i: (0, 0)))
        args.append(gain_next.reshape(1, n))
        out_specs.append(pl.BlockSpec((tm, n), row))
        out_shape.append(jax.ShapeDtypeStruct((m, n), BF16))
    out = pl.pallas_call(
        functools.partial(_ple_kernel, has_next=has_next),
        grid=(m // tm,),
        in_specs=in_specs,
        out_specs=out_specs,
        out_shape=out_shape,
        compiler_params=_cparams(("parallel",)),
        name="ple",
    )(*args)
    return (out[0], out[1]) if has_next else (out[0], None)


def _ssd_prompt_kernel(z_ref, xs_ref, b_ref, c_ref, dt_ref, wx_ref, wb_ref, wc_ref, bx_ref, bb_ref, bc_ref,
                       dtb_ref, alog_ref, d_ref, nw_ref, y_ref, st_ref, s_scr, cbuf, *, nheads):
    c = pl.program_id(2)
    q, width = xs_ref.shape
    n = b_ref.shape[1]
    hd = SSM_HEAD_DIM

    @pl.when(c == 0)
    def _():
        s_scr[...] = jnp.zeros_like(s_scr)
        cbuf[0:SUBLANES, :] = jnp.zeros((SUBLANES, cbuf.shape[1]), F32)

    xs = _silu(_conv_chunk(cbuf, 0, xs_ref[...], wx_ref[...], bx_ref[...]))
    bm = _silu(_conv_chunk(cbuf, width, b_ref[...], wb_ref[...], bb_ref[...]))
    cm = _silu(_conv_chunk(cbuf, width + n, c_ref[...], wc_ref[...], bc_ref[...]))

    dtv = _softplus(_group_lanes(dt_ref[...], pl.program_id(1), nheads) + dtb_ref[...])
    la = dtv * (-jnp.exp(alog_ref[...]))
    row = lax.broadcasted_iota(jnp.int32, (q, q), 0)
    col = lax.broadcasted_iota(jnp.int32, (q, q), 1)
    causal = row >= col
    acs = _dot01(causal.astype(BF16), la)
    acs_t = _tr(acs)
    scores = _dot_nt(cm, bm)
    dt_t = _tr(dtv)
    s_old = s_scr[...]
    y = _dot_nt(cm, s_old) * _expand64(jnp.exp(acs), nheads)
    lane = lax.broadcasted_iota(jnp.int32, (q, LANES), 1)
    parts = []
    for j in range(nheads // 2):
        xp = xs[:, LANES * j:LANES * (j + 1)].astype(BF16)
        ys = []
        for t in range(2):
            r = 2 * j + t
            seg = acs[:, r:r + 1] - acs_t[r:r + 1, :]
            decay = jnp.exp(jnp.where(causal, seg, NEG_BIG))
            ys.append(jnp.dot((scores * decay * dt_t[r:r + 1, :]).astype(BF16), xp, preferred_element_type=F32))
        parts.append(jnp.where(lane < 64, ys[0], ys[1]))
    y = y + (parts[0] if len(parts) == 1 else jnp.concatenate(parts, axis=1))
    y = y + xs * _expand64(d_ref[...], nheads)

    last = acs[q - 1:q, :]
    wsc_t = _tr(jnp.exp(last - acs) * dtv)
    sub = lax.broadcasted_iota(jnp.int32, (LANES, q), 0)
    wv_t = []
    for j in range(width // LANES):
        scale = jnp.where(sub < hd, wsc_t[2 * j:2 * j + 1, :], wsc_t[2 * j + 1:2 * j + 2, :])
        wv_t.append(_tr(xs[:, LANES * j:LANES * (j + 1)]) * scale)
    wv_t = wv_t[0] if len(wv_t) == 1 else jnp.concatenate(wv_t, axis=0)
    e_last = jnp.exp(last)
    e_rows = jnp.concatenate([jnp.broadcast_to(e_last[:, r:r + 1], (hd, n)) for r in range(nheads)], axis=0)
    s_new = s_old * e_rows + _dot(wv_t, bm)
    s_scr[...] = s_new

    y = y * _silu(z_ref[...])
    y_ref[...] = (_rms(y) * nw_ref[...]).astype(BF16)

    @pl.when(c == pl.num_programs(2) - 1)
    def _():
        for r in range(nheads):
            st_ref[0, r] = s_new[hd * r:hd * (r + 1), :]


def _ssd_prompt(proj, dt, w, batch, seq):
    d_inner, nh, g, n = w["d_inner"], w["hpg"], SSM_GROUPS, SSM_D_STATE
    width = nh * SSM_HEAD_DIM
    q = _pick(seq, (SSD_CHUNK, 64, 32, 16, 8))
    nc = seq // q
    xs0, b0, c0 = d_inner // width, 2 * d_inner // n, 2 * d_inner // n + g
    cw0 = d_inner // n
    tok = lambda b, gi, c: b * nc + c
    in_specs = [
        pl.BlockSpec((q, width), lambda b, gi, c: (tok(b, gi, c), gi)),
        pl.BlockSpec((q, width), lambda b, gi, c: (tok(b, gi, c), xs0 + gi)),
        pl.BlockSpec((q, n), lambda b, gi, c: (tok(b, gi, c), b0 + gi)),
        pl.BlockSpec((q, n), lambda b, gi, c: (tok(b, gi, c), c0 + gi)),
        pl.BlockSpec((q, LANES), lambda b, gi, c: (tok(b, gi, c), 0)),
        pl.BlockSpec((CONV_W, width), lambda b, gi, c: (0, gi)),
        pl.BlockSpec((CONV_W, n), lambda b, gi, c: (0, cw0 + gi)),
        pl.BlockSpec((CONV_W, n), lambda b, gi, c: (0, cw0 + g + gi)),
        pl.BlockSpec((1, width), lambda b, gi, c: (0, gi)),
        pl.BlockSpec((1, n), lambda b, gi, c: (0, cw0 + gi)),
        pl.BlockSpec((1, n), lambda b, gi, c: (0, cw0 + g + gi)),
        pl.BlockSpec((1, LANES), lambda b, gi, c: (0, gi)),
        pl.BlockSpec((1, LANES), lambda b, gi, c: (0, gi)),
        pl.BlockSpec((1, LANES), lambda b, gi, c: (0, gi)),
        pl.BlockSpec((1, width), lambda b, gi, c: (0, gi)),
    ]
    y, st = pl.pallas_call(
        functools.partial(_ssd_prompt_kernel, nheads=nh),
        grid=(batch, g, nc),
        in_specs=in_specs,
        out_specs=[pl.BlockSpec((q, width), lambda b, gi, c: (tok(b, gi, c), gi)),
                   pl.BlockSpec((1, nh, SSM_HEAD_DIM, n), lambda b, gi, c: (b, gi, 0, 0))],
        out_shape=[jax.ShapeDtypeStruct((batch * seq, d_inner), BF16),
                   jax.ShapeDtypeStruct((batch, g * nh, SSM_HEAD_DIM, n), F32)],
        scratch_shapes=[pltpu.VMEM((width, n), F32),
                        pltpu.VMEM((q + SUBLANES, width + 2 * n), F32)],
        compiler_params=_cparams(("parallel", "parallel", "arbitrary")),
        name="ssd_prompt",
    )(proj, proj, proj, proj, dt, w["conv_w"], w["conv_w"], w["conv_w"], w["conv_b"], w["conv_b"], w["conv_b"],
      w["dt_bias"], w["a_log"], w["d"], w["norm"])
    return y, st


def _dn_prompt_kernel(q_ref, k_ref, v_ref, z_ref, ba_ref, wq_ref, wk_ref, wv_ref, alog_ref, dtb_ref, nw_ref,
                      o_ref, st_ref, s_scr, cbuf, *, hb, hv):
    c = pl.program_id(2)
    q = q_ref.shape[0]
    hd = DN_HEAD

    @pl.when(c == 0)
    def _():
        s_scr[...] = jnp.zeros_like(s_scr)
        cbuf[0:SUBLANES, :] = jnp.zeros((SUBLANES, cbuf.shape[1]), F32)

    qa = _silu(_conv_chunk(cbuf, 0, q_ref[...], wq_ref[...], None))
    ka = _silu(_conv_chunk(cbuf, hb * hd, k_ref[...], wk_ref[...], None))
    va = _silu(_conv_chunk(cbuf, 2 * hb * hd, v_ref[...], wv_ref[...], None))

    ba = _group_lanes(ba_ref[...], pl.program_id(1), 2 * hb)
    beta_all = _sigmoid(ba)
    gg = -jnp.exp(alog_ref[...]) * _softplus(ba + dtb_ref[...])
    row = lax.broadcasted_iota(jnp.int32, (q, q), 0)
    col = lax.broadcasted_iota(jnp.int32, (q, q), 1)
    incl = row >= col
    strict = row > col
    acs = _dot01(incl.astype(BF16), gg)
    acs_t = _tr(acs)

    heads = [(kh, j) for kh in range(hb) for j in range(2)]
    q3 = jnp.stack([qa[:, kh * hd:(kh + 1) * hd] for kh in range(hb)])
    k3 = jnp.stack([ka[:, kh * hd:(kh + 1) * hd] for kh in range(hb)])
    q3 = q3 * lax.rsqrt(jnp.sum(q3 * q3, axis=-1, keepdims=True) + NORM_EPS) * (hd ** -0.5)
    k3 = k3 * lax.rsqrt(jnp.sum(k3 * k3, axis=-1, keepdims=True) + NORM_EPS)
    kk3 = _bdot_nt(k3, k3)
    qk3 = _bdot_nt(q3, k3)
    k_t3 = jnp.stack([_tr(k3[kh]) for kh in range(hb)])

    nh = len(heads)
    a_col = jnp.stack([acs[:, hv + i:hv + i + 1] for i in range(nh)])
    a_row = jnp.stack([acs_t[hv + i:hv + i + 1, :] for i in range(nh)])
    last = jnp.stack([acs[q - 1:q, hv + i:hv + i + 1] for i in range(nh)])
    bcol = jnp.stack([beta_all[:, i:i + 1] for i in range(nh)])
    rep = lambda x: jnp.stack([x[kh] for kh, _ in heads])
    k_v, q_v = rep(k3), rep(q3)
    gam = jnp.exp(jnp.where(incl[None], a_col - a_row, NEG_BIG))
    a_mat = jnp.where(strict[None], rep(kk3) * gam * bcol, 0.0)
    t_mat = _tri_inv(a_mat, row, col)
    v3 = jnp.stack([va[:, i * hd:(i + 1) * hd] for i in range(len(heads))])
    e_col = jnp.exp(a_col)
    rhs = jnp.concatenate([bcol * v3, (bcol * e_col) * k_v], axis=2)
    sol = _bdot(t_mat, rhs)
    s_old = s_scr[...]
    u = sol[:, :, :hd] - _bdot(sol[:, :, hd:], s_old)
    o = _bdot(rep(qk3) * gam, u) + _bdot(q_v, s_old) * e_col
    s_scr[...] = s_old * jnp.exp(last) + _bdot(rep(k_t3), jnp.exp(last - a_col) * u)
    on = _rms(o) * nw_ref[...]
    for i in range(len(heads)):
        hs = slice(i * hd, (i + 1) * hd)
        o_ref[:, hs] = (on[i] * _silu(z_ref[:, hs])).astype(BF16)

    @pl.when(c == pl.num_programs(2) - 1)
    def _():
        st_ref[0] = s_scr[...]


def _dn_prompt(proj, ba, w, batch, seq):
    hk, hd = w["hk"], DN_HEAD
    hb = w["hb"]
    q = _pick(seq, (DN_CHUNK, 32, 16, 8))
    nc = seq // q
    nhb = hk // hb
    tok = lambda b, h, c: b * nc + c
    in_specs = [
        pl.BlockSpec((q, hb * hd), lambda b, h, c: (tok(b, h, c), h)),
        pl.BlockSpec((q, hb * hd), lambda b, h, c: (tok(b, h, c), nhb + h)),
        pl.BlockSpec((q, 2 * hb * hd), lambda b, h, c: (tok(b, h, c), nhb + h)),
        pl.BlockSpec((q, 2 * hb * hd), lambda b, h, c: (tok(b, h, c), 2 * nhb + h)),
        pl.BlockSpec((q, LANES), lambda b, h, c: (tok(b, h, c), 0)),
        pl.BlockSpec((CONV_W, hb * hd), lambda b, h, c: (0, h)),
        pl.BlockSpec((CONV_W, hb * hd), lambda b, h, c: (0, nhb + h)),
        pl.BlockSpec((CONV_W, 2 * hb * hd), lambda b, h, c: (0, nhb + h)),
        pl.BlockSpec((1, LANES), lambda b, h, c: (0, h)),
        pl.BlockSpec((1, LANES), lambda b, h, c: (0, h)),
        pl.BlockSpec((1, hd), lambda b, h, c: (0, 0)),
    ]
    o, st = pl.pallas_call(
        functools.partial(_dn_prompt_kernel, hb=hb, hv=2 * hk),
        grid=(batch, nhb, nc),
        in_specs=in_specs,
        out_specs=[pl.BlockSpec((q, 2 * hb * hd), lambda b, h, c: (tok(b, h, c), h)),
                   pl.BlockSpec((1, 2 * hb, hd, hd), lambda b, h, c: (b, h, 0, 0))],
        out_shape=[jax.ShapeDtypeStruct((batch * seq, 2 * hk * hd), BF16),
                   jax.ShapeDtypeStruct((batch, 2 * hk, hd, hd), F32)],
        scratch_shapes=[pltpu.VMEM((2 * hb, hd, hd), F32),
                        pltpu.VMEM((q + SUBLANES, 4 * hb * hd), F32)],
        compiler_params=_cparams(("parallel", "parallel", "arbitrary")),
        name="dn_prompt",
    )(proj, proj, proj, proj, ba, w["conv_w"], w["conv_w"], w["conv_w"], w["a_log"], w["dt_bias"], w["norm"])
    return o, st


def _rotate(x, cos, s1, s2):
    w = x.shape[1]
    return x * cos + pltpu.roll(x, w - 1, 1) * s1 + pltpu.roll(x, 1, 1) * s2


def _ret_prompt_kernel(q_ref, k_ref, v_ref, g_ref, cos_ref, s1_ref, s2_ref, lg_ref, y_ref, st_ref, s_scr, *, hb):
    c = pl.program_id(2)
    q = q_ref.shape[0]

    @pl.when(c == 0)
    def _():
        s_scr[...] = jnp.zeros_like(s_scr)

    tile = lambda t: t if hb == 1 else jnp.concatenate([t] * hb, axis=1)
    cos, s1, s2 = tile(cos_ref[...]), tile(s1_ref[...]), tile(s2_ref[...])
    qa = _rotate(q_ref[...], cos, s1, s2)
    ka = _rotate(k_ref[...], cos, s1, s2) * (RET_DK ** -0.5)
    q3 = jnp.stack([qa[:, RET_DK * h:RET_DK * (h + 1)] for h in range(hb)])
    k3 = jnp.stack([ka[:, RET_DK * h:RET_DK * (h + 1)] for h in range(hb)])
    v3 = jnp.stack([v_ref[:, RET_DV * h:RET_DV * (h + 1)] for h in range(hb)])
    lg = lg_ref[...][:, 0:1, 0:1]
    row = lax.broadcasted_iota(jnp.int32, (q, q), 0)
    col = lax.broadcasted_iota(jnp.int32, (q, q), 1)
    dist = (row - col).astype(F32)[None]
    decay = jnp.exp(jnp.where((row >= col)[None], dist * lg, NEG_BIG))
    pos = lax.broadcasted_iota(jnp.int32, (1, q, 1), 1).astype(F32)
    scores = _bdot_nt(q3, k3)
    s_old = s_scr[...]
    y = _bdot(scores * decay, v3) + _bdot(q3, s_old) * jnp.exp((pos + 1.0) * lg)
    wv = jnp.exp((float(q - 1) - pos) * lg) * v3
    k_t = jnp.stack([jnp.concatenate([_tr(k3[h][:, LANES * i:LANES * (i + 1)]) for i in range(RET_DK // LANES)],
                                     axis=0) for h in range(hb)])
    s_new = s_old * jnp.exp(float(q) * lg) + _bdot(k_t, wv)
    s_scr[...] = s_new
    yn = _rms(y)
    for h in range(hb):
        hs = slice(RET_DV * h, RET_DV * (h + 1))
        y_ref[:, hs] = (yn[h] * _silu(g_ref[:, hs])).astype(BF16)

    @pl.when(c == pl.num_programs(2) - 1)
    def _():
        st_ref[0] = s_new


def _ret_prompt(proj, rope, lg, nheads, batch, seq):
    q = _pick(seq, (RET_CHUNK, 64, 32, 16, 8))
    hb = _pick(nheads, (RET_HEADS_PER_STEP, 2, 1))
    nc = seq // q
    nhb = nheads // hb
    tok = lambda b, h, c: b * nc + c
    v0 = 2 * nheads * RET_DK // (hb * RET_DV)
    g0 = v0 + nhb
    cos, s1, s2 = rope
    in_specs = [
        pl.BlockSpec((q, hb * RET_DK), lambda b, h, c: (tok(b, h, c), h)),
        pl.BlockSpec((q, hb * RET_DK), lambda b, h, c: (tok(b, h, c), nhb + h)),
        pl.BlockSpec((q, hb * RET_DV), lambda b, h, c: (tok(b, h, c), v0 + h)),
        pl.BlockSpec((q, hb * RET_DV), lambda b, h, c: (tok(b, h, c), g0 + h)),
        pl.BlockSpec((q, RET_DK), lambda b, h, c: (c, 0)),
        pl.BlockSpec((q, RET_DK), lambda b, h, c: (c, 0)),
        pl.BlockSpec((q, RET_DK), lambda b, h, c: (c, 0)),
        pl.BlockSpec((hb, SUBLANES, LANES), lambda b, h, c: (h, 0, 0)),
    ]
    y, st = pl.pallas_call(
        functools.partial(_ret_prompt_kernel, hb=hb),
        grid=(batch, nhb, nc),
        in_specs=in_specs,
        out_specs=[pl.BlockSpec((q, hb * RET_DV), lambda b, h, c: (tok(b, h, c), h)),
                   pl.BlockSpec((1, hb, RET_DK, RET_DV), lambda b, h, c: (b, h, 0, 0))],
        out_shape=[jax.ShapeDtypeStruct((batch * seq, nheads * RET_DV), BF16),
                   jax.ShapeDtypeStruct((batch, nheads, RET_DK, RET_DV), F32)],
        scratch_shapes=[pltpu.VMEM((hb, RET_DK, RET_DV), F32)],
        compiler_params=_cparams(("parallel", "parallel", "arbitrary")),
        name="ret_prompt",
    )(proj, proj, proj, proj, cos, s1, s2, lg)
    return y, st


def _alias_args(prev, n_in, out_idx):
    if prev is None:
        return [], [], {}
    return [pl.BlockSpec(memory_space=pl.ANY)], [prev], {n_in: out_idx}


def _conv_step_kernel(*refs, has_bias, aliased):
    refs = list(refs)
    x_ref, cs_ref, w_ref = refs[:3]
    b_ref = refs[3] if has_bias else None
    o_ref, cn_ref = refs[3 + int(has_bias) + int(aliased):]
    w = w_ref[...]
    x = x_ref[...]
    c0, c1, c2 = cs_ref[0, 0], cs_ref[0, 1], cs_ref[0, 2]
    acc = c0 * w[0:1, :] + c1 * w[1:2, :] + c2 * w[2:3, :] + x * w[3:4, :]
    if has_bias:
        acc = acc + b_ref[...]
    o_ref[...] = _silu(acc)
    cn_ref[0, 0] = c1
    cn_ref[0, 1] = c2
    cn_ref[0, 2] = x


def _conv_step(proj, col0, cstate, layer, conv_w, conv_b, prev):
    m = proj.shape[0]
    cdim = cstate.shape[3]
    cb = _pick(cdim, (512, 256, 128))
    x0 = col0 // cb
    taps = CONV_W - 1
    in_specs = [pl.BlockSpec((m, cb), lambda j: (0, x0 + j)),
                pl.BlockSpec((1, taps, m, cb), lambda j: (layer, 0, 0, j)),
                pl.BlockSpec((CONV_W, cb), lambda j: (0, j))]
    args = [proj, cstate, conv_w]
    if conv_b is not None:
        in_specs.append(pl.BlockSpec((1, cb), lambda j: (0, j)))
        args.append(conv_b)
    a_specs, a_args, aliases = _alias_args(prev, len(args), 1)
    return pl.pallas_call(
        functools.partial(_conv_step_kernel, has_bias=conv_b is not None, aliased=prev is not None),
        grid=(cdim // cb,),
        in_specs=in_specs + a_specs,
        out_specs=[pl.BlockSpec((m, cb), lambda j: (0, j)),
                   pl.BlockSpec((1, taps, m, cb), lambda j: (layer, 0, 0, j))],
        out_shape=[jax.ShapeDtypeStruct((m, cdim), F32),
                   jax.ShapeDtypeStruct(cstate.shape, F32)],
        input_output_aliases=aliases,
        compiler_params=_cparams(("parallel",)),
        name="conv_step",
    )(*args, *a_args)


def _ssd_step_kernel(*refs, nheads):
    z_ref, xs_ref, b_ref, c_ref, dt_ref, dtb_ref, alog_ref, d_ref, nw_ref, st_ref = refs[:10]
    y_ref, so_ref, ytb = refs[-3:]
    nb = xs_ref.shape[0]
    hd = SSM_HEAD_DIM
    xs = xs_ref[...]
    dtv = _softplus(_group_lanes(dt_ref[...], pl.program_id(1), nheads) + dtb_ref[...])
    decay = jnp.exp(dtv * (-jnp.exp(alog_ref[...])))
    pad = jnp.zeros((LANES - nb, LANES), F32)
    kmat = jnp.concatenate([b_ref[...], pad], axis=0).astype(BF16)
    qmat = jnp.concatenate([c_ref[...], pad], axis=0).T.astype(BF16)
    v_all = xs * _expand64(dtv, nheads)
    v_t = [jnp.concatenate([v_all[:, LANES * j:LANES * (j + 1)], pad], axis=0).T
           for j in range(nheads // 2)]
    lane = lax.broadcasted_iota(jnp.int32, (hd, LANES), 1)
    for r in range(nheads):
        vt = v_t[r // 2][hd * (r % 2):hd * (r % 2 + 1), :]
        for bi in range(nb):
            outer = jnp.dot(jnp.where(lane == bi, vt, 0.0).astype(BF16), kmat, preferred_element_type=F32)
            so_ref[0, bi, r] = st_ref[0, bi, r] * decay[bi:bi + 1, r:r + 1] + outer
    for r in range(nheads):
        acc = jnp.zeros((hd, LANES), F32)
        for bi in range(nb):
            yb = jnp.dot(so_ref[0, bi, r].astype(BF16), qmat, preferred_element_type=F32)
            acc = jnp.where(lane == bi, yb, acc)
        ytb[hd * r:hd * (r + 1), :] = acc
    y = [ytb[LANES * j:LANES * (j + 1), :].T[:nb, :] for j in range(nheads // 2)]
    y = y[0] if len(y) == 1 else jnp.concatenate(y, axis=1)
    y = y + xs * _expand64(d_ref[...], nheads)
    y = y * _silu(z_ref[...])
    y_ref[...] = (_rms(y) * nw_ref[...]).astype(BF16)


def _ssd_step(proj, xbc, dt, w, state, layer, prev):
    m = proj.shape[0]
    d_inner, nh, g, n = w["d_inner"], w["hpg"], SSM_GROUPS, SSM_D_STATE
    width = nh * SSM_HEAD_DIM
    nb = _pick(m, (2 * STEP_BATCH, STEP_BATCH))
    b0 = d_inner // n
    st_spec = pl.BlockSpec((1, nb, nh, SSM_HEAD_DIM, n), lambda i, gi: (layer, i, gi, 0, 0))
    in_specs = [
        pl.BlockSpec((nb, width), lambda i, gi: (i, gi)),
        pl.BlockSpec((nb, width), lambda i, gi: (i, gi)),
        pl.BlockSpec((nb, n), lambda i, gi: (i, b0 + gi)),
        pl.BlockSpec((nb, n), lambda i, gi: (i, b0 + g + gi)),
        pl.BlockSpec((nb, LANES), lambda i, gi: (i, 0)),
        pl.BlockSpec((1, LANES), lambda i, gi: (0, gi)),
        pl.BlockSpec((1, LANES), lambda i, gi: (0, gi)),
        pl.BlockSpec((1, LANES), lambda i, gi: (0, gi)),
        pl.BlockSpec((1, width), lambda i, gi: (0, gi)),
        st_spec,
    ]
    args = [proj, xbc, xbc, xbc, dt, w["dt_bias"], w["a_log"], w["d"], w["norm"], state]
    a_specs, a_args, aliases = _alias_args(prev, len(args), 1)
    y, st = pl.pallas_call(
        functools.partial(_ssd_step_kernel, nheads=nh),
        grid=(m // nb, g),
        in_specs=in_specs + a_specs,
        out_specs=[pl.BlockSpec((nb, width), lambda i, gi: (i, gi)), st_spec],
        out_shape=[jax.ShapeDtypeStruct((m, d_inner), BF16),
                   jax.ShapeDtypeStruct(state.shape, F32)],
        scratch_shapes=[pltpu.VMEM((width, LANES), F32)],
        input_output_aliases=aliases,
        compiler_params=_cparams(("parallel", "parallel")),
        name="ssd_step",
    )(*args, *a_args)
    return y, st


def _dn_step_kernel(q_ref, k_ref, v_ref, z_ref, ba_ref, alog_ref, dtb_ref, nw_ref, st_ref,
                    o_ref, so_ref, obuf, *, hv):
    nb = q_ref.shape[0]
    hd = DN_HEAD
    qq = q_ref[...]
    kk = k_ref[...]
    qq = qq * lax.rsqrt(jnp.sum(qq * qq, axis=-1, keepdims=True) + NORM_EPS) * (hd ** -0.5)
    kk = kk * lax.rsqrt(jnp.sum(kk * kk, axis=-1, keepdims=True) + NORM_EPS)
    vv = v_ref[...]
    ba = _group_lanes(ba_ref[...], pl.program_id(1), 2)
    beta = _sigmoid(ba)
    eg = jnp.exp(-jnp.exp(alog_ref[...]) * _softplus(ba + dtb_ref[...]))
    k_t = _tr(kk)
    q_t = _tr(qq)
    for bi in range(nb):
        kcol = k_t[:, bi:bi + 1]
        qcol = q_t[:, bi:bi + 1]
        for j in range(2):
            s_old = st_ref[0, bi, j]
            b = beta[bi:bi + 1, j:j + 1]
            e = eg[bi:bi + 1, hv + j:hv + j + 1]
            ks = jnp.sum(kcol * s_old, axis=0, keepdims=True)
            u = b * vv[bi:bi + 1, hd * j:hd * (j + 1)] - (b * e) * ks
            s_new = s_old * e + kcol * u
            so_ref[0, bi, j] = s_new
            obuf[bi:bi + 1, hd * j:hd * (j + 1)] = jnp.sum(qcol * s_new, axis=0, keepdims=True)
    for j in range(2):
        o = obuf[:, hd * j:hd * (j + 1)]
        on = _rms(o) * nw_ref[...]
        o_ref[:, hd * j:hd * (j + 1)] = (on * _silu(z_ref[:, hd * j:hd * (j + 1)])).astype(BF16)


def _dn_step(proj, qkv, ba, w, state, layer):
    m = proj.shape[0]
    hk, hd = w["hk"], DN_HEAD
    nb = _pick(m, (2 * STEP_BATCH, STEP_BATCH))
    st_spec = pl.BlockSpec((1, nb, 2, hd, hd), lambda i, h: (layer, i, h, 0, 0))
    in_specs = [
        pl.BlockSpec((nb, hd), lambda i, h: (i, h)),
        pl.BlockSpec((nb, hd), lambda i, h: (i, hk + h)),
        pl.BlockSpec((nb, 2 * hd), lambda i, h: (i, hk + h)),
        pl.BlockSpec((nb, 2 * hd), lambda i, h: (i, 2 * hk + h)),
        pl.BlockSpec((nb, LANES), lambda i, h: (i, 0)),
        pl.BlockSpec((1, LANES), lambda i, h: (0, h)),
        pl.BlockSpec((1, LANES), lambda i, h: (0, h)),
        pl.BlockSpec((1, hd), lambda i, h: (0, 0)),
        st_spec,
    ]
    o, st = pl.pallas_call(
        functools.partial(_dn_step_kernel, hv=2 * hk),
        grid=(m // nb, hk),
        in_specs=in_specs,
        out_specs=[pl.BlockSpec((nb, 2 * hd), lambda i, h: (i, h)), st_spec],
        out_shape=[jax.ShapeDtypeStruct((m, 2 * hk * hd), BF16),
                   jax.ShapeDtypeStruct(state.shape, F32)],
        scratch_shapes=[pltpu.VMEM((nb, 2 * hd), F32)],
        compiler_params=_cparams(("parallel", "parallel")),
        name="dn_step",
    )(qkv, qkv, qkv, proj, ba, w["a_log_step"], w["dt_bias_step"], w["norm"], state)
    return o, st


def _ret_step_kernel(q_ref, k_ref, v_ref, g_ref, cos_ref, s1_ref, s2_ref, lg_ref, st_ref, y_ref, so_ref, ybuf):
    nb = q_ref.shape[0]
    cos, s1, s2 = cos_ref[...], s1_ref[...], s2_ref[...]
    qq = _rotate(q_ref[...], cos, s1, s2)
    kk = _rotate(k_ref[...], cos, s1, s2) * (RET_DK ** -0.5)
    vv = v_ref[...]
    gamma = jnp.exp(lg_ref[0][0:1, 0:1])
    nk = RET_DK // LANES
    k_t = jnp.concatenate([_tr(kk[:, LANES * i:LANES * (i + 1)]) for i in range(nk)], axis=0)
    q_t = jnp.concatenate([_tr(qq[:, LANES * i:LANES * (i + 1)]) for i in range(nk)], axis=0)
    for bi in range(nb):
        s_new = st_ref[0, bi, 0] * gamma + k_t[:, bi:bi + 1] * vv[bi:bi + 1, :]
        so_ref[0, bi, 0] = s_new
        ybuf[bi:bi + 1, :] = jnp.sum(q_t[:, bi:bi + 1] * s_new, axis=0, keepdims=True)
    y_ref[...] = (_rms(ybuf[...]) * _silu(g_ref[...])).astype(BF16)


def _ret_step(proj, rope, lg, nheads, state, layer):
    m = proj.shape[0]
    nb = _pick(m, (STEP_BATCH,))
    v0 = 2 * nheads * RET_DK // RET_DV
    g0 = v0 + nheads
    cos, s1, s2 = rope
    st_spec = pl.BlockSpec((1, nb, 1, RET_DK, RET_DV), lambda i, h: (layer, i, h, 0, 0))
    in_specs = [
        pl.BlockSpec((nb, RET_DK), lambda i, h: (i, h)),
        pl.BlockSpec((nb, RET_DK), lambda i, h: (i, nheads + h)),
        pl.BlockSpec((nb, RET_DV), lambda i, h: (i, v0 + h)),
        pl.BlockSpec((nb, RET_DV), lambda i, h: (i, g0 + h)),
        pl.BlockSpec((1, RET_DK), lambda i, h: (0, 0)),
        pl.BlockSpec((1, RET_DK), lambda i, h: (0, 0)),
        pl.BlockSpec((1, RET_DK), lambda i, h: (0, 0)),
        pl.BlockSpec((1, SUBLANES, LANES), lambda i, h: (h, 0, 0)),
        st_spec,
    ]
    y, st = pl.pallas_call(
        _ret_step_kernel,
        grid=(m // nb, nheads),
        in_specs=in_specs,
        out_specs=[pl.BlockSpec((nb, RET_DV), lambda i, h: (i, h)), st_spec],
        out_shape=[jax.ShapeDtypeStruct((m, nheads * RET_DV), BF16),
                   jax.ShapeDtypeStruct(state.shape, F32)],
        scratch_shapes=[pltpu.VMEM((nb, RET_DV), F32)],
        compiler_params=_cparams(("parallel", "parallel")),
        name="ret_step",
    )(proj, proj, proj, proj, cos, s1, s2, lg, state)
    return y, st


def _pad_lanes(x):
    return jnp.pad(x, [(0, 0)] * (x.ndim - 1) + [(0, LANES - x.shape[-1])])


def _prep_ssm(w_in, conv_w, conv_b, dt_bias, a_log, d_skip, norm_w):
    heads = dt_bias.shape[0]
    g = SSM_GROUPS
    hpg = heads // g
    d_inner = heads * SSM_HEAD_DIM
    conv_dim = conv_w.shape[1]
    main = d_inner + conv_dim
    per_group = lambda v: _pad_lanes(v.reshape(g, hpg)).reshape(1, g * LANES)
    w_dt = _pad_lanes(w_in[:, main:]).astype(BF16)
    return dict(d_inner=d_inner, hpg=hpg, conv_dim=conv_dim, main=main, w_dt=w_dt,
                conv_w=conv_w, conv_b=conv_b.reshape(1, conv_dim),
                dt_bias=per_group(dt_bias), a_log=per_group(a_log), d=per_group(d_skip),
                norm=norm_w.reshape(1, d_inner))


def _prep_dn(w_in, conv_w, a_log, dt_bias, norm_w):
    hv = a_log.shape[0]
    hk = hv // 2
    hb = _pick(hk, (DN_HEADS_PER_STEP, 2, 1))
    conv_dim = conv_w.shape[1]
    main = conv_dim + hv * DN_HEAD
    w_ba = _pad_lanes(w_in[:, main:main + 2 * hv]).astype(BF16)

    def table(v, heads_per_step):
        t = v.reshape(hk // heads_per_step, 2 * heads_per_step)
        t = jnp.pad(t, ((0, 0), (hv, LANES - hv - 2 * heads_per_step)))
        return t.reshape(1, -1)

    return dict(hk=hk, hb=hb, conv_dim=conv_dim, main=main, w_ba=w_ba, conv_w=conv_w,
                a_log=table(a_log, hb), dt_bias=table(dt_bias, hb),
                a_log_step=table(a_log, 1), dt_bias_step=table(dt_bias, 1),
                norm=norm_w.reshape(1, DN_HEAD))


def _rope_tables(pos):
    half = RET_DK // 2
    inv = 1.0 / (RET_ROPE_BASE ** jnp.linspace(0.0, 1.0, half, dtype=F32))
    ang = pos.astype(F32)[:, None] * inv[None, :]
    cos, sin, zero = jnp.cos(ang), jnp.sin(ang), jnp.zeros_like(ang)
    inter = lambda a, b: jnp.stack([a, b], axis=-1).reshape(pos.shape[0], RET_DK)
    return inter(cos, cos), inter(-sin, zero), inter(zero, sin)


def _forward(x, p, prompt, states, prm, wts):
    batch, seq, d_model = x.shape
    m = batch * seq
    h = x.reshape(m, d_model)
    p = p.reshape(p.shape[0], m, p.shape[-1])
    depth = prm["norm_mix_pre"].shape[0]
    ssm_s, ssm_c, dn_s, dn_c, ret_s = states
    o_ssm = o_ssm_c = o_dn = o_dn_c = o_ret = None
    p_ssm, p_ssm_c, p_dn, p_dn_c, p_ret = [], [], [], [], []
    tail = slice(seq - (CONV_W - 1), seq)
    xn = _norm(h, prm["norm_mix_pre"][0])
    for i in range(depth):
        kind, j = i % 3, i // 3
        if kind == 0:
            w = wts["ssm"][j]
            d_inner, conv_dim = w["d_inner"], w["conv_dim"]
            proj = _mm(xn, wts["ssm_in_t"], j, w["main"], True)
            dt = _mm_small(xn, w["w_dt"])
            if prompt:
                y, st = _ssd_prompt(proj, dt, w, batch, seq)
                p_ssm.append(st)
                p_ssm_c.append(proj.reshape(batch, seq, -1)[:, tail, d_inner:d_inner + conv_dim])
            else:
                xbc, o_ssm_c = _conv_step(proj, d_inner, ssm_c, j, w["conv_w"], w["conv_b"], o_ssm_c)
                y, o_ssm = _ssd_step(proj, xbc, dt, w, ssm_s, j, o_ssm)
            w_out = wts["ssm_out"]
        elif kind == 1:
            w = wts["dn"][j]
            conv_dim = w["conv_dim"]
            proj = _mm(xn, wts["dn_in_t"], j, w["main"], True)
            ba = _mm_small(xn, w["w_ba"])
            if prompt:
                y, st = _dn_prompt(proj, ba, w, batch, seq)
                p_dn.append(st)
                p_dn_c.append(proj.reshape(batch, seq, -1)[:, tail, :conv_dim])
            else:
                qkv, o_dn_c = _conv_step(proj, 0, dn_c, j, w["conv_w"], None, o_dn_c)
                y, o_dn = _dn_step(proj, qkv, ba, w, dn_s, j)
            w_out = wts["dn_out"]
        else:
            nheads = wts["ret_heads"]
            proj = _mm(xn, wts["ret_in"], j, wts["ret_in"].shape[2], False)
            if prompt:
                y, st = _ret_prompt(proj, wts["rope"], wts["lg"], nheads, batch, seq)
                p_ret.append(st)
            else:
                y, o_ret = _ret_step(proj, wts["rope"], wts["lg"], nheads, ret_s, j)
            w_out = wts["ret_out"]
        h, xn = _mm_out(y, w_out, j, h, prm["norm_mix_post"][i], prm["norm_ffn_pre"][i])
        act = _ffn_in(xn, wts["ffn_gate"], wts["ffn_up"], i)
        h, xn = _mm_out(act, wts["ffn_down"], i, h, prm["norm_ffn_post"][i], prm["norm_ple"][i])
        gain_next = prm["norm_mix_pre"][i + 1] if i + 1 < depth else None
        h, xn = _ple(h, xn, wts["ple_gate"], p, wts["ple_proj"], i, gain_next)
    h = h.reshape(batch, seq, d_model)
    if prompt:
        return (h, jnp.swapaxes(jnp.stack(p_ssm), -1, -2), jnp.stack(p_ssm_c), jnp.stack(p_dn),
                jnp.stack(p_dn_c), jnp.stack(p_ret))
    return (h, jnp.swapaxes(o_ssm, -1, -2), jnp.swapaxes(o_ssm_c, 1, 2), o_dn, jnp.swapaxes(o_dn_c, 1, 2), o_ret)


def kernel(x_prompt, x_sample, state_ssm, state_ssm_conv, state_delta, state_delta_conv, state_ret, p_prompt, p_sample, norm_mix_pre, norm_mix_post, norm_ffn_pre, norm_ffn_post, norm_ple, ffn_w_gate, ffn_w_up, ffn_w_down, ple_w_proj, ple_w_gate, ssm_w_in, ssm_conv_w, ssm_conv_b, ssm_dt_bias, ssm_a_log, ssm_d, ssm_norm, ssm_w_out, dn_w_in, dn_conv_w, dn_a_log, dn_dt_bias, dn_norm, dn_w_out, ret_w_in, ret_w_out):
    assert x_sample.shape[1] == 1, "the sample group advances one token per sequence"
    prm = dict(norm_mix_pre=norm_mix_pre, norm_mix_post=norm_mix_post, norm_ffn_pre=norm_ffn_pre,
               norm_ffn_post=norm_ffn_post, norm_ple=norm_ple)
    ret_heads = ret_w_out.shape[1] // RET_DV
    log_gamma = jnp.log(1.0 - 2.0 ** (-5.0 - jnp.arange(ret_heads, dtype=F32)))
    base = dict(
        ssm=[_prep_ssm(ssm_w_in[j], ssm_conv_w[j], ssm_conv_b[j], ssm_dt_bias[j], ssm_a_log[j], ssm_d[j],
                       ssm_norm[j]) for j in range(ssm_w_in.shape[0])],
        dn=[_prep_dn(dn_w_in[j], dn_conv_w[j], dn_a_log[j], dn_dt_bias[j], dn_norm[j])
            for j in range(dn_w_in.shape[0])],
        ssm_in_t=jnp.swapaxes(ssm_w_in, 1, 2), dn_in_t=jnp.swapaxes(dn_w_in, 1, 2),
        ssm_out=ssm_w_out.astype(BF16), dn_out=dn_w_out.astype(BF16),
        ret_in=ret_w_in, ret_out=ret_w_out.astype(BF16), ret_heads=ret_heads,
        ffn_gate=ffn_w_gate, ffn_up=ffn_w_up, ffn_down=ffn_w_down.astype(BF16),
        ple_gate=ple_w_gate.astype(BF16), ple_proj=ple_w_proj.astype(BF16),
        lg=jnp.broadcast_to(log_gamma[:, None, None], (ret_heads, SUBLANES, LANES)),
    )
    seq = x_prompt.shape[1]
    pos_prompt = jnp.arange(seq, dtype=jnp.int32)
    pos_sample = PAST_LEN + jnp.arange(1, dtype=jnp.int32)
    out_p = _forward(x_prompt, p_prompt, True, (None,) * 5, prm, dict(base, rope=_rope_tables(pos_prompt)))
    states = (jnp.swapaxes(state_ssm, -1, -2), jnp.swapaxes(state_ssm_conv, 1, 2), state_delta,
              jnp.swapaxes(state_delta_conv, 1, 2), state_ret)
    out_s = _forward(x_sample, p_sample, False, states, prm, dict(base, rope=_rope_tables(pos_sample)))
    return (out_p[0], out_s[0]) + out_p[1:] + out_s[1:]
```

```python
import functools

import jax
import jax.numpy as jnp
from jax import lax
from jax.experimental import pallas as pl
from jax.experimental.pallas import tpu as pltpu

F32 = jnp.float32
BF16 = jnp.bfloat16

NORM_EPS = 1e-6
PAST_LEN = 16384
CONV_W = 4
SSM_GROUPS = 8
SSM_HEAD_DIM = 64
SSM_D_STATE = 128
DN_HEAD = 128
RET_DK = 256
RET_DV = 512
RET_ROPE_BASE = 10000.0

LANES = 128
SUBLANES = 8
NEG_BIG = -1e30
SSD_CHUNK = 128
RET_CHUNK = 128
DN_CHUNK = 64
DN_HEADS_PER_STEP = 8
RET_HEADS_PER_STEP = 4
STEP_BATCH = 8
VMEM_LIMIT = 52 * 1024 * 1024


def _pick(n, prefs):
    for p in prefs:
        if n % p == 0:
            return p
    return n


def _cparams(sem):
    return pltpu.CompilerParams(dimension_semantics=sem, vmem_limit_bytes=VMEM_LIMIT)


def _sigmoid(x):
    return 1.0 / (1.0 + jnp.exp(-x))


def _silu(x):
    h = 0.5 * x
    return h + h * jnp.tanh(h)


def _softplus(x):
    return jnp.maximum(x, 0.0) + jnp.log1p(jnp.exp(-jnp.abs(x)))


def _dot(a, b):
    return jnp.dot(a.astype(BF16), b.astype(BF16), preferred_element_type=F32)


def _dot_nt(a, b):
    return lax.dot_general(a.astype(BF16), b.astype(BF16), (((1,), (1,)), ((), ())),
                           preferred_element_type=F32)


def _dot01(m01, x):
    hi = x.astype(BF16)
    r = x - hi.astype(F32)
    mid = r.astype(BF16)
    lo = (r - mid.astype(F32)).astype(BF16)
    out = jnp.dot(m01, hi, preferred_element_type=F32)
    out = out + jnp.dot(m01, mid, preferred_element_type=F32)
    return out + jnp.dot(m01, lo, preferred_element_type=F32)


def _tr(x):
    r, c = x.shape
    assert c == LANES and r <= LANES
    if r < LANES:
        x = jnp.concatenate([x, jnp.zeros((LANES - r, c), x.dtype)], axis=0)
    return x.T[:, :r]


def _expand64(x, nheads):
    rows = x.shape[0]
    lane = lax.broadcasted_iota(jnp.int32, (rows, LANES), 1)
    parts = []
    for j in range(nheads // 2):
        a = jnp.broadcast_to(x[:, 2 * j:2 * j + 1], (rows, LANES))
        b = jnp.broadcast_to(x[:, 2 * j + 1:2 * j + 2], (rows, LANES))
        parts.append(jnp.where(lane < 64, a, b))
    return parts[0] if len(parts) == 1 else jnp.concatenate(parts, axis=1)


def _rms(y):
    return y * lax.rsqrt(jnp.mean(y * y, axis=-1, keepdims=True) + NORM_EPS)


def _conv_chunk(buf_ref, col0, x, w, bias):
    q, width = x.shape
    cols = slice(col0, col0 + width)
    buf_ref[SUBLANES:SUBLANES + q, cols] = x
    acc = x * w[CONV_W - 1:CONV_W, :]
    for s in range(1, CONV_W):
        acc = acc + buf_ref[SUBLANES - s:SUBLANES - s + q, cols] * w[CONV_W - 1 - s:CONV_W - s, :]
    buf_ref[0:SUBLANES, cols] = x[q - SUBLANES:q, :]
    if bias is not None:
        acc = acc + bias
    return acc


def _bdot(a, b):
    return lax.dot_general(a.astype(BF16), b.astype(BF16), (((2,), (1,)), ((0,), (0,))),
                           preferred_element_type=F32)


def _bdot_nt(a, b):
    return lax.dot_general(a.astype(BF16), b.astype(BF16), (((2,), (2,)), ((0,), (0,))),
                           preferred_element_type=F32)


def _tri_inv(a, row, col):
    q = a.shape[-1]
    eye = (row == col).astype(F32)[None]
    d = eye - jnp.where((jnp.right_shift(row, 1) == jnp.right_shift(col, 1))[None], a, 0.0)
    sh = 1
    while (1 << sh) < q:
        same_big = jnp.right_shift(row, sh + 1) == jnp.right_shift(col, sh + 1)
        diff_small = jnp.right_shift(row, sh) != jnp.right_shift(col, sh)
        lb = jnp.where((same_big & diff_small)[None], a, 0.0)
        d = d - _bdot(d, _bdot(lb, d))
        sh += 1
    return d


def _norm_kernel(x_ref, g_ref, o_ref):
    o_ref[...] = (_rms(x_ref[...]) * g_ref[...]).astype(BF16)


def _norm(x, gain):
    m, k = x.shape
    tm = _pick(m, (512, 256, 128))
    return pl.pallas_call(
        _norm_kernel,
        grid=(m // tm,),
        in_specs=[pl.BlockSpec((tm, k), lambda i: (i, 0)),
                  pl.BlockSpec((1, k), lambda i: (0, 0))],
        out_specs=pl.BlockSpec((tm, k), lambda i: (i, 0)),
        out_shape=jax.ShapeDtypeStruct((m, k), BF16),
        compiler_params=_cparams(("parallel",)),
        name="norm",
    )(x, gain.reshape(1, k))


def _mm_kernel(x_ref, w_ref, o_ref, wb_ref, *, transposed):
    @pl.when(pl.program_id(1) == 0)
    def _():
        wb_ref[...] = w_ref[0].astype(BF16)

    if transposed:
        o_ref[...] = lax.dot_general(x_ref[...], wb_ref[...], (((1,), (1,)), ((), ())),
                                     preferred_element_type=F32)
    else:
        o_ref[...] = jnp.dot(x_ref[...], wb_ref[...], preferred_element_type=F32)


def _mm(xn, w, layer, n, transposed):
    m, k = xn.shape
    tm = _pick(m, (1024, 512, 256, 128))
    tn = _pick(n, (1024, 512, 256, 128))
    if transposed:
        w_spec = pl.BlockSpec((1, tn, k), lambda j, i: (layer, j, 0))
        w_scratch = pltpu.VMEM((tn, k), BF16)
    else:
        w_spec = pl.BlockSpec((1, k, tn), lambda j, i: (layer, 0, j))
        w_scratch = pltpu.VMEM((k, tn), BF16)
    return pl.pallas_call(
        functools.partial(_mm_kernel, transposed=transposed),
        grid=(n // tn, m // tm),
        in_specs=[pl.BlockSpec((tm, k), lambda j, i: (i, 0)), w_spec],
        out_specs=pl.BlockSpec((tm, tn), lambda j, i: (i, j)),
        out_shape=jax.ShapeDtypeStruct((m, n), F32),
        scratch_shapes=[w_scratch],
        compiler_params=_cparams(("parallel", "arbitrary")),
        name="mm",
    )(xn, w)


def _mm_small_kernel(x_ref, w_ref, o_ref, *, nblk, width):
    w = w_ref[0]
    rows, k = w.shape
    if rows < LANES:
        w = jnp.concatenate([w, jnp.zeros((LANES - rows, k), w.dtype)], axis=0)
    y = _dot_nt(x_ref[...], w)
    for b in range(nblk):
        shift = (LANES - b * width) % LANES
        o_ref[:, LANES * b:LANES * (b + 1)] = y if shift == 0 else pltpu.roll(y, shift, 1)


def _mm_small(xn, w_t, layer, row0, rows, nblk, width):
    m, k = xn.shape
    tm = _pick(m, (1024, 512, 256, 128))
    assert rows <= LANES and row0 % rows == 0 and rows % SUBLANES == 0
    return pl.pallas_call(
        functools.partial(_mm_small_kernel, nblk=nblk, width=width),
        grid=(m // tm,),
        in_specs=[pl.BlockSpec((tm, k), lambda i: (i, 0)),
                  pl.BlockSpec((1, rows, k), lambda i: (layer, row0 // rows, 0))],
        out_specs=pl.BlockSpec((tm, nblk * LANES), lambda i: (i, 0)),
        out_shape=jax.ShapeDtypeStruct((m, nblk * LANES), F32),
        compiler_params=_cparams(("parallel",)),
        name="mm_small",
    )(xn, w_t)


def _mm_out_kernel(a_ref, w_ref, h_ref, g_ref, gn_ref, o_ref, xn_ref):
    y = jnp.dot(a_ref[...], w_ref[0], preferred_element_type=F32)
    h = h_ref[...] + _rms(y) * g_ref[...]
    o_ref[...] = h
    xn_ref[...] = (_rms(h) * gn_ref[...]).astype(BF16)


def _mm_out(a, w, layer, h, gain, gain_next):
    m, k = a.shape
    n = w.shape[2]
    tm = _pick(m, (256, 128))
    return pl.pallas_call(
        _mm_out_kernel,
        grid=(m // tm,),
        in_specs=[pl.BlockSpec((tm, k), lambda i: (i, 0)),
                  pl.BlockSpec((1, k, n), lambda i: (layer, 0, 0), pipeline_mode=pl.Buffered(1)),
                  pl.BlockSpec((tm, n), lambda i: (i, 0)),
                  pl.BlockSpec((1, n), lambda i: (0, 0)),
                  pl.BlockSpec((1, n), lambda i: (0, 0))],
        out_specs=[pl.BlockSpec((tm, n), lambda i: (i, 0)),
                   pl.BlockSpec((tm, n), lambda i: (i, 0))],
        out_shape=[jax.ShapeDtypeStruct((m, n), F32),
                   jax.ShapeDtypeStruct((m, n), BF16)],
        compiler_params=_cparams(("parallel",)),
        name="mm_out",
    )(a, w, h, gain.reshape(1, n), gain_next.reshape(1, n))


def _ffn_in_kernel(x_ref, wg_ref, wu_ref, o_ref, wgb_ref, wub_ref):
    @pl.when(pl.program_id(1) == 0)
    def _():
        wgb_ref[...] = wg_ref[0].astype(BF16)
        wub_ref[...] = wu_ref[0].astype(BF16)

    xn = x_ref[...]
    gate = jnp.dot(xn, wgb_ref[...], preferred_element_type=F32)
    up = jnp.dot(xn, wub_ref[...], preferred_element_type=F32)
    o_ref[...] = (_silu(gate) * up).astype(BF16)


def _ffn_in(xn, wg, wu, layer):
    m, k = xn.shape
    n = wg.shape[2]
    tm = _pick(m, (1024, 512, 256, 128))
    tn = _pick(n, (512, 256, 128))
    return pl.pallas_call(
        _ffn_in_kernel,
        grid=(n // tn, m // tm),
        in_specs=[pl.BlockSpec((tm, k), lambda j, i: (i, 0)),
                  pl.BlockSpec((1, k, tn), lambda j, i: (layer, 0, j)),
                  pl.BlockSpec((1, k, tn), lambda j, i: (layer, 0, j))],
        out_specs=pl.BlockSpec((tm, tn), lambda j, i: (i, j)),
        out_shape=jax.ShapeDtypeStruct((m, n), BF16),
        scratch_shapes=[pltpu.VMEM((k, tn), BF16), pltpu.VMEM((k, tn), BF16)],
        compiler_params=_cparams(("parallel", "arbitrary")),
        name="ffn_in",
    )(xn, wg, wu)


def _ple_kernel(*refs, has_next):
    h_ref, xn_ref, wg_ref, p_ref, wp_ref = refs[:5]
    gate = _sigmoid(jnp.dot(xn_ref[...], wg_ref[0], preferred_element_type=F32))
    proj = jnp.dot(p_ref[0].astype(BF16), wp_ref[0], preferred_element_type=F32)
    h = h_ref[...] + proj * gate
    if has_next:
        gn_ref, o_ref, on_ref = refs[5:]
        on_ref[...] = (_rms(h) * gn_ref[...]).astype(BF16)
    else:
        o_ref, = refs[5:]
    o_ref[...] = h


def _ple(h, xn, wg, p, wp, layer, gain_next):
    m, k = h.shape
    n = wg.shape[2]
    pd = p.shape[2]
    tm = _pick(m, (512, 256, 128))
    has_next = gain_next is not None
    row = lambda i: (i, 0)
    in_specs = [pl.BlockSpec((tm, k), row),
                pl.BlockSpec((tm, k), row),
                pl.BlockSpec((1, k, n), lambda i: (layer, 0, 0), pipeline_mode=pl.Buffered(1)),
                pl.BlockSpec((1, tm, pd), lambda i: (layer, i, 0)),
                pl.BlockSpec((1, pd, n), lambda i: (layer, 0, 0), pipeline_mode=pl.Buffered(1))]
    args = [h, xn, wg, p, wp]
    out_specs = [pl.BlockSpec((tm, n), row)]
    out_shape = [jax.ShapeDtypeStruct((m, n), F32)]
    if has_next:
        in_specs.append(pl.BlockSpec((1, n), lambda i: (0, 0)))
        args.append(gain_next.reshape(1, n))
        out_specs.append(pl.BlockSpec((tm, n), row))
        out_shape.append(jax.ShapeDtypeStruct((m, n), BF16))
    out = pl.pallas_call(
        functools.partial(_ple_kernel, has_next=has_next),
        grid=(m // tm,),
        in_specs=in_specs,
        out_specs=out_specs,
        out_shape=out_shape,
        compiler_params=_cparams(("parallel",)),
        name="ple",
    )(*args)
    return (out[0], out[1]) if has_next else (out[0], None)


def _ssd_prompt_kernel(z_ref, xs_ref, b_ref, c_ref, dt_ref, wx_ref, wb_ref, wc_ref, bx_ref, bb_ref, bc_ref,
                       dtb_ref, alog_ref, d_ref, nw_ref, y_ref, st_ref, s_scr, cbuf, *, nheads):
    c = pl.program_id(2)
    q, width = xs_ref.shape
    n = b_ref.shape[1]
    hd = SSM_HEAD_DIM

    @pl.when(c == 0)
    def _():
        s_scr[...] = jnp.zeros_like(s_scr)
        cbuf[0:SUBLANES, :] = jnp.zeros((SUBLANES, cbuf.shape[1]), F32)

    xs = _silu(_conv_chunk(cbuf, 0, xs_ref[...], wx_ref[...], bx_ref[...]))
    bm = _silu(_conv_chunk(cbuf, width, b_ref[...], wb_ref[...], bb_ref[...]))
    cm = _silu(_conv_chunk(cbuf, width + n, c_ref[...], wc_ref[...], bc_ref[...]))

    dtv = _softplus(dt_ref[...] + dtb_ref[...])
    la = dtv * (-jnp.exp(alog_ref[...]))
    row = lax.broadcasted_iota(jnp.int32, (q, q), 0)
    col = lax.broadcasted_iota(jnp.int32, (q, q), 1)
    causal = row >= col
    acs = _dot01(causal.astype(BF16), la)
    acs_t = _tr(acs)
    scores = _dot_nt(cm, bm)
    dt_t = _tr(dtv)
    s_old = s_scr[...]
    y = _dot_nt(cm, s_old) * _expand64(jnp.exp(acs), nheads)
    lane = lax.broadcasted_iota(jnp.int32, (q, LANES), 1)
    parts = []
    for j in range(nheads // 2):
        xp = xs[:, LANES * j:LANES * (j + 1)].astype(BF16)
        ys = []
        for t in range(2):
            r = 2 * j + t
            seg = acs[:, r:r + 1] - acs_t[r:r + 1, :]
            decay = jnp.exp(jnp.where(causal, seg, NEG_BIG))
            ys.append(jnp.dot((scores * decay * dt_t[r:r + 1, :]).astype(BF16), xp, preferred_element_type=F32))
        parts.append(jnp.where(lane < 64, ys[0], ys[1]))
    y = y + (parts[0] if len(parts) == 1 else jnp.concatenate(parts, axis=1))
    y = y + xs * _expand64(d_ref[...], nheads)

    last = acs[q - 1:q, :]
    wsc_t = _tr(jnp.exp(last - acs) * dtv)
    sub = lax.broadcasted_iota(jnp.int32, (LANES, q), 0)
    wv_t = []
    for j in range(width // LANES):
        scale = jnp.where(sub < hd, wsc_t[2 * j:2 * j + 1, :], wsc_t[2 * j + 1:2 * j + 2, :])
        wv_t.append(_tr(xs[:, LANES * j:LANES * (j + 1)]) * scale)
    wv_t = wv_t[0] if len(wv_t) == 1 else jnp.concatenate(wv_t, axis=0)
    e_last = jnp.exp(last)
    e_rows = jnp.concatenate([jnp.broadcast_to(e_last[:, r:r + 1], (hd, n)) for r in range(nheads)], axis=0)
    s_new = s_old * e_rows + _dot(wv_t, bm)
    s_scr[...] = s_new

    y = y * _silu(z_ref[...])
    y_ref[...] = (_rms(y) * nw_ref[...]).astype(BF16)

    @pl.when(c == pl.num_programs(2) - 1)
    def _():
        for r in range(nheads):
            st_ref[0, r] = s_new[hd * r:hd * (r + 1), :]


def _ssd_prompt(proj, dt, w, batch, seq):
    d_inner, nh, g, n = w["d_inner"], w["hpg"], SSM_GROUPS, SSM_D_STATE
    width = nh * SSM_HEAD_DIM
    q = _pick(seq, (SSD_CHUNK, 64, 32, 16, 8))
    nc = seq // q
    xs0, b0, c0 = d_inner // width, 2 * d_inner // n, 2 * d_inner // n + g
    cw0 = d_inner // n
    tok = lambda b, gi, c: b * nc + c
    in_specs = [
        pl.BlockSpec((q, width), lambda b, gi, c: (tok(b, gi, c), gi)),
        pl.BlockSpec((q, width), lambda b, gi, c: (tok(b, gi, c), xs0 + gi)),
        pl.BlockSpec((q, n), lambda b, gi, c: (tok(b, gi, c), b0 + gi)),
        pl.BlockSpec((q, n), lambda b, gi, c: (tok(b, gi, c), c0 + gi)),
        pl.BlockSpec((q, LANES), lambda b, gi, c: (tok(b, gi, c), gi)),
        pl.BlockSpec((CONV_W, width), lambda b, gi, c: (0, gi)),
        pl.BlockSpec((CONV_W, n), lambda b, gi, c: (0, cw0 + gi)),
        pl.BlockSpec((CONV_W, n), lambda b, gi, c: (0, cw0 + g + gi)),
        pl.BlockSpec((1, width), lambda b, gi, c: (0, gi)),
        pl.BlockSpec((1, n), lambda b, gi, c: (0, cw0 + gi)),
        pl.BlockSpec((1, n), lambda b, gi, c: (0, cw0 + g + gi)),
        pl.BlockSpec((1, LANES), lambda b, gi, c: (0, gi)),
        pl.BlockSpec((1, LANES), lambda b, gi, c: (0, gi)),
        pl.BlockSpec((1, LANES), lambda b, gi, c: (0, gi)),
        pl.BlockSpec((1, width), lambda b, gi, c: (0, gi)),
    ]
    y, st = pl.pallas_call(
        functools.partial(_ssd_prompt_kernel, nheads=nh),
        grid=(batch, g, nc),
        in_specs=in_specs,
        out_specs=[pl.BlockSpec((q, width), lambda b, gi, c: (tok(b, gi, c), gi)),
                   pl.BlockSpec((1, nh, SSM_HEAD_DIM, n), lambda b, gi, c: (b, gi, 0, 0))],
        out_shape=[jax.ShapeDtypeStruct((batch * seq, d_inner), BF16),
                   jax.ShapeDtypeStruct((batch, g * nh, SSM_HEAD_DIM, n), F32)],
        scratch_shapes=[pltpu.VMEM((width, n), F32),
                        pltpu.VMEM((q + SUBLANES, width + 2 * n), F32)],
        compiler_params=_cparams(("parallel", "parallel", "arbitrary")),
        name="ssd_prompt",
    )(proj, proj, proj, proj, dt, w["conv_w"], w["conv_w"], w["conv_w"], w["conv_b"], w["conv_b"], w["conv_b"],
      w["dt_bias"], w["a_log"], w["d"], w["norm"])
    return y, st


def _dn_prompt_kernel(q_ref, k_ref, v_ref, z_ref, ba_ref, wq_ref, wk_ref, wv_ref, alog_ref, dtb_ref, nw_ref,
                      o_ref, st_ref, s_scr, cbuf, *, hb, hv):
    c = pl.program_id(2)
    q = q_ref.shape[0]
    hd = DN_HEAD

    @pl.when(c == 0)
    def _():
        s_scr[...] = jnp.zeros_like(s_scr)
        cbuf[0:SUBLANES, :] = jnp.zeros((SUBLANES, cbuf.shape[1]), F32)

    qa = _silu(_conv_chunk(cbuf, 0, q_ref[...], wq_ref[...], None))
    ka = _silu(_conv_chunk(cbuf, hb * hd, k_ref[...], wk_ref[...], None))
    va = _silu(_conv_chunk(cbuf, 2 * hb * hd, v_ref[...], wv_ref[...], None))

    ba = ba_ref[...]
    beta_all = _sigmoid(ba)
    gg = -jnp.exp(alog_ref[...]) * _softplus(ba + dtb_ref[...])
    row = lax.broadcasted_iota(jnp.int32, (q, q), 0)
    col = lax.broadcasted_iota(jnp.int32, (q, q), 1)
    incl = row >= col
    strict = row > col
    acs = _dot01(incl.astype(BF16), gg)
    acs_t = _tr(acs)

    heads = [(kh, j) for kh in range(hb) for j in range(2)]
    q3 = jnp.stack([qa[:, kh * hd:(kh + 1) * hd] for kh in range(hb)])
    k3 = jnp.stack([ka[:, kh * hd:(kh + 1) * hd] for kh in range(hb)])
    q3 = q3 * lax.rsqrt(jnp.sum(q3 * q3, axis=-1, keepdims=True) + NORM_EPS) * (hd ** -0.5)
    k3 = k3 * lax.rsqrt(jnp.sum(k3 * k3, axis=-1, keepdims=True) + NORM_EPS)
    kk3 = _bdot_nt(k3, k3)
    qk3 = _bdot_nt(q3, k3)
    k_t3 = jnp.stack([_tr(k3[kh]) for kh in range(hb)])

    nh = len(heads)
    a_col = jnp.stack([acs[:, hv + i:hv + i + 1] for i in range(nh)])
    a_row = jnp.stack([acs_t[hv + i:hv + i + 1, :] for i in range(nh)])
    last = jnp.stack([acs[q - 1:q, hv + i:hv + i + 1] for i in range(nh)])
    bcol = jnp.stack([beta_all[:, i:i + 1] for i in range(nh)])
    rep = lambda x: jnp.stack([x[kh] for kh, _ in heads])
    k_v, q_v = rep(k3), rep(q3)
    gam = jnp.exp(jnp.where(incl[None], a_col - a_row, NEG_BIG))
    a_mat = jnp.where(strict[None], rep(kk3) * gam * bcol, 0.0)
    t_mat = _tri_inv(a_mat, row, col)
    v3 = jnp.stack([va[:, i * hd:(i + 1) * hd] for i in range(len(heads))])
    e_col = jnp.exp(a_col)
    rhs = jnp.concatenate([bcol * v3, (bcol * e_col) * k_v], axis=2)
    sol = _bdot(t_mat, rhs)
    s_old = s_scr[...]
    u = sol[:, :, :hd] - _bdot(sol[:, :, hd:], s_old)
    o = _bdot(rep(qk3) * gam, u) + _bdot(q_v, s_old) * e_col
    s_scr[...] = s_old * jnp.exp(last) + _bdot(rep(k_t3), jnp.exp(last - a_col) * u)
    on = _rms(o) * nw_ref[...]
    for i in range(len(heads)):
        hs = slice(i * hd, (i + 1) * hd)
        o_ref[:, hs] = (on[i] * _silu(z_ref[:, hs])).astype(BF16)

    @pl.when(c == pl.num_programs(2) - 1)
    def _():
        st_ref[0] = s_scr[...]


def _dn_prompt(proj, ba, w, batch, seq):
    hk, hd = w["hk"], DN_HEAD
    hb = w["hb"]
    q = _pick(seq, (DN_CHUNK, 32, 16, 8))
    nc = seq // q
    nhb = hk // hb
    tok = lambda b, h, c: b * nc + c
    in_specs = [
        pl.BlockSpec((q, hb * hd), lambda b, h, c: (tok(b, h, c), h)),
        pl.BlockSpec((q, hb * hd), lambda b, h, c: (tok(b, h, c), nhb + h)),
        pl.BlockSpec((q, 2 * hb * hd), lambda b, h, c: (tok(b, h, c), nhb + h)),
        pl.BlockSpec((q, 2 * hb * hd), lambda b, h, c: (tok(b, h, c), 2 * nhb + h)),
        pl.BlockSpec((q, LANES), lambda b, h, c: (tok(b, h, c), h)),
        pl.BlockSpec((CONV_W, hb * hd), lambda b, h, c: (0, h)),
        pl.BlockSpec((CONV_W, hb * hd), lambda b, h, c: (0, nhb + h)),
        pl.BlockSpec((CONV_W, 2 * hb * hd), lambda b, h, c: (0, nhb + h)),
        pl.BlockSpec((1, LANES), lambda b, h, c: (0, h)),
        pl.BlockSpec((1, LANES), lambda b, h, c: (0, h)),
        pl.BlockSpec((1, hd), lambda b, h, c: (0, 0)),
    ]
    o, st = pl.pallas_call(
        functools.partial(_dn_prompt_kernel, hb=hb, hv=2 * hk),
        grid=(batch, nhb, nc),
        in_specs=in_specs,
        out_specs=[pl.BlockSpec((q, 2 * hb * hd), lambda b, h, c: (tok(b, h, c), h)),
                   pl.BlockSpec((1, 2 * hb, hd, hd), lambda b, h, c: (b, h, 0, 0))],
        out_shape=[jax.ShapeDtypeStruct((batch * seq, 2 * hk * hd), BF16),
                   jax.ShapeDtypeStruct((batch, 2 * hk, hd, hd), F32)],
        scratch_shapes=[pltpu.VMEM((2 * hb, hd, hd), F32),
                        pltpu.VMEM((q + SUBLANES, 4 * hb * hd), F32)],
        compiler_params=_cparams(("parallel", "parallel", "arbitrary")),
        name="dn_prompt",
    )(proj, proj, proj, proj, ba, w["conv_w"], w["conv_w"], w["conv_w"], w["a_log"], w["dt_bias"], w["norm"])
    return o, st


def _rotate(x, cos, s1, s2):
    w = x.shape[1]
    return x * cos + pltpu.roll(x, w - 1, 1) * s1 + pltpu.roll(x, 1, 1) * s2


def _ret_prompt_kernel(q_ref, k_ref, v_ref, g_ref, cos_ref, s1_ref, s2_ref, lg_ref, y_ref, st_ref, s_scr, *, hb):
    c = pl.program_id(2)
    q = q_ref.shape[0]

    @pl.when(c == 0)
    def _():
        s_scr[...] = jnp.zeros_like(s_scr)

    tile = lambda t: t if hb == 1 else jnp.concatenate([t] * hb, axis=1)
    cos, s1, s2 = tile(cos_ref[...]), tile(s1_ref[...]), tile(s2_ref[...])
    qa = _rotate(q_ref[...], cos, s1, s2)
    ka = _rotate(k_ref[...], cos, s1, s2) * (RET_DK ** -0.5)
    q3 = jnp.stack([qa[:, RET_DK * h:RET_DK * (h + 1)] for h in range(hb)])
    k3 = jnp.stack([ka[:, RET_DK * h:RET_DK * (h + 1)] for h in range(hb)])
    v3 = jnp.stack([v_ref[:, RET_DV * h:RET_DV * (h + 1)] for h in range(hb)])
    lg = lg_ref[...][:, 0:1, 0:1]
    row = lax.broadcasted_iota(jnp.int32, (q, q), 0)
    col = lax.broadcasted_iota(jnp.int32, (q, q), 1)
    dist = (row - col).astype(F32)[None]
    decay = jnp.exp(jnp.where((row >= col)[None], dist * lg, NEG_BIG))
    pos = lax.broadcasted_iota(jnp.int32, (1, q, 1), 1).astype(F32)
    scores = _bdot_nt(q3, k3)
    s_old = s_scr[...]
    y = _bdot(scores * decay, v3) + _bdot(q3, s_old) * jnp.exp((pos + 1.0) * lg)
    wv = jnp.exp((float(q - 1) - pos) * lg) * v3
    k_t = jnp.stack([jnp.concatenate([_tr(k3[h][:, LANES * i:LANES * (i + 1)]) for i in range(RET_DK // LANES)],
                                     axis=0) for h in range(hb)])
    s_new = s_old * jnp.exp(float(q) * lg) + _bdot(k_t, wv)
    s_scr[...] = s_new
    yn = _rms(y)
    for h in range(hb):
        hs = slice(RET_DV * h, RET_DV * (h + 1))
        y_ref[:, hs] = (yn[h] * _silu(g_ref[:, hs])).astype(BF16)

    @pl.when(c == pl.num_programs(2) - 1)
    def _():
        st_ref[0] = s_new


def _ret_prompt(proj, rope, lg, nheads, batch, seq):
    q = _pick(seq, (RET_CHUNK, 64, 32, 16, 8))
    hb = _pick(nheads, (RET_HEADS_PER_STEP, 2, 1))
    nc = seq // q
    nhb = nheads // hb
    tok = lambda b, h, c: b * nc + c
    v0 = 2 * nheads * RET_DK // (hb * RET_DV)
    g0 = v0 + nhb
    cos, s1, s2 = rope
    in_specs = [
        pl.BlockSpec((q, hb * RET_DK), lambda b, h, c: (tok(b, h, c), h)),
        pl.BlockSpec((q, hb * RET_DK), lambda b, h, c: (tok(b, h, c), nhb + h)),
        pl.BlockSpec((q, hb * RET_DV), lambda b, h, c: (tok(b, h, c), v0 + h)),
        pl.BlockSpec((q, hb * RET_DV), lambda b, h, c: (tok(b, h, c), g0 + h)),
        pl.BlockSpec((q, RET_DK), lambda b, h, c: (c, 0)),
        pl.BlockSpec((q, RET_DK), lambda b, h, c: (c, 0)),
        pl.BlockSpec((q, RET_DK), lambda b, h, c: (c, 0)),
        pl.BlockSpec((hb, SUBLANES, LANES), lambda b, h, c: (h, 0, 0)),
    ]
    y, st = pl.pallas_call(
        functools.partial(_ret_prompt_kernel, hb=hb),
        grid=(batch, nhb, nc),
        in_specs=in_specs,
        out_specs=[pl.BlockSpec((q, hb * RET_DV), lambda b, h, c: (tok(b, h, c), h)),
                   pl.BlockSpec((1, hb, RET_DK, RET_DV), lambda b, h, c: (b, h, 0, 0))],
        out_shape=[jax.ShapeDtypeStruct((batch * seq, nheads * RET_DV), BF16),
                   jax.ShapeDtypeStruct((batch, nheads, RET_DK, RET_DV), F32)],
        scratch_shapes=[pltpu.VMEM((hb, RET_DK, RET_DV), F32)],
        compiler_params=_cparams(("parallel", "parallel", "arbitrary")),
        name="ret_prompt",
    )(proj, proj, proj, proj, cos, s1, s2, lg)
    return y, st


def _alias_args(prev, n_in, out_idx):
    if prev is None:
        return [], [], {}
    return [pl.BlockSpec(memory_space=pl.ANY)], [prev], {n_in: out_idx}


def _conv_step_kernel(*refs, has_bias, aliased):
    refs = list(refs)
    x_ref, cs_ref, w_ref = refs[:3]
    b_ref = refs[3] if has_bias else None
    o_ref, cn_ref = refs[3 + int(has_bias) + int(aliased):]
    w = w_ref[...]
    x = x_ref[...]
    c0, c1, c2 = cs_ref[0, 0], cs_ref[0, 1], cs_ref[0, 2]
    acc = c0 * w[0:1, :] + c1 * w[1:2, :] + c2 * w[2:3, :] + x * w[3:4, :]
    if has_bias:
        acc = acc + b_ref[...]
    o_ref[...] = _silu(acc)
    cn_ref[0, 0] = c1
    cn_ref[0, 1] = c2
    cn_ref[0, 2] = x


def _conv_step(proj, col0, cstate, layer, conv_w, conv_b, prev):
    m = proj.shape[0]
    cdim = cstate.shape[3]
    cb = _pick(cdim, (512, 256, 128))
    x0 = col0 // cb
    taps = CONV_W - 1
    in_specs = [pl.BlockSpec((m, cb), lambda j: (0, x0 + j)),
                pl.BlockSpec((1, taps, m, cb), lambda j: (layer, 0, 0, j)),
                pl.BlockSpec((CONV_W, cb), lambda j: (0, j))]
    args = [proj, cstate, conv_w]
    if conv_b is not None:
        in_specs.append(pl.BlockSpec((1, cb), lambda j: (0, j)))
        args.append(conv_b)
    a_specs, a_args, aliases = _alias_args(prev, len(args), 1)
    return pl.pallas_call(
        functools.partial(_conv_step_kernel, has_bias=conv_b is not None, aliased=prev is not None),
        grid=(cdim // cb,),
        in_specs=in_specs + a_specs,
        out_specs=[pl.BlockSpec((m, cb), lambda j: (0, j)),
                   pl.BlockSpec((1, taps, m, cb), lambda j: (layer, 0, 0, j))],
        out_shape=[jax.ShapeDtypeStruct((m, cdim), F32),
                   jax.ShapeDtypeStruct(cstate.shape, F32)],
        input_output_aliases=aliases,
        compiler_params=_cparams(("parallel",)),
        name="conv_step",
    )(*args, *a_args)


def _ssd_step_kernel(*refs, nheads):
    z_ref, xs_ref, b_ref, c_ref, dt_ref, dtb_ref, alog_ref, d_ref, nw_ref, st_ref = refs[:10]
    y_ref, so_ref, ytb = refs[-3:]
    nb = xs_ref.shape[0]
    hd = SSM_HEAD_DIM
    xs = xs_ref[...]
    dtv = _softplus(dt_ref[...] + dtb_ref[...])
    decay = jnp.exp(dtv * (-jnp.exp(alog_ref[...])))
    pad = jnp.zeros((LANES - nb, LANES), F32)
    kmat = jnp.concatenate([b_ref[...], pad], axis=0).astype(BF16)
    qmat = jnp.concatenate([c_ref[...], pad], axis=0).T.astype(BF16)
    v_all = xs * _expand64(dtv, nheads)
    v_t = [jnp.concatenate([v_all[:, LANES * j:LANES * (j + 1)], pad], axis=0).T
           for j in range(nheads // 2)]
    lane = lax.broadcasted_iota(jnp.int32, (hd, LANES), 1)
    for r in range(nheads):
        vt = v_t[r // 2][hd * (r % 2):hd * (r % 2 + 1), :]
        for bi in range(nb):
            outer = jnp.dot(jnp.where(lane == bi, vt, 0.0).astype(BF16), kmat, preferred_element_type=F32)
            so_ref[0, bi, r] = st_ref[0, bi, r] * decay[bi:bi + 1, r:r + 1] + outer
    for r in range(nheads):
        acc = jnp.zeros((hd, LANES), F32)
        for bi in range(nb):
            yb = jnp.dot(so_ref[0, bi, r].astype(BF16), qmat, preferred_element_type=F32)
            acc = jnp.where(lane == bi, yb, acc)
        ytb[hd * r:hd * (r + 1), :] = acc
    y = [ytb[LANES * j:LANES * (j + 1), :].T[:nb, :] for j in range(nheads // 2)]
    y = y[0] if len(y) == 1 else jnp.concatenate(y, axis=1)
    y = y + xs * _expand64(d_ref[...], nheads)
    y = y * _silu(z_ref[...])
    y_ref[...] = (_rms(y) * nw_ref[...]).astype(BF16)


def _ssd_step(proj, xbc, dt, w, state, layer, prev):
    m = proj.shape[0]
    d_inner, nh, g, n = w["d_inner"], w["hpg"], SSM_GROUPS, SSM_D_STATE
    width = nh * SSM_HEAD_DIM
    nb = _pick(m, (2 * STEP_BATCH, STEP_BATCH))
    b0 = d_inner // n
    st_spec = pl.BlockSpec((1, nb, nh, SSM_HEAD_DIM, n), lambda i, gi: (layer, i, gi, 0, 0))
    in_specs = [
        pl.BlockSpec((nb, width), lambda i, gi: (i, gi)),
        pl.BlockSpec((nb, width), lambda i, gi: (i, gi)),
        pl.BlockSpec((nb, n), lambda i, gi: (i, b0 + gi)),
        pl.BlockSpec((nb, n), lambda i, gi: (i, b0 + g + gi)),
        pl.BlockSpec((nb, LANES), lambda i, gi: (i, gi)),
        pl.BlockSpec((1, LANES), lambda i, gi: (0, gi)),
        pl.BlockSpec((1, LANES), lambda i, gi: (0, gi)),
        pl.BlockSpec((1, LANES), lambda i, gi: (0, gi)),
        pl.BlockSpec((1, width), lambda i, gi: (0, gi)),
        st_spec,
    ]
    args = [proj, xbc, xbc, xbc, dt, w["dt_bias"], w["a_log"], w["d"], w["norm"], state]
    a_specs, a_args, aliases = _alias_args(prev, len(args), 1)
    y, st = pl.pallas_call(
        functools.partial(_ssd_step_kernel, nheads=nh),
        grid=(m // nb, g),
        in_specs=in_specs + a_specs,
        out_specs=[pl.BlockSpec((nb, width), lambda i, gi: (i, gi)), st_spec],
        out_shape=[jax.ShapeDtypeStruct((m, d_inner), BF16),
                   jax.ShapeDtypeStruct(state.shape, F32)],
        scratch_shapes=[pltpu.VMEM((width, LANES), F32)],
        input_output_aliases=aliases,
        compiler_params=_cparams(("parallel", "parallel")),
        name="ssd_step",
    )(*args, *a_args)
    return y, st


def _dn_step_kernel(q_ref, k_ref, v_ref, z_ref, ba_ref, alog_ref, dtb_ref, nw_ref, st_ref,
                    o_ref, so_ref, obuf, *, hv):
    nb = q_ref.shape[0]
    hd = DN_HEAD
    qq = q_ref[...]
    kk = k_ref[...]
    qq = qq * lax.rsqrt(jnp.sum(qq * qq, axis=-1, keepdims=True) + NORM_EPS) * (hd ** -0.5)
    kk = kk * lax.rsqrt(jnp.sum(kk * kk, axis=-1, keepdims=True) + NORM_EPS)
    vv = v_ref[...]
    ba = ba_ref[...]
    beta = _sigmoid(ba)
    eg = jnp.exp(-jnp.exp(alog_ref[...]) * _softplus(ba + dtb_ref[...]))
    k_t = _tr(kk)
    q_t = _tr(qq)
    for bi in range(nb):
        kcol = k_t[:, bi:bi + 1]
        qcol = q_t[:, bi:bi + 1]
        for j in range(2):
            s_old = st_ref[0, bi, j]
            b = beta[bi:bi + 1, j:j + 1]
            e = eg[bi:bi + 1, hv + j:hv + j + 1]
            ks = jnp.sum(kcol * s_old, axis=0, keepdims=True)
            u = b * vv[bi:bi + 1, hd * j:hd * (j + 1)] - (b * e) * ks
            s_new = s_old * e + kcol * u
            so_ref[0, bi, j] = s_new
            obuf[bi:bi + 1, hd * j:hd * (j + 1)] = jnp.sum(qcol * s_new, axis=0, keepdims=True)
    for j in range(2):
        o = obuf[:, hd * j:hd * (j + 1)]
        on = _rms(o) * nw_ref[...]
        o_ref[:, hd * j:hd * (j + 1)] = (on * _silu(z_ref[:, hd * j:hd * (j + 1)])).astype(BF16)


def _dn_step(proj, qkv, ba, w, state, layer):
    m = proj.shape[0]
    hk, hd = w["hk"], DN_HEAD
    nb = _pick(m, (2 * STEP_BATCH, STEP_BATCH))
    st_spec = pl.BlockSpec((1, nb, 2, hd, hd), lambda i, h: (layer, i, h, 0, 0))
    in_specs = [
        pl.BlockSpec((nb, hd), lambda i, h: (i, h)),
        pl.BlockSpec((nb, hd), lambda i, h: (i, hk + h)),
        pl.BlockSpec((nb, 2 * hd), lambda i, h: (i, hk + h)),
        pl.BlockSpec((nb, 2 * hd), lambda i, h: (i, 2 * hk + h)),
        pl.BlockSpec((nb, LANES), lambda i, h: (i, h)),
        pl.BlockSpec((1, LANES), lambda i, h: (0, h)),
        pl.BlockSpec((1, LANES), lambda i, h: (0, h)),
        pl.BlockSpec((1, hd), lambda i, h: (0, 0)),
        st_spec,
    ]
    o, st = pl.pallas_call(
        functools.partial(_dn_step_kernel, hv=2 * hk),
        grid=(m // nb, hk),
        in_specs=in_specs,
        out_specs=[pl.BlockSpec((nb, 2 * hd), lambda i, h: (i, h)), st_spec],
        out_shape=[jax.ShapeDtypeStruct((m, 2 * hk * hd), BF16),
                   jax.ShapeDtypeStruct(state.shape, F32)],
        scratch_shapes=[pltpu.VMEM((nb, 2 * hd), F32)],
        compiler_params=_cparams(("parallel", "parallel")),
        name="dn_step",
    )(qkv, qkv, qkv, proj, ba, w["a_log_step"], w["dt_bias_step"], w["norm"], state)
    return o, st


def _ret_step_kernel(q_ref, k_ref, v_ref, g_ref, cos_ref, s1_ref, s2_ref, lg_ref, st_ref, y_ref, so_ref, ybuf):
    nb = q_ref.shape[0]
    cos, s1, s2 = cos_ref[...], s1_ref[...], s2_ref[...]
    qq = _rotate(q_ref[...], cos, s1, s2)
    kk = _rotate(k_ref[...], cos, s1, s2) * (RET_DK ** -0.5)
    vv = v_ref[...]
    gamma = jnp.exp(lg_ref[0][0:1, 0:1])
    nk = RET_DK // LANES
    k_t = jnp.concatenate([_tr(kk[:, LANES * i:LANES * (i + 1)]) for i in range(nk)], axis=0)
    q_t = jnp.concatenate([_tr(qq[:, LANES * i:LANES * (i + 1)]) for i in range(nk)], axis=0)
    for bi in range(nb):
        s_new = st_ref[0, bi, 0] * gamma + k_t[:, bi:bi + 1] * vv[bi:bi + 1, :]
        so_ref[0, bi, 0] = s_new
        ybuf[bi:bi + 1, :] = jnp.sum(q_t[:, bi:bi + 1] * s_new, axis=0, keepdims=True)
    y_ref[...] = (_rms(ybuf[...]) * _silu(g_ref[...])).astype(BF16)


def _ret_step(proj, rope, lg, nheads, state, layer):
    m = proj.shape[0]
    nb = _pick(m, (STEP_BATCH,))
    v0 = 2 * nheads * RET_DK // RET_DV
    g0 = v0 + nheads
    cos, s1, s2 = rope
    st_spec = pl.BlockSpec((1, nb, 1, RET_DK, RET_DV), lambda i, h: (layer, i, h, 0, 0))
    in_specs = [
        pl.BlockSpec((nb, RET_DK), lambda i, h: (i, h)),
        pl.BlockSpec((nb, RET_DK), lambda i, h: (i, nheads + h)),
        pl.BlockSpec((nb, RET_DV), lambda i, h: (i, v0 + h)),
        pl.BlockSpec((nb, RET_DV), lambda i, h: (i, g0 + h)),
        pl.BlockSpec((1, RET_DK), lambda i, h: (0, 0)),
        pl.BlockSpec((1, RET_DK), lambda i, h: (0, 0)),
        pl.BlockSpec((1, RET_DK), lambda i, h: (0, 0)),
        pl.BlockSpec((1, SUBLANES, LANES), lambda i, h: (h, 0, 0)),
        st_spec,
    ]
    y, st = pl.pallas_call(
        _ret_step_kernel,
        grid=(m // nb, nheads),
        in_specs=in_specs,
        out_specs=[pl.BlockSpec((nb, RET_DV), lambda i, h: (i, h)), st_spec],
        out_shape=[jax.ShapeDtypeStruct((m, nheads * RET_DV), BF16),
                   jax.ShapeDtypeStruct(state.shape, F32)],
        scratch_shapes=[pltpu.VMEM((nb, RET_DV), F32)],
        compiler_params=_cparams(("parallel", "parallel")),
        name="ret_step",
    )(proj, proj, proj, proj, cos, s1, s2, lg, state)
    return y, st


def _pad_lanes(x):
    return jnp.pad(x, [(0, 0)] * (x.ndim - 1) + [(0, LANES - x.shape[-1])])


def _prep_ssm(conv_w, conv_b, dt_bias, a_log, d_skip, norm_w):
    heads = dt_bias.shape[0]
    g = SSM_GROUPS
    hpg = heads // g
    d_inner = heads * SSM_HEAD_DIM
    conv_dim = conv_w.shape[1]
    main = d_inner + conv_dim
    per_group = lambda v: _pad_lanes(v.reshape(g, hpg)).reshape(1, g * LANES)
    return dict(d_inner=d_inner, hpg=hpg, conv_dim=conv_dim, main=main,
                conv_w=conv_w, conv_b=conv_b.reshape(1, conv_dim),
                dt_bias=per_group(dt_bias), a_log=per_group(a_log), d=per_group(d_skip),
                norm=norm_w.reshape(1, d_inner))


def _prep_dn(conv_w, a_log, dt_bias, norm_w):
    hv = a_log.shape[0]
    hk = hv // 2
    hb = _pick(hk, (DN_HEADS_PER_STEP, 2, 1))
    conv_dim = conv_w.shape[1]
    main = conv_dim + hv * DN_HEAD

    def table(v, heads_per_step):
        t = v.reshape(hk // heads_per_step, 2 * heads_per_step)
        t = jnp.pad(t, ((0, 0), (hv, LANES - hv - 2 * heads_per_step)))
        return t.reshape(1, -1)

    return dict(hk=hk, hb=hb, conv_dim=conv_dim, main=main, conv_w=conv_w,
                a_log=table(a_log, hb), dt_bias=table(dt_bias, hb),
                a_log_step=table(a_log, 1), dt_bias_step=table(dt_bias, 1),
                norm=norm_w.reshape(1, DN_HEAD))


def _rope_tables(pos):
    half = RET_DK // 2
    inv = 1.0 / (RET_ROPE_BASE ** jnp.linspace(0.0, 1.0, half, dtype=F32))
    ang = pos.astype(F32)[:, None] * inv[None, :]
    cos, sin, zero = jnp.cos(ang), jnp.sin(ang), jnp.zeros_like(ang)
    inter = lambda a, b: jnp.stack([a, b], axis=-1).reshape(pos.shape[0], RET_DK)
    return inter(cos, cos), inter(-sin, zero), inter(zero, sin)


def _forward(x, p, prompt, states, prm, wts):
    batch, seq, d_model = x.shape
    m = batch * seq
    h = x.reshape(m, d_model)
    p = p.reshape(p.shape[0], m, p.shape[-1])
    depth = prm["norm_mix_pre"].shape[0]
    ssm_s, ssm_c, dn_s, dn_c, ret_s = states
    o_ssm = o_ssm_c = o_dn = o_dn_c = o_ret = None
    p_ssm, p_ssm_c, p_dn, p_dn_c, p_ret = [], [], [], [], []
    tail = slice(seq - (CONV_W - 1), seq)
    xn = _norm(h, prm["norm_mix_pre"][0])
    for i in range(depth):
        kind, j = i % 3, i // 3
        if kind == 0:
            w = wts["ssm"][j]
            d_inner, conv_dim = w["d_inner"], w["conv_dim"]
            proj = _mm(xn, wts["ssm_in_t"], j, w["main"], True)
            dt = _mm_small(xn, wts["ssm_in_t"], j, w["main"], SSM_GROUPS * w["hpg"], SSM_GROUPS, w["hpg"])
            if prompt:
                y, st = _ssd_prompt(proj, dt, w, batch, seq)
                p_ssm.append(st)
                p_ssm_c.append(proj.reshape(batch, seq, -1)[:, tail, d_inner:d_inner + conv_dim])
            else:
                xbc, o_ssm_c = _conv_step(proj, d_inner, ssm_c, j, w["conv_w"], w["conv_b"], o_ssm_c)
                y, o_ssm = _ssd_step(proj, xbc, dt, w, ssm_s, j, o_ssm)
            w_out = wts["ssm_out"]
        elif kind == 1:
            w = wts["dn"][j]
            conv_dim = w["conv_dim"]
            proj = _mm(xn, wts["dn_in_t"], j, w["main"], True)
            nblk, width = (w["hk"] // w["hb"], 2 * w["hb"]) if prompt else (w["hk"], 2)
            ba = _mm_small(xn, wts["dn_in_t"], j, w["main"], 4 * w["hk"], nblk, width)
            if prompt:
                y, st = _dn_prompt(proj, ba, w, batch, seq)
                p_dn.append(st)
                p_dn_c.append(proj.reshape(batch, seq, -1)[:, tail, :conv_dim])
            else:
                qkv, o_dn_c = _conv_step(proj, 0, dn_c, j, w["conv_w"], None, o_dn_c)
                y, o_dn = _dn_step(proj, qkv, ba, w, dn_s, j)
            w_out = wts["dn_out"]
        else:
            nheads = wts["ret_heads"]
            proj = _mm(xn, wts["ret_in"], j, wts["ret_in"].shape[2], False)
            if prompt:
                y, st = _ret_prompt(proj, wts["rope"], wts["lg"], nheads, batch, seq)
                p_ret.append(st)
            else:
                y, o_ret = _ret_step(proj, wts["rope"], wts["lg"], nheads, ret_s, j)
            w_out = wts["ret_out"]
        h, xn = _mm_out(y, w_out, j, h, prm["norm_mix_post"][i], prm["norm_ffn_pre"][i])
        act = _ffn_in(xn, wts["ffn_gate"], wts["ffn_up"], i)
        h, xn = _mm_out(act, wts["ffn_down"], i, h, prm["norm_ffn_post"][i], prm["norm_ple"][i])
        gain_next = prm["norm_mix_pre"][i + 1] if i + 1 < depth else None
        h, xn = _ple(h, xn, wts["ple_gate"], p, wts["ple_proj"], i, gain_next)
    h = h.reshape(batch, seq, d_model)
    if prompt:
        return (h, jnp.swapaxes(jnp.stack(p_ssm), -1, -2), jnp.stack(p_ssm_c), jnp.stack(p_dn),
                jnp.stack(p_dn_c), jnp.stack(p_ret))
    return (h, jnp.swapaxes(o_ssm, -1, -2), jnp.swapaxes(o_ssm_c, 1, 2), o_dn, jnp.swapaxes(o_dn_c, 1, 2), o_ret)


def kernel(x_prompt, x_sample, state_ssm, state_ssm_conv, state_delta, state_delta_conv, state_ret, p_prompt, p_sample, norm_mix_pre, norm_mix_post, norm_ffn_pre, norm_ffn_post, norm_ple, ffn_w_gate, ffn_w_up, ffn_w_down, ple_w_proj, ple_w_gate, ssm_w_in, ssm_conv_w, ssm_conv_b, ssm_dt_bias, ssm_a_log, ssm_d, ssm_norm, ssm_w_out, dn_w_in, dn_conv_w, dn_a_log, dn_dt_bias, dn_norm, dn_w_out, ret_w_in, ret_w_out):
    assert x_sample.shape[1] == 1, "the sample group advances one token per sequence"
    prm = dict(norm_mix_pre=norm_mix_pre, norm_mix_post=norm_mix_post, norm_ffn_pre=norm_ffn_pre,
               norm_ffn_post=norm_ffn_post, norm_ple=norm_ple)
    ret_heads = ret_w_out.shape[1] // RET_DV
    log_gamma = jnp.log(1.0 - 2.0 ** (-5.0 - jnp.arange(ret_heads, dtype=F32)))
    ssm_in_t, dn_in_t = jnp.swapaxes(ssm_w_in, 1, 2), jnp.swapaxes(dn_w_in, 1, 2)
    base = dict(
        ssm=[_prep_ssm(ssm_conv_w[j], ssm_conv_b[j], ssm_dt_bias[j], ssm_a_log[j], ssm_d[j],
                       ssm_norm[j]) for j in range(ssm_w_in.shape[0])],
        dn=[_prep_dn(dn_conv_w[j], dn_a_log[j], dn_dt_bias[j], dn_norm[j])
            for j in range(dn_w_in.shape[0])],
        ssm_in_t=ssm_in_t, dn_in_t=dn_in_t,
        ssm_out=ssm_w_out.astype(BF16), dn_out=dn_w_out.astype(BF16),
        ret_in=ret_w_in, ret_out=ret_w_out.astype(BF16), ret_heads=ret_heads,
        ffn_gate=ffn_w_gate, ffn_up=ffn_w_up, ffn_down=ffn_w_down.astype(BF16),
        ple_gate=ple_w_gate.astype(BF16), ple_proj=ple_w_proj.astype(BF16),
        lg=jnp.broadcast_to(log_gamma[:, None, None], (ret_heads, SUBLANES, LANES)),
    )
    seq = x_prompt.shape[1]
    pos_prompt = jnp.arange(seq, dtype=jnp.int32)
    pos_sample = PAST_LEN + jnp.arange(1, dtype=jnp.int32)
    out_p = _forward(x_prompt, p_prompt, True, (None,) * 5, prm, dict(base, rope=_rope_tables(pos_prompt)))
    states = (jnp.swapaxes(state_ssm, -1, -2), jnp.swapaxes(state_ssm_conv, 1, 2), state_delta,
              jnp.swapaxes(state_delta_conv, 1, 2), state_ret)
    out_s = _forward(x_sample, p_sample, False, states, prm, dict(base, rope=_rope_tables(pos_sample)))
    return (out_p[0], out_s[0]) + out_p[1:] + out_s[1:]
```

```python
import functools

import jax
import jax.numpy as jnp
from jax import lax
from jax.experimental import pallas as pl
from jax.experimental.pallas import tpu as pltpu

F32 = jnp.float32
BF16 = jnp.bfloat16

NORM_EPS = 1e-6
PAST_LEN = 16384
CONV_W = 4
SSM_GROUPS = 8
SSM_HEAD_DIM = 64
SSM_D_STATE = 128
DN_HEAD = 128
RET_DK = 256
RET_DV = 512
RET_ROPE_BASE = 10000.0

LANES = 128
SUBLANES = 8
NEG_BIG = -1e30
SSD_CHUNK = 128
RET_CHUNK = 128
DN_CHUNK = 64
DN_HEADS_PER_STEP = 8
RET_HEADS_PER_STEP = 4
STEP_BATCH = 8
VMEM_LIMIT = 52 * 1024 * 1024


def _pick(n, prefs):
    for p in prefs:
        if n % p == 0:
            return p
    return n


def _cparams(sem):
    return pltpu.CompilerParams(dimension_semantics=sem, vmem_limit_bytes=VMEM_LIMIT)


def _sigmoid(x):
    return 1.0 / (1.0 + jnp.exp(-x))


def _silu(x):
    h = 0.5 * x
    return h + h * jnp.tanh(h)


def _softplus(x):
    return jnp.maximum(x, 0.0) + jnp.log1p(jnp.exp(-jnp.abs(x)))


def _dot(a, b):
    return jnp.dot(a.astype(BF16), b.astype(BF16), preferred_element_type=F32)


def _dot_nt(a, b):
    return lax.dot_general(a.astype(BF16), b.astype(BF16), (((1,), (1,)), ((), ())),
                           preferred_element_type=F32)


def _dot01(m01, x):
    hi = x.astype(BF16)
    r = x - hi.astype(F32)
    mid = r.astype(BF16)
    lo = (r - mid.astype(F32)).astype(BF16)
    out = jnp.dot(m01, hi, preferred_element_type=F32)
    out = out + jnp.dot(m01, mid, preferred_element_type=F32)
    return out + jnp.dot(m01, lo, preferred_element_type=F32)


def _tr(x):
    r, c = x.shape
    assert c == LANES and r <= LANES
    if r < LANES:
        x = jnp.concatenate([x, jnp.zeros((LANES - r, c), x.dtype)], axis=0)
    return x.T[:, :r]


def _expand64(x, nheads):
    rows = x.shape[0]
    lane = lax.broadcasted_iota(jnp.int32, (rows, LANES), 1)
    parts = []
    for j in range(nheads // 2):
        a = jnp.broadcast_to(x[:, 2 * j:2 * j + 1], (rows, LANES))
        b = jnp.broadcast_to(x[:, 2 * j + 1:2 * j + 2], (rows, LANES))
        parts.append(jnp.where(lane < 64, a, b))
    return parts[0] if len(parts) == 1 else jnp.concatenate(parts, axis=1)


def _rms(y):
    return y * lax.rsqrt(jnp.mean(y * y, axis=-1, keepdims=True) + NORM_EPS)


def _conv_chunk(buf_ref, col0, x, w, bias):
    q, width = x.shape
    cols = slice(col0, col0 + width)
    buf_ref[SUBLANES:SUBLANES + q, cols] = x
    acc = x * w[CONV_W - 1:CONV_W, :]
    for s in range(1, CONV_W):
        acc = acc + buf_ref[SUBLANES - s:SUBLANES - s + q, cols] * w[CONV_W - 1 - s:CONV_W - s, :]
    buf_ref[0:SUBLANES, cols] = x[q - SUBLANES:q, :]
    if bias is not None:
        acc = acc + bias
    return acc


def _bdot(a, b):
    return lax.dot_general(a.astype(BF16), b.astype(BF16), (((2,), (1,)), ((0,), (0,))),
                           preferred_element_type=F32)


def _bdot_nt(a, b):
    return lax.dot_general(a.astype(BF16), b.astype(BF16), (((2,), (2,)), ((0,), (0,))),
                           preferred_element_type=F32)


def _tri_inv(a, row, col):
    q = a.shape[-1]
    eye = (row == col).astype(F32)[None]
    d = eye - jnp.where((jnp.right_shift(row, 1) == jnp.right_shift(col, 1))[None], a, 0.0)
    sh = 1
    while (1 << sh) < q:
        same_big = jnp.right_shift(row, sh + 1) == jnp.right_shift(col, sh + 1)
        diff_small = jnp.right_shift(row, sh) != jnp.right_shift(col, sh)
        lb = jnp.where((same_big & diff_small)[None], a, 0.0)
        d = d - _bdot(d, _bdot(lb, d))
        sh += 1
    return d


def _two_group_specs(tm, ms, ntile, cols, col_of=None):
    if col_of is None:
        return (pl.BlockSpec((tm, cols), lambda i: (jnp.minimum(i, ntile - 1), 0)),
                pl.BlockSpec((ms, cols), lambda i: (0, 0)))
    return (pl.BlockSpec((tm, cols), lambda j, i: (jnp.minimum(i, ntile - 1), col_of(j))),
            pl.BlockSpec((ms, cols), lambda j, i: (0, col_of(j))))


def _norm_kernel(xp_ref, xs_ref, g_ref, op_ref, os_ref, *, ntile):
    i = pl.program_id(0)

    @pl.when(i < ntile)
    def _():
        op_ref[...] = (_rms(xp_ref[...]) * g_ref[...]).astype(BF16)

    @pl.when(i == ntile)
    def _():
        os_ref[...] = (_rms(xs_ref[...]) * g_ref[...]).astype(BF16)


def _norm(xp, xs, gain):
    mp, k = xp.shape
    ms = xs.shape[0]
    tm = _pick(mp, (512, 256, 128))
    ntile = mp // tm
    p_spec, s_spec = _two_group_specs(tm, ms, ntile, k)
    return pl.pallas_call(
        functools.partial(_norm_kernel, ntile=ntile),
        grid=(ntile + 1,),
        in_specs=[p_spec, s_spec, pl.BlockSpec((1, k), lambda i: (0, 0))],
        out_specs=[p_spec, s_spec],
        out_shape=[jax.ShapeDtypeStruct((mp, k), BF16), jax.ShapeDtypeStruct((ms, k), BF16)],
        compiler_params=_cparams(("arbitrary",)),
        name="norm",
    )(xp, xs, gain.reshape(1, k))


def _mm_kernel(xp_ref, xs_ref, w_ref, op_ref, os_ref, wb_ref, *, transposed, ntile):
    i = pl.program_id(1)

    @pl.when(i == 0)
    def _():
        wb_ref[...] = w_ref[0].astype(BF16)

    def mm(x):
        if transposed:
            return lax.dot_general(x, wb_ref[...], (((1,), (1,)), ((), ())), preferred_element_type=F32)
        return jnp.dot(x, wb_ref[...], preferred_element_type=F32)

    @pl.when(i < ntile)
    def _():
        op_ref[...] = mm(xp_ref[...])

    @pl.when(i == ntile)
    def _():
        os_ref[...] = mm(xs_ref[...])


def _mm(xp, xs, w, layer, n, transposed):
    mp, k = xp.shape
    ms = xs.shape[0]
    tm = _pick(mp, (1024, 512, 256, 128))
    tn = _pick(n, (1024, 512, 256, 128))
    ntile = mp // tm
    if transposed:
        w_spec = pl.BlockSpec((1, tn, k), lambda j, i: (layer, j, 0))
        w_scratch = pltpu.VMEM((tn, k), BF16)
    else:
        w_spec = pl.BlockSpec((1, k, tn), lambda j, i: (layer, 0, j))
        w_scratch = pltpu.VMEM((k, tn), BF16)
    xp_spec, xs_spec = _two_group_specs(tm, ms, ntile, k, lambda j: 0)
    op_spec, os_spec = _two_group_specs(tm, ms, ntile, tn, lambda j: j)
    return pl.pallas_call(
        functools.partial(_mm_kernel, transposed=transposed, ntile=ntile),
        grid=(n // tn, ntile + 1),
        in_specs=[xp_spec, xs_spec, w_spec],
        out_specs=[op_spec, os_spec],
        out_shape=[jax.ShapeDtypeStruct((mp, n), F32), jax.ShapeDtypeStruct((ms, n), F32)],
        scratch_shapes=[w_scratch],
        compiler_params=_cparams(("parallel", "arbitrary")),
        name="mm",
    )(xp, xs, w)


def _mm_small_kernel(xp_ref, xs_ref, w_ref, *out_refs, ntile, widths):
    i = pl.program_id(0)
    w = w_ref[0]
    rows, k = w.shape
    if rows < LANES:
        w = jnp.concatenate([w, jnp.zeros((LANES - rows, k), w.dtype)], axis=0)

    def project(x_ref, o_ref, width):
        y = _dot_nt(x_ref[...], w)
        for b in range(o_ref.shape[1] // LANES):
            shift = (LANES - b * width) % LANES
            o_ref[:, LANES * b:LANES * (b + 1)] = y if shift == 0 else pltpu.roll(y, shift, 1)

    @pl.when(i < ntile)
    def _():
        project(xp_ref, out_refs[0], widths[0])

    @pl.when(i == ntile)
    def _():
        project(xs_ref, out_refs[1], widths[1])


def _mm_small(xp, xs, w_t, layer, row0, rows, blocks_p, blocks_s):
    mp, k = xp.shape
    ms = xs.shape[0]
    tm = _pick(mp, (1024, 512, 256, 128))
    ntile = mp // tm
    assert rows <= LANES and row0 % rows == 0 and rows % SUBLANES == 0
    xp_spec, xs_spec = _two_group_specs(tm, ms, ntile, k)
    op_spec, _ = _two_group_specs(tm, ms, ntile, blocks_p[0] * LANES)
    _, os_spec = _two_group_specs(tm, ms, ntile, blocks_s[0] * LANES)
    return pl.pallas_call(
        functools.partial(_mm_small_kernel, ntile=ntile, widths=(blocks_p[1], blocks_s[1])),
        grid=(ntile + 1,),
        in_specs=[xp_spec, xs_spec, pl.BlockSpec((1, rows, k), lambda i: (layer, row0 // rows, 0))],
        out_specs=[op_spec, os_spec],
        out_shape=[jax.ShapeDtypeStruct((mp, blocks_p[0] * LANES), F32),
                   jax.ShapeDtypeStruct((ms, blocks_s[0] * LANES), F32)],
        compiler_params=_cparams(("arbitrary",)),
        name="mm_small",
    )(xp, xs, w_t)


def _mm_out_kernel(a_ref, w_ref, h_ref, g_ref, gn_ref, o_ref, xn_ref):
    y = jnp.dot(a_ref[...], w_ref[0], preferred_element_type=F32)
    h = h_ref[...] + _rms(y) * g_ref[...]
    o_ref[...] = h
    xn_ref[...] = (_rms(h) * gn_ref[...]).astype(BF16)


def _mm_out(a, w, layer, h, gain, gain_next):
    m, k = a.shape
    n = w.shape[2]
    tm = _pick(m, (256, 128))
    return pl.pallas_call(
        _mm_out_kernel,
        grid=(m // tm,),
        in_specs=[pl.BlockSpec((tm, k), lambda i: (i, 0)),
                  pl.BlockSpec((1, k, n), lambda i: (layer, 0, 0), pipeline_mode=pl.Buffered(1)),
                  pl.BlockSpec((tm, n), lambda i: (i, 0)),
                  pl.BlockSpec((1, n), lambda i: (0, 0)),
                  pl.BlockSpec((1, n), lambda i: (0, 0))],
        out_specs=[pl.BlockSpec((tm, n), lambda i: (i, 0)),
                   pl.BlockSpec((tm, n), lambda i: (i, 0))],
        out_shape=[jax.ShapeDtypeStruct((m, n), F32),
                   jax.ShapeDtypeStruct((m, n), BF16)],
        compiler_params=_cparams(("parallel",)),
        name="mm_out",
    )(a, w, h, gain.reshape(1, n), gain_next.reshape(1, n))


def _ffn_in_kernel(xp_ref, xs_ref, wg_ref, wu_ref, op_ref, os_ref, wgb_ref, wub_ref, *, ntile):
    i = pl.program_id(1)

    @pl.when(i == 0)
    def _():
        wgb_ref[...] = wg_ref[0].astype(BF16)
        wub_ref[...] = wu_ref[0].astype(BF16)

    def body(x_ref, o_ref):
        xn = x_ref[...]
        gate = jnp.dot(xn, wgb_ref[...], preferred_element_type=F32)
        up = jnp.dot(xn, wub_ref[...], preferred_element_type=F32)
        o_ref[...] = (_silu(gate) * up).astype(BF16)

    @pl.when(i < ntile)
    def _():
        body(xp_ref, op_ref)

    @pl.when(i == ntile)
    def _():
        body(xs_ref, os_ref)


def _ffn_in(xp, xs, wg, wu, layer):
    mp, k = xp.shape
    ms = xs.shape[0]
    n = wg.shape[2]
    tm = _pick(mp, (1024, 512, 256, 128))
    tn = _pick(n, (512, 256, 128))
    ntile = mp // tm
    xp_spec, xs_spec = _two_group_specs(tm, ms, ntile, k, lambda j: 0)
    op_spec, os_spec = _two_group_specs(tm, ms, ntile, tn, lambda j: j)
    w_spec = pl.BlockSpec((1, k, tn), lambda j, i: (layer, 0, j))
    return pl.pallas_call(
        functools.partial(_ffn_in_kernel, ntile=ntile),
        grid=(n // tn, ntile + 1),
        in_specs=[xp_spec, xs_spec, w_spec, w_spec],
        out_specs=[op_spec, os_spec],
        out_shape=[jax.ShapeDtypeStruct((mp, n), BF16), jax.ShapeDtypeStruct((ms, n), BF16)],
        scratch_shapes=[pltpu.VMEM((k, tn), BF16), pltpu.VMEM((k, tn), BF16)],
        compiler_params=_cparams(("parallel", "arbitrary")),
        name="ffn_in",
    )(xp, xs, wg, wu)


def _ple_kernel(*refs, has_next, ntile):
    hp_ref, hs_ref, xnp_ref, xns_ref, wg_ref, pp_ref, ps_ref, wp_ref = refs[:8]
    if has_next:
        gn_ref, op_ref, os_ref, onp_ref, ons_ref = refs[8:]
    else:
        op_ref, os_ref = refs[8:]
        onp_ref = ons_ref = None
    i = pl.program_id(0)

    def body(h_ref, xn_ref, p_ref, o_ref, on_ref):
        gate = _sigmoid(jnp.dot(xn_ref[...], wg_ref[0], preferred_element_type=F32))
        proj = jnp.dot(p_ref[0].astype(BF16), wp_ref[0], preferred_element_type=F32)
        h = h_ref[...] + proj * gate
        o_ref[...] = h
        if has_next:
            on_ref[...] = (_rms(h) * gn_ref[...]).astype(BF16)

    @pl.when(i < ntile)
    def _():
        body(hp_ref, xnp_ref, pp_ref, op_ref, onp_ref)

    @pl.when(i == ntile)
    def _():
        body(hs_ref, xns_ref, ps_ref, os_ref, ons_ref)


def _ple(hp, hs, xnp, xns, wg, pp, ps, wp, layer, gain_next):
    mp, k = hp.shape
    ms = hs.shape[0]
    n = wg.shape[2]
    pd = pp.shape[2]
    tm = _pick(mp, (256, 128))
    ntile = mp // tm
    has_next = gain_next is not None
    r_specs = _two_group_specs(tm, ms, ntile, k)
    pp_spec = pl.BlockSpec((1, tm, pd), lambda i: (layer, jnp.minimum(i, ntile - 1), 0))
    ps_spec = pl.BlockSpec((1, ms, pd), lambda i: (layer, 0, 0))
    in_specs = [r_specs[0], r_specs[1], r_specs[0], r_specs[1],
                pl.BlockSpec((1, k, n), lambda i: (layer, 0, 0), pipeline_mode=pl.Buffered(1)),
                pp_spec, ps_spec,
                pl.BlockSpec((1, pd, n), lambda i: (layer, 0, 0), pipeline_mode=pl.Buffered(1))]
    args = [hp, hs, xnp, xns, wg, pp, ps, wp]
    o_specs = _two_group_specs(tm, ms, ntile, n)
    out_specs = [o_specs[0], o_specs[1]]
    out_shape = [jax.ShapeDtypeStruct((mp, n), F32), jax.ShapeDtypeStruct((ms, n), F32)]
    if has_next:
        in_specs.append(pl.BlockSpec((1, n), lambda i: (0, 0)))
        args.append(gain_next.reshape(1, n))
        out_specs += [o_specs[0], o_specs[1]]
        out_shape += [jax.ShapeDtypeStruct((mp, n), BF16), jax.ShapeDtypeStruct((ms, n), BF16)]
    out = pl.pallas_call(
        functools.partial(_ple_kernel, has_next=has_next, ntile=ntile),
        grid=(ntile + 1,),
        in_specs=in_specs,
        out_specs=out_specs,
        out_shape=out_shape,
        compiler_params=_cparams(("arbitrary",)),
        name="ple",
    )(*args)
    return tuple(out) if has_next else (out[0], out[1], None, None)


def _ssd_prompt_kernel(z_ref, xs_ref, b_ref, c_ref, dt_ref, wx_ref, wb_ref, wc_ref, bx_ref, bb_ref, bc_ref,
                       dtb_ref, alog_ref, d_ref, nw_ref, y_ref, st_ref, s_scr, cbuf, *, nheads):
    c = pl.program_id(2)
    q, width = xs_ref.shape
    n = b_ref.shape[1]
    hd = SSM_HEAD_DIM

    @pl.when(c == 0)
    def _():
        s_scr[...] = jnp.zeros_like(s_scr)
        cbuf[0:SUBLANES, :] = jnp.zeros((SUBLANES, cbuf.shape[1]), F32)

    xs = _silu(_conv_chunk(cbuf, 0, xs_ref[...], wx_ref[...], bx_ref[...]))
    bm = _silu(_conv_chunk(cbuf, width, b_ref[...], wb_ref[...], bb_ref[...]))
    cm = _silu(_conv_chunk(cbuf, width + n, c_ref[...], wc_ref[...], bc_ref[...]))

    dtv = _softplus(dt_ref[...] + dtb_ref[...])
    la = dtv * (-jnp.exp(alog_ref[...]))
    row = lax.broadcasted_iota(jnp.int32, (q, q), 0)
    col = lax.broadcasted_iota(jnp.int32, (q, q), 1)
    causal = row >= col
    acs = _dot01(causal.astype(BF16), la)
    acs_t = _tr(acs)
    scores = _dot_nt(cm, bm)
    dt_t = _tr(dtv)
    s_old = s_scr[...]
    y = _dot_nt(cm, s_old) * _expand64(jnp.exp(acs), nheads)
    lane = lax.broadcasted_iota(jnp.int32, (q, LANES), 1)
    parts = []
    for j in range(nheads // 2):
        xp = xs[:, LANES * j:LANES * (j + 1)].astype(BF16)
        ys = []
        for t in range(2):
            r = 2 * j + t
            seg = acs[:, r:r + 1] - acs_t[r:r + 1, :]
            decay = jnp.exp(jnp.where(causal, seg, NEG_BIG))
            ys.append(jnp.dot((scores * decay * dt_t[r:r + 1, :]).astype(BF16), xp, preferred_element_type=F32))
        parts.append(jnp.where(lane < 64, ys[0], ys[1]))
    y = y + (parts[0] if len(parts) == 1 else jnp.concatenate(parts, axis=1))
    y = y + xs * _expand64(d_ref[...], nheads)

    last = acs[q - 1:q, :]
    wsc_t = _tr(jnp.exp(last - acs) * dtv)
    sub = lax.broadcasted_iota(jnp.int32, (LANES, q), 0)
    wv_t = []
    for j in range(width // LANES):
        scale = jnp.where(sub < hd, wsc_t[2 * j:2 * j + 1, :], wsc_t[2 * j + 1:2 * j + 2, :])
        wv_t.append(_tr(xs[:, LANES * j:LANES * (j + 1)]) * scale)
    wv_t = wv_t[0] if len(wv_t) == 1 else jnp.concatenate(wv_t, axis=0)
    e_last = jnp.exp(last)
    e_rows = jnp.concatenate([jnp.broadcast_to(e_last[:, r:r + 1], (hd, n)) for r in range(nheads)], axis=0)
    s_new = s_old * e_rows + _dot(wv_t, bm)
    s_scr[...] = s_new

    y = y * _silu(z_ref[...])
    y_ref[...] = (_rms(y) * nw_ref[...]).astype(BF16)

    @pl.when(c == pl.num_programs(2) - 1)
    def _():
        for r in range(nheads):
            st_ref[0, r] = s_new[hd * r:hd * (r + 1), :]


def _ssd_prompt(proj, dt, w, batch, seq):
    d_inner, nh, g, n = w["d_inner"], w["hpg"], SSM_GROUPS, SSM_D_STATE
    width = nh * SSM_HEAD_DIM
    q = _pick(seq, (SSD_CHUNK, 64, 32, 16, 8))
    nc = seq // q
    xs0, b0, c0 = d_inner // width, 2 * d_inner // n, 2 * d_inner // n + g
    cw0 = d_inner // n
    tok = lambda b, gi, c: b * nc + c
    in_specs = [
        pl.BlockSpec((q, width), lambda b, gi, c: (tok(b, gi, c), gi)),
        pl.BlockSpec((q, width), lambda b, gi, c: (tok(b, gi, c), xs0 + gi)),
        pl.BlockSpec((q, n), lambda b, gi, c: (tok(b, gi, c), b0 + gi)),
        pl.BlockSpec((q, n), lambda b, gi, c: (tok(b, gi, c), c0 + gi)),
        pl.BlockSpec((q, LANES), lambda b, gi, c: (tok(b, gi, c), gi)),
        pl.BlockSpec((CONV_W, width), lambda b, gi, c: (0, gi)),
        pl.BlockSpec((CONV_W, n), lambda b, gi, c: (0, cw0 + gi)),
        pl.BlockSpec((CONV_W, n), lambda b, gi, c: (0, cw0 + g + gi)),
        pl.BlockSpec((1, width), lambda b, gi, c: (0, gi)),
        pl.BlockSpec((1, n), lambda b, gi, c: (0, cw0 + gi)),
        pl.BlockSpec((1, n), lambda b, gi, c: (0, cw0 + g + gi)),
        pl.BlockSpec((1, LANES), lambda b, gi, c: (0, gi)),
        pl.BlockSpec((1, LANES), lambda b, gi, c: (0, gi)),
        pl.BlockSpec((1, LANES), lambda b, gi, c: (0, gi)),
        pl.BlockSpec((1, width), lambda b, gi, c: (0, gi)),
    ]
    y, st = pl.pallas_call(
        functools.partial(_ssd_prompt_kernel, nheads=nh),
        grid=(batch, g, nc),
        in_specs=in_specs,
        out_specs=[pl.BlockSpec((q, width), lambda b, gi, c: (tok(b, gi, c), gi)),
                   pl.BlockSpec((1, nh, SSM_HEAD_DIM, n), lambda b, gi, c: (b, gi, 0, 0))],
        out_shape=[jax.ShapeDtypeStruct((batch * seq, d_inner), BF16),
                   jax.ShapeDtypeStruct((batch, g * nh, SSM_HEAD_DIM, n), F32)],
        scratch_shapes=[pltpu.VMEM((width, n), F32),
                        pltpu.VMEM((q + SUBLANES, width + 2 * n), F32)],
        compiler_params=_cparams(("parallel", "parallel", "arbitrary")),
        name="ssd_prompt",
    )(proj, proj, proj, proj, dt, w["conv_w"], w["conv_w"], w["conv_w"], w["conv_b"], w["conv_b"], w["conv_b"],
      w["dt_bias"], w["a_log"], w["d"], w["norm"])
    return y, st


def _dn_prompt_kernel(q_ref, k_ref, v_ref, z_ref, ba_ref, wq_ref, wk_ref, wv_ref, alog_ref, dtb_ref, nw_ref,
                      o_ref, st_ref, s_scr, cbuf, *, hb, hv):
    c = pl.program_id(2)
    q = q_ref.shape[0]
    hd = DN_HEAD

    @pl.when(c == 0)
    def _():
        s_scr[...] = jnp.zeros_like(s_scr)
        cbuf[0:SUBLANES, :] = jnp.zeros((SUBLANES, cbuf.shape[1]), F32)

    qa = _silu(_conv_chunk(cbuf, 0, q_ref[...], wq_ref[...], None))
    ka = _silu(_conv_chunk(cbuf, hb * hd, k_ref[...], wk_ref[...], None))
    va = _silu(_conv_chunk(cbuf, 2 * hb * hd, v_ref[...], wv_ref[...], None))

    ba = ba_ref[...]
    beta_all = _sigmoid(ba)
    gg = -jnp.exp(alog_ref[...]) * _softplus(ba + dtb_ref[...])
    row = lax.broadcasted_iota(jnp.int32, (q, q), 0)
    col = lax.broadcasted_iota(jnp.int32, (q, q), 1)
    incl = row >= col
    strict = row > col
    acs = _dot01(incl.astype(BF16), gg)
    acs_t = _tr(acs)

    heads = [(kh, j) for kh in range(hb) for j in range(2)]
    q3 = jnp.stack([qa[:, kh * hd:(kh + 1) * hd] for kh in range(hb)])
    k3 = jnp.stack([ka[:, kh * hd:(kh + 1) * hd] for kh in range(hb)])
    q3 = q3 * lax.rsqrt(jnp.sum(q3 * q3, axis=-1, keepdims=True) + NORM_EPS) * (hd ** -0.5)
    k3 = k3 * lax.rsqrt(jnp.sum(k3 * k3, axis=-1, keepdims=True) + NORM_EPS)
    kk3 = _bdot_nt(k3, k3)
    qk3 = _bdot_nt(q3, k3)
    k_t3 = jnp.stack([_tr(k3[kh]) for kh in range(hb)])

    nh = len(heads)
    a_col = jnp.stack([acs[:, hv + i:hv + i + 1] for i in range(nh)])
    a_row = jnp.stack([acs_t[hv + i:hv + i + 1, :] for i in range(nh)])
    last = jnp.stack([acs[q - 1:q, hv + i:hv + i + 1] for i in range(nh)])
    bcol = jnp.stack([beta_all[:, i:i + 1] for i in range(nh)])
    rep = lambda x: jnp.stack([x[kh] for kh, _ in heads])
    k_v, q_v = rep(k3), rep(q3)
    gam = jnp.exp(jnp.where(incl[None], a_col - a_row, NEG_BIG))
    a_mat = jnp.where(strict[None], rep(kk3) * gam * bcol, 0.0)
    t_mat = _tri_inv(a_mat, row, col)
    v3 = jnp.stack([va[:, i * hd:(i + 1) * hd] for i in range(len(heads))])
    e_col = jnp.exp(a_col)
    rhs = jnp.concatenate([bcol * v3, (bcol * e_col) * k_v], axis=2)
    sol = _bdot(t_mat, rhs)
    s_old = s_scr[...]
    u = sol[:, :, :hd] - _bdot(sol[:, :, hd:], s_old)
    o = _bdot(rep(qk3) * gam, u) + _bdot(q_v, s_old) * e_col
    s_scr[...] = s_old * jnp.exp(last) + _bdot(rep(k_t3), jnp.exp(last - a_col) * u)
    on = _rms(o) * nw_ref[...]
    for i in range(len(heads)):
        hs = slice(i * hd, (i + 1) * hd)
        o_ref[:, hs] = (on[i] * _silu(z_ref[:, hs])).astype(BF16)

    @pl.when(c == pl.num_programs(2) - 1)
    def _():
        st_ref[0] = s_scr[...]


def _dn_prompt(proj, ba, w, batch, seq):
    hk, hd = w["hk"], DN_HEAD
    hb = w["hb"]
    q = _pick(seq, (DN_CHUNK, 32, 16, 8))
    nc = seq // q
    nhb = hk // hb
    tok = lambda b, h, c: b * nc + c
    in_specs = [
        pl.BlockSpec((q, hb * hd), lambda b, h, c: (tok(b, h, c), h)),
        pl.BlockSpec((q, hb * hd), lambda b, h, c: (tok(b, h, c), nhb + h)),
        pl.BlockSpec((q, 2 * hb * hd), lambda b, h, c: (tok(b, h, c), nhb + h)),
        pl.BlockSpec((q, 2 * hb * hd), lambda b, h, c: (tok(b, h, c), 2 * nhb + h)),
        pl.BlockSpec((q, LANES), lambda b, h, c: (tok(b, h, c), h)),
        pl.BlockSpec((CONV_W, hb * hd), lambda b, h, c: (0, h)),
        pl.BlockSpec((CONV_W, hb * hd), lambda b, h, c: (0, nhb + h)),
        pl.BlockSpec((CONV_W, 2 * hb * hd), lambda b, h, c: (0, nhb + h)),
        pl.BlockSpec((1, LANES), lambda b, h, c: (0, h)),
        pl.BlockSpec((1, LANES), lambda b, h, c: (0, h)),
        pl.BlockSpec((1, hd), lambda b, h, c: (0, 0)),
    ]
    o, st = pl.pallas_call(
        functools.partial(_dn_prompt_kernel, hb=hb, hv=2 * hk),
        grid=(batch, nhb, nc),
        in_specs=in_specs,
        out_specs=[pl.BlockSpec((q, 2 * hb * hd), lambda b, h, c: (tok(b, h, c), h)),
                   pl.BlockSpec((1, 2 * hb, hd, hd), lambda b, h, c: (b, h, 0, 0))],
        out_shape=[jax.ShapeDtypeStruct((batch * seq, 2 * hk * hd), BF16),
                   jax.ShapeDtypeStruct((batch, 2 * hk, hd, hd), F32)],
        scratch_shapes=[pltpu.VMEM((2 * hb, hd, hd), F32),
                        pltpu.VMEM((q + SUBLANES, 4 * hb * hd), F32)],
        compiler_params=_cparams(("parallel", "parallel", "arbitrary")),
        name="dn_prompt",
    )(proj, proj, proj, proj, ba, w["conv_w"], w["conv_w"], w["conv_w"], w["a_log"], w["dt_bias"], w["norm"])
    return o, st


def _rotate(x, cos, s1, s2):
    w = x.shape[1]
    return x * cos + pltpu.roll(x, w - 1, 1) * s1 + pltpu.roll(x, 1, 1) * s2


def _ret_prompt_kernel(q_ref, k_ref, v_ref, g_ref, cos_ref, s1_ref, s2_ref, lg_ref, y_ref, st_ref, s_scr, *, hb):
    c = pl.program_id(2)
    q = q_ref.shape[0]

    @pl.when(c == 0)
    def _():
        s_scr[...] = jnp.zeros_like(s_scr)

    tile = lambda t: t if hb == 1 else jnp.concatenate([t] * hb, axis=1)
    cos, s1, s2 = tile(cos_ref[...]), tile(s1_ref[...]), tile(s2_ref[...])
    qa = _rotate(q_ref[...], cos, s1, s2)
    ka = _rotate(k_ref[...], cos, s1, s2) * (RET_DK ** -0.5)
    q3 = jnp.stack([qa[:, RET_DK * h:RET_DK * (h + 1)] for h in range(hb)])
    k3 = jnp.stack([ka[:, RET_DK * h:RET_DK * (h + 1)] for h in range(hb)])
    v3 = jnp.stack([v_ref[:, RET_DV * h:RET_DV * (h + 1)] for h in range(hb)])
    lg = lg_ref[...][:, 0:1, 0:1]
    row = lax.broadcasted_iota(jnp.int32, (q, q), 0)
    col = lax.broadcasted_iota(jnp.int32, (q, q), 1)
    dist = (row - col).astype(F32)[None]
    decay = jnp.exp(jnp.where((row >= col)[None], dist * lg, NEG_BIG))
    pos = lax.broadcasted_iota(jnp.int32, (1, q, 1), 1).astype(F32)
    scores = _bdot_nt(q3, k3)
    s_old = s_scr[...]
    y = _bdot(scores * decay, v3) + _bdot(q3, s_old) * jnp.exp((pos + 1.0) * lg)
    wv = jnp.exp((float(q - 1) - pos) * lg) * v3
    k_t = jnp.stack([jnp.concatenate([_tr(k3[h][:, LANES * i:LANES * (i + 1)]) for i in range(RET_DK // LANES)],
                                     axis=0) for h in range(hb)])
    s_new = s_old * jnp.exp(float(q) * lg) + _bdot(k_t, wv)
    s_scr[...] = s_new
    yn = _rms(y)
    for h in range(hb):
        hs = slice(RET_DV * h, RET_DV * (h + 1))
        y_ref[:, hs] = (yn[h] * _silu(g_ref[:, hs])).astype(BF16)

    @pl.when(c == pl.num_programs(2) - 1)
    def _():
        st_ref[0] = s_new


def _ret_prompt(proj, rope, lg, nheads, batch, seq):
    q = _pick(seq, (RET_CHUNK, 64, 32, 16, 8))
    hb = _pick(nheads, (RET_HEADS_PER_STEP, 2, 1))
    nc = seq // q
    nhb = nheads // hb
    tok = lambda b, h, c: b * nc + c
    v0 = 2 * nheads * RET_DK // (hb * RET_DV)
    g0 = v0 + nhb
    cos, s1, s2 = rope
    in_specs = [
        pl.BlockSpec((q, hb * RET_DK), lambda b, h, c: (tok(b, h, c), h)),
        pl.BlockSpec((q, hb * RET_DK), lambda b, h, c: (tok(b, h, c), nhb + h)),
        pl.BlockSpec((q, hb * RET_DV), lambda b, h, c: (tok(b, h, c), v0 + h)),
        pl.BlockSpec((q, hb * RET_DV), lambda b, h, c: (tok(b, h, c), g0 + h)),
        pl.BlockSpec((q, RET_DK), lambda b, h, c: (c, 0)),
        pl.BlockSpec((q, RET_DK), lambda b, h, c: (c, 0)),
        pl.BlockSpec((q, RET_DK), lambda b, h, c: (c, 0)),
        pl.BlockSpec((hb, SUBLANES, LANES), lambda b, h, c: (h, 0, 0)),
    ]
    y, st = pl.pallas_call(
        functools.partial(_ret_prompt_kernel, hb=hb),
        grid=(batch, nhb, nc),
        in_specs=in_specs,
        out_specs=[pl.BlockSpec((q, hb * RET_DV), lambda b, h, c: (tok(b, h, c), h)),
                   pl.BlockSpec((1, hb, RET_DK, RET_DV), lambda b, h, c: (b, h, 0, 0))],
        out_shape=[jax.ShapeDtypeStruct((batch * seq, nheads * RET_DV), BF16),
                   jax.ShapeDtypeStruct((batch, nheads, RET_DK, RET_DV), F32)],
        scratch_shapes=[pltpu.VMEM((hb, RET_DK, RET_DV), F32)],
        compiler_params=_cparams(("parallel", "parallel", "arbitrary")),
        name="ret_prompt",
    )(proj, proj, proj, proj, cos, s1, s2, lg)
    return y, st


def _alias_args(prev, n_in, out_idx):
    if prev is None:
        return [], [], {}
    return [pl.BlockSpec(memory_space=pl.ANY)], [prev], {n_in: out_idx}


def _conv_step_kernel(*refs, has_bias, aliased):
    refs = list(refs)
    x_ref, cs_ref, w_ref = refs[:3]
    b_ref = refs[3] if has_bias else None
    o_ref, cn_ref = refs[3 + int(has_bias) + int(aliased):]
    w = w_ref[...]
    x = x_ref[...]
    c0, c1, c2 = cs_ref[0, 0], cs_ref[0, 1], cs_ref[0, 2]
    acc = c0 * w[0:1, :] + c1 * w[1:2, :] + c2 * w[2:3, :] + x * w[3:4, :]
    if has_bias:
        acc = acc + b_ref[...]
    o_ref[...] = _silu(acc)
    cn_ref[0, 0] = c1
    cn_ref[0, 1] = c2
    cn_ref[0, 2] = x


def _conv_step(proj, col0, cstate, layer, conv_w, conv_b, prev):
    m = proj.shape[0]
    cdim = cstate.shape[3]
    cb = _pick(cdim, (512, 256, 128))
    x0 = col0 // cb
    taps = CONV_W - 1
    in_specs = [pl.BlockSpec((m, cb), lambda j: (0, x0 + j)),
                pl.BlockSpec((1, taps, m, cb), lambda j: (layer, 0, 0, j)),
                pl.BlockSpec((CONV_W, cb), lambda j: (0, j))]
    args = [proj, cstate, conv_w]
    if conv_b is not None:
        in_specs.append(pl.BlockSpec((1, cb), lambda j: (0, j)))
        args.append(conv_b)
    a_specs, a_args, aliases = _alias_args(prev, len(args), 1)
    return pl.pallas_call(
        functools.partial(_conv_step_kernel, has_bias=conv_b is not None, aliased=prev is not None),
        grid=(cdim // cb,),
        in_specs=in_specs + a_specs,
        out_specs=[pl.BlockSpec((m, cb), lambda j: (0, j)),
                   pl.BlockSpec((1, taps, m, cb), lambda j: (layer, 0, 0, j))],
        out_shape=[jax.ShapeDtypeStruct((m, cdim), F32),
                   jax.ShapeDtypeStruct(cstate.shape, F32)],
        input_output_aliases=aliases,
        compiler_params=_cparams(("parallel",)),
        name="conv_step",
    )(*args, *a_args)


def _ssd_step_kernel(*refs, nheads):
    z_ref, xs_ref, b_ref, c_ref, dt_ref, dtb_ref, alog_ref, d_ref, nw_ref, st_ref = refs[:10]
    y_ref, so_ref, ytb = refs[-3:]
    nb = xs_ref.shape[0]
    hd = SSM_HEAD_DIM
    xs = xs_ref[...]
    dtv = _softplus(dt_ref[...] + dtb_ref[...])
    decay = jnp.exp(dtv * (-jnp.exp(alog_ref[...])))
    pad = jnp.zeros((LANES - nb, LANES), F32)
    kmat = jnp.concatenate([b_ref[...], pad], axis=0).astype(BF16)
    qmat = jnp.concatenate([c_ref[...], pad], axis=0).T.astype(BF16)
    v_all = xs * _expand64(dtv, nheads)
    v_t = [jnp.concatenate([v_all[:, LANES * j:LANES * (j + 1)], pad], axis=0).T
           for j in range(nheads // 2)]
    lane = lax.broadcasted_iota(jnp.int32, (hd, LANES), 1)
    for r in range(nheads):
        vt = v_t[r // 2][hd * (r % 2):hd * (r % 2 + 1), :]
        for bi in range(nb):
            outer = jnp.dot(jnp.where(lane == bi, vt, 0.0).astype(BF16), kmat, preferred_element_type=F32)
            so_ref[0, bi, r] = st_ref[0, bi, r] * decay[bi:bi + 1, r:r + 1] + outer
    for r in range(nheads):
        acc = jnp.zeros((hd, LANES), F32)
        for bi in range(nb):
            yb = jnp.dot(so_ref[0, bi, r].astype(BF16), qmat, preferred_element_type=F32)
            acc = jnp.where(lane == bi, yb, acc)
        ytb[hd * r:hd * (r + 1), :] = acc
    y = [ytb[LANES * j:LANES * (j + 1), :].T[:nb, :] for j in range(nheads // 2)]
    y = y[0] if len(y) == 1 else jnp.concatenate(y, axis=1)
    y = y + xs * _expand64(d_ref[...], nheads)
    y = y * _silu(z_ref[...])
    y_ref[...] = (_rms(y) * nw_ref[...]).astype(BF16)


def _ssd_step(proj, xbc, dt, w, state, layer, prev):
    m = proj.shape[0]
    d_inner, nh, g, n = w["d_inner"], w["hpg"], SSM_GROUPS, SSM_D_STATE
    width = nh * SSM_HEAD_DIM
    nb = _pick(m, (2 * STEP_BATCH, STEP_BATCH))
    b0 = d_inner // n
    st_spec = pl.BlockSpec((1, nb, nh, SSM_HEAD_DIM, n), lambda i, gi: (layer, i, gi, 0, 0))
    in_specs = [
        pl.BlockSpec((nb, width), lambda i, gi: (i, gi)),
        pl.BlockSpec((nb, width), lambda i, gi: (i, gi)),
        pl.BlockSpec((nb, n), lambda i, gi: (i, b0 + gi)),
        pl.BlockSpec((nb, n), lambda i, gi: (i, b0 + g + gi)),
        pl.BlockSpec((nb, LANES), lambda i, gi: (i, gi)),
        pl.BlockSpec((1, LANES), lambda i, gi: (0, gi)),
        pl.BlockSpec((1, LANES), lambda i, gi: (0, gi)),
        pl.BlockSpec((1, LANES), lambda i, gi: (0, gi)),
        pl.BlockSpec((1, width), lambda i, gi: (0, gi)),
        st_spec,
    ]
    args = [proj, xbc, xbc, xbc, dt, w["dt_bias"], w["a_log"], w["d"], w["norm"], state]
    a_specs, a_args, aliases = _alias_args(prev, len(args), 1)
    y, st = pl.pallas_call(
        functools.partial(_ssd_step_kernel, nheads=nh),
        grid=(m // nb, g),
        in_specs=in_specs + a_specs,
        out_specs=[pl.BlockSpec((nb, width), lambda i, gi: (i, gi)), st_spec],
        out_shape=[jax.ShapeDtypeStruct((m, d_inner), BF16),
                   jax.ShapeDtypeStruct(state.shape, F32)],
        scratch_shapes=[pltpu.VMEM((width, LANES), F32)],
        input_output_aliases=aliases,
        compiler_params=_cparams(("parallel", "parallel")),
        name="ssd_step",
    )(*args, *a_args)
    return y, st


def _dn_step_kernel(q_ref, k_ref, v_ref, z_ref, ba_ref, alog_ref, dtb_ref, nw_ref, st_ref,
                    o_ref, so_ref, obuf, *, hv):
    nb = q_ref.shape[0]
    hd = DN_HEAD
    qq = q_ref[...]
    kk = k_ref[...]
    qq = qq * lax.rsqrt(jnp.sum(qq * qq, axis=-1, keepdims=True) + NORM_EPS) * (hd ** -0.5)
    kk = kk * lax.rsqrt(jnp.sum(kk * kk, axis=-1, keepdims=True) + NORM_EPS)
    vv = v_ref[...]
    ba = ba_ref[...]
    beta = _sigmoid(ba)
    eg = jnp.exp(-jnp.exp(alog_ref[...]) * _softplus(ba + dtb_ref[...]))
    k_t = _tr(kk)
    q_t = _tr(qq)
    for bi in range(nb):
        kcol = k_t[:, bi:bi + 1]
        qcol = q_t[:, bi:bi + 1]
        for j in range(2):
            s_old = st_ref[0, bi, j]
            b = beta[bi:bi + 1, j:j + 1]
            e = eg[bi:bi + 1, hv + j:hv + j + 1]
            ks = jnp.sum(kcol * s_old, axis=0, keepdims=True)
            u = b * vv[bi:bi + 1, hd * j:hd * (j + 1)] - (b * e) * ks
            s_new = s_old * e + kcol * u
            so_ref[0, bi, j] = s_new
            obuf[bi:bi + 1, hd * j:hd * (j + 1)] = jnp.sum(qcol * s_new, axis=0, keepdims=True)
    for j in range(2):
        o = obuf[:, hd * j:hd * (j + 1)]
        on = _rms(o) * nw_ref[...]
        o_ref[:, hd * j:hd * (j + 1)] = (on * _silu(z_ref[:, hd * j:hd * (j + 1)])).astype(BF16)


def _dn_step(proj, qkv, ba, w, state, layer):
    m = proj.shape[0]
    hk, hd = w["hk"], DN_HEAD
    nb = _pick(m, (2 * STEP_BATCH, STEP_BATCH))
    st_spec = pl.BlockSpec((1, nb, 2, hd, hd), lambda i, h: (layer, i, h, 0, 0))
    in_specs = [
        pl.BlockSpec((nb, hd), lambda i, h: (i, h)),
        pl.BlockSpec((nb, hd), lambda i, h: (i, hk + h)),
        pl.BlockSpec((nb, 2 * hd), lambda i, h: (i, hk + h)),
        pl.BlockSpec((nb, 2 * hd), lambda i, h: (i, 2 * hk + h)),
        pl.BlockSpec((nb, LANES), lambda i, h: (i, h)),
        pl.BlockSpec((1, LANES), lambda i, h: (0, h)),
        pl.BlockSpec((1, LANES), lambda i, h: (0, h)),
        pl.BlockSpec((1, hd), lambda i, h: (0, 0)),
        st_spec,
    ]
    o, st = pl.pallas_call(
        functools.partial(_dn_step_kernel, hv=2 * hk),
        grid=(m // nb, hk),
        in_specs=in_specs,
        out_specs=[pl.BlockSpec((nb, 2 * hd), lambda i, h: (i, h)), st_spec],
        out_shape=[jax.ShapeDtypeStruct((m, 2 * hk * hd), BF16),
                   jax.ShapeDtypeStruct(state.shape, F32)],
        scratch_shapes=[pltpu.VMEM((nb, 2 * hd), F32)],
        compiler_params=_cparams(("parallel", "parallel")),
        name="dn_step",
    )(qkv, qkv, qkv, proj, ba, w["a_log_step"], w["dt_bias_step"], w["norm"], state)
    return o, st


def _ret_step_kernel(q_ref, k_ref, v_ref, g_ref, cos_ref, s1_ref, s2_ref, lg_ref, st_ref, y_ref, so_ref, ybuf):
    nb = q_ref.shape[0]
    cos, s1, s2 = cos_ref[...], s1_ref[...], s2_ref[...]
    qq = _rotate(q_ref[...], cos, s1, s2)
    kk = _rotate(k_ref[...], cos, s1, s2) * (RET_DK ** -0.5)
    vv = v_ref[...]
    gamma = jnp.exp(lg_ref[0][0:1, 0:1])
    nk = RET_DK // LANES
    k_t = jnp.concatenate([_tr(kk[:, LANES * i:LANES * (i + 1)]) for i in range(nk)], axis=0)
    q_t = jnp.concatenate([_tr(qq[:, LANES * i:LANES * (i + 1)]) for i in range(nk)], axis=0)
    for bi in range(nb):
        s_new = st_ref[0, bi, 0] * gamma + k_t[:, bi:bi + 1] * vv[bi:bi + 1, :]
        so_ref[0, bi, 0] = s_new
        ybuf[bi:bi + 1, :] = jnp.sum(q_t[:, bi:bi + 1] * s_new, axis=0, keepdims=True)
    y_ref[...] = (_rms(ybuf[...]) * _silu(g_ref[...])).astype(BF16)


def _ret_step(proj, rope, lg, nheads, state, layer):
    m = proj.shape[0]
    nb = _pick(m, (STEP_BATCH,))
    v0 = 2 * nheads * RET_DK // RET_DV
    g0 = v0 + nheads
    cos, s1, s2 = rope
    st_spec = pl.BlockSpec((1, nb, 1, RET_DK, RET_DV), lambda i, h: (layer, i, h, 0, 0))
    in_specs = [
        pl.BlockSpec((nb, RET_DK), lambda i, h: (i, h)),
        pl.BlockSpec((nb, RET_DK), lambda i, h: (i, nheads + h)),
        pl.BlockSpec((nb, RET_DV), lambda i, h: (i, v0 + h)),
        pl.BlockSpec((nb, RET_DV), lambda i, h: (i, g0 + h)),
        pl.BlockSpec((1, RET_DK), lambda i, h: (0, 0)),
        pl.BlockSpec((1, RET_DK), lambda i, h: (0, 0)),
        pl.BlockSpec((1, RET_DK), lambda i, h: (0, 0)),
        pl.BlockSpec((1, SUBLANES, LANES), lambda i, h: (h, 0, 0)),
        st_spec,
    ]
    y, st = pl.pallas_call(
        _ret_step_kernel,
        grid=(m // nb, nheads),
        in_specs=in_specs,
        out_specs=[pl.BlockSpec((nb, RET_DV), lambda i, h: (i, h)), st_spec],
        out_shape=[jax.ShapeDtypeStruct((m, nheads * RET_DV), BF16),
                   jax.ShapeDtypeStruct(state.shape, F32)],
        scratch_shapes=[pltpu.VMEM((nb, RET_DV), F32)],
        compiler_params=_cparams(("parallel", "parallel")),
        name="ret_step",
    )(proj, proj, proj, proj, cos, s1, s2, lg, state)
    return y, st


def _pad_lanes(x):
    return jnp.pad(x, [(0, 0)] * (x.ndim - 1) + [(0, LANES - x.shape[-1])])


def _prep_ssm(conv_w, conv_b, dt_bias, a_log, d_skip, norm_w):
    heads = dt_bias.shape[0]
    g = SSM_GROUPS
    hpg = heads // g
    d_inner = heads * SSM_HEAD_DIM
    conv_dim = conv_w.shape[1]
    main = d_inner + conv_dim
    per_group = lambda v: _pad_lanes(v.reshape(g, hpg)).reshape(1, g * LANES)
    return dict(d_inner=d_inner, hpg=hpg, conv_dim=conv_dim, main=main,
                conv_w=conv_w, conv_b=conv_b.reshape(1, conv_dim),
                dt_bias=per_group(dt_bias), a_log=per_group(a_log), d=per_group(d_skip),
                norm=norm_w.reshape(1, d_inner))


def _prep_dn(conv_w, a_log, dt_bias, norm_w):
    hv = a_log.shape[0]
    hk = hv // 2
    hb = _pick(hk, (DN_HEADS_PER_STEP, 2, 1))
    conv_dim = conv_w.shape[1]
    main = conv_dim + hv * DN_HEAD

    def table(v, heads_per_step):
        t = v.reshape(hk // heads_per_step, 2 * heads_per_step)
        t = jnp.pad(t, ((0, 0), (hv, LANES - hv - 2 * heads_per_step)))
        return t.reshape(1, -1)

    return dict(hk=hk, hb=hb, conv_dim=conv_dim, main=main, conv_w=conv_w,
                a_log=table(a_log, hb), dt_bias=table(dt_bias, hb),
                a_log_step=table(a_log, 1), dt_bias_step=table(dt_bias, 1),
                norm=norm_w.reshape(1, DN_HEAD))


def _rope_tables(pos):
    half = RET_DK // 2
    inv = 1.0 / (RET_ROPE_BASE ** jnp.linspace(0.0, 1.0, half, dtype=F32))
    ang = pos.astype(F32)[:, None] * inv[None, :]
    cos, sin, zero = jnp.cos(ang), jnp.sin(ang), jnp.zeros_like(ang)
    inter = lambda a, b: jnp.stack([a, b], axis=-1).reshape(pos.shape[0], RET_DK)
    return inter(cos, cos), inter(-sin, zero), inter(zero, sin)


def _forward(xp, xs, pp, ps, states, prm, wts):
    bp, lp, d_model = xp.shape
    bs = xs.shape[0]
    mp, ms = bp * lp, bs
    hp, hs = xp.reshape(mp, d_model), xs.reshape(ms, d_model)
    pp = pp.reshape(pp.shape[0], mp, pp.shape[-1])
    ps = ps.reshape(ps.shape[0], ms, ps.shape[-1])
    depth = prm["norm_mix_pre"].shape[0]
    ssm_s, ssm_c, dn_s, dn_c, ret_s = states
    o_ssm = o_ssm_c = o_dn = o_dn_c = o_ret = None
    p_ssm, p_ssm_c, p_dn, p_dn_c, p_ret = [], [], [], [], []
    tail = slice(lp - (CONV_W - 1), lp)
    xnp, xns = _norm(hp, hs, prm["norm_mix_pre"][0])
    for i in range(depth):
        kind, j = i % 3, i // 3
        if kind == 0:
            w = wts["ssm"][j]
            d_inner, conv_dim = w["d_inner"], w["conv_dim"]
            proj_p, proj_s = _mm(xnp, xns, wts["ssm_in_t"], j, w["main"], True)
            blocks = (SSM_GROUPS, w["hpg"])
            dt_p, dt_s = _mm_small(xnp, xns, wts["ssm_in_t"], j, w["main"], SSM_GROUPS * w["hpg"], blocks, blocks)
            yp, st = _ssd_prompt(proj_p, dt_p, w, bp, lp)
            p_ssm.append(st)
            p_ssm_c.append(proj_p.reshape(bp, lp, -1)[:, tail, d_inner:d_inner + conv_dim])
            xbc, o_ssm_c = _conv_step(proj_s, d_inner, ssm_c, j, w["conv_w"], w["conv_b"], o_ssm_c)
            ys, o_ssm = _ssd_step(proj_s, xbc, dt_s, w, ssm_s, j, o_ssm)
            w_out = wts["ssm_out"]
        elif kind == 1:
            w = wts["dn"][j]
            conv_dim, hk, hb = w["conv_dim"], w["hk"], w["hb"]
            proj_p, proj_s = _mm(xnp, xns, wts["dn_in_t"], j, w["main"], True)
            ba_p, ba_s = _mm_small(xnp, xns, wts["dn_in_t"], j, w["main"], 4 * hk, (hk // hb, 2 * hb), (hk, 2))
            yp, st = _dn_prompt(proj_p, ba_p, w, bp, lp)
            p_dn.append(st)
            p_dn_c.append(proj_p.reshape(bp, lp, -1)[:, tail, :conv_dim])
            qkv, o_dn_c = _conv_step(proj_s, 0, dn_c, j, w["conv_w"], None, o_dn_c)
            ys, o_dn = _dn_step(proj_s, qkv, ba_s, w, dn_s, j)
            w_out = wts["dn_out"]
        else:
            nheads = wts["ret_heads"]
            proj_p, proj_s = _mm(xnp, xns, wts["ret_in"], j, wts["ret_in"].shape[2], False)
            yp, st = _ret_prompt(proj_p, wts["rope_p"], wts["lg"], nheads, bp, lp)
            p_ret.append(st)
            ys, o_ret = _ret_step(proj_s, wts["rope_s"], wts["lg"], nheads, ret_s, j)
            w_out = wts["ret_out"]
        hp, xnp = _mm_out(yp, w_out, j, hp, prm["norm_mix_post"][i], prm["norm_ffn_pre"][i])
        hs, xns = _mm_out(ys, w_out, j, hs, prm["norm_mix_post"][i], prm["norm_ffn_pre"][i])
        ap, as_ = _ffn_in(xnp, xns, wts["ffn_gate"], wts["ffn_up"], i)
        hp, xnp = _mm_out(ap, wts["ffn_down"], i, hp, prm["norm_ffn_post"][i], prm["norm_ple"][i])
        hs, xns = _mm_out(as_, wts["ffn_down"], i, hs, prm["norm_ffn_post"][i], prm["norm_ple"][i])
        gain_next = prm["norm_mix_pre"][i + 1] if i + 1 < depth else None
        hp, hs, xnp, xns = _ple(hp, hs, xnp, xns, wts["ple_gate"], pp, ps, wts["ple_proj"], i, gain_next)
    out_p = (hp.reshape(bp, lp, d_model), jnp.swapaxes(jnp.stack(p_ssm), -1, -2), jnp.stack(p_ssm_c),
             jnp.stack(p_dn), jnp.stack(p_dn_c), jnp.stack(p_ret))
    out_s = (hs.reshape(bs, 1, d_model), jnp.swapaxes(o_ssm, -1, -2), jnp.swapaxes(o_ssm_c, 1, 2), o_dn,
             jnp.swapaxes(o_dn_c, 1, 2), o_ret)
    return out_p, out_s


def kernel(x_prompt, x_sample, state_ssm, state_ssm_conv, state_delta, state_delta_conv, state_ret, p_prompt, p_sample, norm_mix_pre, norm_mix_post, norm_ffn_pre, norm_ffn_post, norm_ple, ffn_w_gate, ffn_w_up, ffn_w_down, ple_w_proj, ple_w_gate, ssm_w_in, ssm_conv_w, ssm_conv_b, ssm_dt_bias, ssm_a_log, ssm_d, ssm_norm, ssm_w_out, dn_w_in, dn_conv_w, dn_a_log, dn_dt_bias, dn_norm, dn_w_out, ret_w_in, ret_w_out):
    assert x_sample.shape[1] == 1, "the sample group advances one token per sequence"
    prm = dict(norm_mix_pre=norm_mix_pre, norm_mix_post=norm_mix_post, norm_ffn_pre=norm_ffn_pre,
               norm_ffn_post=norm_ffn_post, norm_ple=norm_ple)
    ret_heads = ret_w_out.shape[1] // RET_DV
    log_gamma = jnp.log(1.0 - 2.0 ** (-5.0 - jnp.arange(ret_heads, dtype=F32)))
    ssm_in_t, dn_in_t = jnp.swapaxes(ssm_w_in, 1, 2), jnp.swapaxes(dn_w_in, 1, 2)
    base = dict(
        ssm=[_prep_ssm(ssm_conv_w[j], ssm_conv_b[j], ssm_dt_bias[j], ssm_a_log[j], ssm_d[j],
                       ssm_norm[j]) for j in range(ssm_w_in.shape[0])],
        dn=[_prep_dn(dn_conv_w[j], dn_a_log[j], dn_dt_bias[j], dn_norm[j])
            for j in range(dn_w_in.shape[0])],
        ssm_in_t=ssm_in_t, dn_in_t=dn_in_t,
        ssm_out=ssm_w_out.astype(BF16), dn_out=dn_w_out.astype(BF16),
        ret_in=ret_w_in, ret_out=ret_w_out.astype(BF16), ret_heads=ret_heads,
        ffn_gate=ffn_w_gate, ffn_up=ffn_w_up, ffn_down=ffn_w_down.astype(BF16),
        ple_gate=ple_w_gate.astype(BF16), ple_proj=ple_w_proj.astype(BF16),
        lg=jnp.broadcast_to(log_gamma[:, None, None], (ret_heads, SUBLANES, LANES)),
    )
    seq = x_prompt.shape[1]
    pos_prompt = jnp.arange(seq, dtype=jnp.int32)
    pos_sample = PAST_LEN + jnp.arange(1, dtype=jnp.int32)
    states = (jnp.swapaxes(state_ssm, -1, -2), jnp.swapaxes(state_ssm_conv, 1, 2), state_delta,
              jnp.swapaxes(state_delta_conv, 1, 2), state_ret)
    out_p, out_s = _forward(x_prompt, x_sample, p_prompt, p_sample, states, prm,
                            dict(base, rope_p=_rope_tables(pos_prompt), rope_s=_rope_tables(pos_sample)))
    return (out_p[0], out_s[0]) + out_p[1:] + out_s[1:]
```

```python
import functools

import jax
import jax.numpy as jnp
from jax import lax
from jax.experimental import pallas as pl
from jax.experimental.pallas import tpu as pltpu

F32 = jnp.float32
BF16 = jnp.bfloat16

NORM_EPS = 1e-6
PAST_LEN = 16384
CONV_W = 4
SSM_GROUPS = 8
SSM_HEAD_DIM = 64
SSM_D_STATE = 128
DN_HEAD = 128
RET_DK = 256
RET_DV = 512
RET_ROPE_BASE = 10000.0

LANES = 128
SUBLANES = 8
NEG_BIG = -1e30
SSD_CHUNK = 128
RET_CHUNK = 128
DN_CHUNK = 64
DN_HEADS_PER_STEP = 16
RET_HEADS_PER_STEP = 8
STEP_BATCH = 8
VMEM_LIMIT = 52 * 1024 * 1024


def _pick(n, prefs):
    for p in prefs:
        if n % p == 0:
            return p
    return n


def _cparams(sem):
    return pltpu.CompilerParams(dimension_semantics=sem, vmem_limit_bytes=VMEM_LIMIT)


def _sigmoid(x):
    return 1.0 / (1.0 + jnp.exp(-x))


def _silu(x):
    h = 0.5 * x
    return h + h * jnp.tanh(h)


def _softplus(x):
    return jnp.maximum(x, 0.0) + jnp.log1p(jnp.exp(-jnp.abs(x)))


def _dot(a, b):
    return jnp.dot(a.astype(BF16), b.astype(BF16), preferred_element_type=F32)


def _dot_nt(a, b):
    return lax.dot_general(a.astype(BF16), b.astype(BF16), (((1,), (1,)), ((), ())),
                           preferred_element_type=F32)


def _dot01(m01, x):
    hi = x.astype(BF16)
    r = x - hi.astype(F32)
    mid = r.astype(BF16)
    lo = (r - mid.astype(F32)).astype(BF16)
    out = jnp.dot(m01, hi, preferred_element_type=F32)
    out = out + jnp.dot(m01, mid, preferred_element_type=F32)
    return out + jnp.dot(m01, lo, preferred_element_type=F32)


def _tr(x):
    r, c = x.shape
    assert c == LANES and r <= LANES
    if r < LANES:
        x = jnp.concatenate([x, jnp.zeros((LANES - r, c), x.dtype)], axis=0)
    return x.T[:, :r]


def _expand64(x, nheads):
    rows = x.shape[0]
    lane = lax.broadcasted_iota(jnp.int32, (rows, LANES), 1)
    parts = []
    for j in range(nheads // 2):
        a = jnp.broadcast_to(x[:, 2 * j:2 * j + 1], (rows, LANES))
        b = jnp.broadcast_to(x[:, 2 * j + 1:2 * j + 2], (rows, LANES))
        parts.append(jnp.where(lane < 64, a, b))
    return parts[0] if len(parts) == 1 else jnp.concatenate(parts, axis=1)


def _rms(y):
    return y * lax.rsqrt(jnp.mean(y * y, axis=-1, keepdims=True) + NORM_EPS)


def _conv_chunk(buf_ref, col0, x, w, bias):
    q, width = x.shape
    cols = slice(col0, col0 + width)
    buf_ref[SUBLANES:SUBLANES + q, cols] = x
    acc = x * w[CONV_W - 1:CONV_W, :]
    for s in range(1, CONV_W):
        acc = acc + buf_ref[SUBLANES - s:SUBLANES - s + q, cols] * w[CONV_W - 1 - s:CONV_W - s, :]
    buf_ref[0:SUBLANES, cols] = x[q - SUBLANES:q, :]
    if bias is not None:
        acc = acc + bias
    return acc


def _bdot(a, b):
    return lax.dot_general(a.astype(BF16), b.astype(BF16), (((2,), (1,)), ((0,), (0,))),
                           preferred_element_type=F32)


def _bdot_nt(a, b):
    return lax.dot_general(a.astype(BF16), b.astype(BF16), (((2,), (2,)), ((0,), (0,))),
                           preferred_element_type=F32)


def _tri_inv(a, row, col):
    q = a.shape[-1]
    eye = (row == col).astype(F32)[None]
    d = eye - jnp.where((jnp.right_shift(row, 1) == jnp.right_shift(col, 1))[None], a, 0.0)
    sh = 1
    while (1 << sh) < q:
        same_big = jnp.right_shift(row, sh + 1) == jnp.right_shift(col, sh + 1)
        diff_small = jnp.right_shift(row, sh) != jnp.right_shift(col, sh)
        lb = jnp.where((same_big & diff_small)[None], a, 0.0)
        d = d - _bdot(d, _bdot(lb, d))
        sh += 1
    return d


def _norm_kernel(x_ref, g_ref, o_ref):
    o_ref[...] = (_rms(x_ref[...]) * g_ref[...]).astype(BF16)


def _norm(x, gain):
    m, k = x.shape
    tm = _pick(m, (512, 256, 128))
    return pl.pallas_call(
        _norm_kernel,
        grid=(m // tm,),
        in_specs=[pl.BlockSpec((tm, k), lambda i: (i, 0)),
                  pl.BlockSpec((1, k), lambda i: (0, 0))],
        out_specs=pl.BlockSpec((tm, k), lambda i: (i, 0)),
        out_shape=jax.ShapeDtypeStruct((m, k), BF16),
        compiler_params=_cparams(("parallel",)),
        name="norm",
    )(x, gain.reshape(1, k))


def _mm_kernel(x_ref, w_ref, o_ref, wb_ref, *, transposed):
    @pl.when(pl.program_id(1) == 0)
    def _():
        wb_ref[...] = w_ref[0].astype(BF16)

    if transposed:
        o_ref[...] = lax.dot_general(x_ref[...], wb_ref[...], (((1,), (1,)), ((), ())),
                                     preferred_element_type=F32)
    else:
        o_ref[...] = jnp.dot(x_ref[...], wb_ref[...], preferred_element_type=F32)


def _mm(xn, w, layer, n, transposed):
    m, k = xn.shape
    tm = _pick(m, (1024, 512, 256, 128))
    tn = _pick(n, (1024, 512, 256, 128))
    if transposed:
        w_spec = pl.BlockSpec((1, tn, k), lambda j, i: (layer, j, 0))
        w_scratch = pltpu.VMEM((tn, k), BF16)
    else:
        w_spec = pl.BlockSpec((1, k, tn), lambda j, i: (layer, 0, j))
        w_scratch = pltpu.VMEM((k, tn), BF16)
    return pl.pallas_call(
        functools.partial(_mm_kernel, transposed=transposed),
        grid=(n // tn, m // tm),
        in_specs=[pl.BlockSpec((tm, k), lambda j, i: (i, 0)), w_spec],
        out_specs=pl.BlockSpec((tm, tn), lambda j, i: (i, j)),
        out_shape=jax.ShapeDtypeStruct((m, n), F32),
        scratch_shapes=[w_scratch],
        compiler_params=_cparams(("parallel", "arbitrary")),
        name="mm",
    )(xn, w)


def _mm_small_kernel(x_ref, w_ref, o_ref, *, nblk, width):
    w = w_ref[0]
    rows, k = w.shape
    if rows < LANES:
        w = jnp.concatenate([w, jnp.zeros((LANES - rows, k), w.dtype)], axis=0)
    y = _dot_nt(x_ref[...], w)
    for b in range(nblk):
        shift = (LANES - b * width) % LANES
        o_ref[:, LANES * b:LANES * (b + 1)] = y if shift == 0 else pltpu.roll(y, shift, 1)


def _mm_small(xn, w_t, layer, row0, rows, nblk, width):
    m, k = xn.shape
    tm = _pick(m, (1024, 512, 256, 128))
    assert rows <= LANES and row0 % rows == 0 and rows % SUBLANES == 0
    return pl.pallas_call(
        functools.partial(_mm_small_kernel, nblk=nblk, width=width),
        grid=(m // tm,),
        in_specs=[pl.BlockSpec((tm, k), lambda i: (i, 0)),
                  pl.BlockSpec((1, rows, k), lambda i: (layer, row0 // rows, 0))],
        out_specs=pl.BlockSpec((tm, nblk * LANES), lambda i: (i, 0)),
        out_shape=jax.ShapeDtypeStruct((m, nblk * LANES), F32),
        compiler_params=_cparams(("parallel",)),
        name="mm_small",
    )(xn, w_t)


def _mm_out_kernel(a_ref, w_ref, h_ref, g_ref, gn_ref, o_ref, xn_ref):
    y = jnp.dot(a_ref[...], w_ref[0], preferred_element_type=F32)
    h = h_ref[...] + _rms(y) * g_ref[...]
    o_ref[...] = h
    xn_ref[...] = (_rms(h) * gn_ref[...]).astype(BF16)


def _mm_out(a, w, layer, h, gain, gain_next):
    m, k = a.shape
    n = w.shape[2]
    tm = _pick(m, (256, 128))
    return pl.pallas_call(
        _mm_out_kernel,
        grid=(m // tm,),
        in_specs=[pl.BlockSpec((tm, k), lambda i: (i, 0)),
                  pl.BlockSpec((1, k, n), lambda i: (layer, 0, 0), pipeline_mode=pl.Buffered(1)),
                  pl.BlockSpec((tm, n), lambda i: (i, 0)),
                  pl.BlockSpec((1, n), lambda i: (0, 0)),
                  pl.BlockSpec((1, n), lambda i: (0, 0))],
        out_specs=[pl.BlockSpec((tm, n), lambda i: (i, 0)),
                   pl.BlockSpec((tm, n), lambda i: (i, 0))],
        out_shape=[jax.ShapeDtypeStruct((m, n), F32),
                   jax.ShapeDtypeStruct((m, n), BF16)],
        compiler_params=_cparams(("parallel",)),
        name="mm_out",
    )(a, w, h, gain.reshape(1, n), gain_next.reshape(1, n))


def _ffn_in_kernel(x_ref, wg_ref, wu_ref, o_ref, wgb_ref, wub_ref):
    @pl.when(pl.program_id(1) == 0)
    def _():
        wgb_ref[...] = wg_ref[0].astype(BF16)
        wub_ref[...] = wu_ref[0].astype(BF16)

    xn = x_ref[...]
    gate = jnp.dot(xn, wgb_ref[...], preferred_element_type=F32)
    up = jnp.dot(xn, wub_ref[...], preferred_element_type=F32)
    o_ref[...] = (_silu(gate) * up).astype(BF16)


def _ffn_in(xn, wg, wu, layer):
    m, k = xn.shape
    n = wg.shape[2]
    tm = _pick(m, (1024, 512, 256, 128))
    tn = _pick(n, (512, 256, 128))
    return pl.pallas_call(
        _ffn_in_kernel,
        grid=(n // tn, m // tm),
        in_specs=[pl.BlockSpec((tm, k), lambda j, i: (i, 0)),
                  pl.BlockSpec((1, k, tn), lambda j, i: (layer, 0, j)),
                  pl.BlockSpec((1, k, tn), lambda j, i: (layer, 0, j))],
        out_specs=pl.BlockSpec((tm, tn), lambda j, i: (i, j)),
        out_shape=jax.ShapeDtypeStruct((m, n), BF16),
        scratch_shapes=[pltpu.VMEM((k, tn), BF16), pltpu.VMEM((k, tn), BF16)],
        compiler_params=_cparams(("parallel", "arbitrary")),
        name="ffn_in",
    )(xn, wg, wu)


def _ple_kernel(*refs, has_next):
    h_ref, xn_ref, wg_ref, p_ref, wp_ref = refs[:5]
    gate = _sigmoid(jnp.dot(xn_ref[...], wg_ref[0], preferred_element_type=F32))
    proj = jnp.dot(p_ref[0].astype(BF16), wp_ref[0], preferred_element_type=F32)
    h = h_ref[...] + proj * gate
    if has_next:
        gn_ref, o_ref, on_ref = refs[5:]
        on_ref[...] = (_rms(h) * gn_ref[...]).astype(BF16)
    else:
        o_ref, = refs[5:]
    o_ref[...] = h


def _ple(h, xn, wg, p, wp, layer, gain_next):
    m, k = h.shape
    n = wg.shape[2]
    pd = p.shape[2]
    tm = _pick(m, (512, 256, 128))
    has_next = gain_next is not None
    row = lambda i: (i, 0)
    in_specs = [pl.BlockSpec((tm, k), row),
                pl.BlockSpec((tm, k), row),
                pl.BlockSpec((1, k, n), lambda i: (layer, 0, 0), pipeline_mode=pl.Buffered(1)),
                pl.BlockSpec((1, tm, pd), lambda i: (layer, i, 0)),
                pl.BlockSpec((1, pd, n), lambda i: (layer, 0, 0), pipeline_mode=pl.Buffered(1))]
    args = [h, xn, wg, p, wp]
    out_specs = [pl.BlockSpec((tm, n), row)]
    out_shape = [jax.ShapeDtypeStruct((m, n), F32)]
    if has_next:
        in_specs.append(pl.BlockSpec((1, n), lambda i: (0, 0)))
        args.append(gain_next.reshape(1, n))
        out_specs.append(pl.BlockSpec((tm, n), row))
        out_shape.append(jax.ShapeDtypeStruct((m, n), BF16))
    out = pl.pallas_call(
        functools.partial(_ple_kernel, has_next=has_next),
        grid=(m // tm,),
        in_specs=in_specs,
        out_specs=out_specs,
        out_shape=out_shape,
        compiler_params=_cparams(("parallel",)),
        name="ple",
    )(*args)
    return (out[0], out[1]) if has_next else (out[0], None)


def _ssd_prompt_kernel(z_ref, xs_ref, b_ref, c_ref, dt_ref, wx_ref, wb_ref, wc_ref, bx_ref, bb_ref, bc_ref,
                       dtb_ref, alog_ref, d_ref, nw_ref, y_ref, st_ref, s_scr, cbuf, *, nheads):
    c = pl.program_id(2)
    q, width = xs_ref.shape
    n = b_ref.shape[1]
    hd = SSM_HEAD_DIM

    @pl.when(c == 0)
    def _():
        s_scr[...] = jnp.zeros_like(s_scr)
        cbuf[0:SUBLANES, :] = jnp.zeros((SUBLANES, cbuf.shape[1]), F32)

    xs = _silu(_conv_chunk(cbuf, 0, xs_ref[...], wx_ref[...], bx_ref[...]))
    bm = _silu(_conv_chunk(cbuf, width, b_ref[...], wb_ref[...], bb_ref[...]))
    cm = _silu(_conv_chunk(cbuf, width + n, c_ref[...], wc_ref[...], bc_ref[...]))

    dtv = _softplus(dt_ref[...] + dtb_ref[...])
    la = dtv * (-jnp.exp(alog_ref[...]))
    row = lax.broadcasted_iota(jnp.int32, (q, q), 0)
    col = lax.broadcasted_iota(jnp.int32, (q, q), 1)
    causal = row >= col
    acs = _dot01(causal.astype(BF16), la)
    acs_t = _tr(acs)
    scores = _dot_nt(cm, bm)
    dt_t = _tr(dtv)
    s_old = s_scr[...]
    y = _dot_nt(cm, s_old) * _expand64(jnp.exp(acs), nheads)
    lane = lax.broadcasted_iota(jnp.int32, (q, LANES), 1)
    parts = []
    for j in range(nheads // 2):
        xp = xs[:, LANES * j:LANES * (j + 1)].astype(BF16)
        ys = []
        for t in range(2):
            r = 2 * j + t
            seg = acs[:, r:r + 1] - acs_t[r:r + 1, :]
            decay = jnp.exp(jnp.where(causal, seg, NEG_BIG))
            ys.append(jnp.dot((scores * decay * dt_t[r:r + 1, :]).astype(BF16), xp, preferred_element_type=F32))
        parts.append(jnp.where(lane < 64, ys[0], ys[1]))
    y = y + (parts[0] if len(parts) == 1 else jnp.concatenate(parts, axis=1))
    y = y + xs * _expand64(d_ref[...], nheads)

    last = acs[q - 1:q, :]
    wsc_t = _tr(jnp.exp(last - acs) * dtv)
    sub = lax.broadcasted_iota(jnp.int32, (LANES, q), 0)
    wv_t = []
    for j in range(width // LANES):
        scale = jnp.where(sub < hd, wsc_t[2 * j:2 * j + 1, :], wsc_t[2 * j + 1:2 * j + 2, :])
        wv_t.append(_tr(xs[:, LANES * j:LANES * (j + 1)]) * scale)
    wv_t = wv_t[0] if len(wv_t) == 1 else jnp.concatenate(wv_t, axis=0)
    e_last = jnp.exp(last)
    e_rows = jnp.concatenate([jnp.broadcast_to(e_last[:, r:r + 1], (hd, n)) for r in range(nheads)], axis=0)
    s_new = s_old * e_rows + _dot(wv_t, bm)
    s_scr[...] = s_new

    y = y * _silu(z_ref[...])
    y_ref[...] = (_rms(y) * nw_ref[...]).astype(BF16)

    @pl.when(c == pl.num_programs(2) - 1)
    def _():
        for r in range(nheads):
            st_ref[0, r] = s_new[hd * r:hd * (r + 1), :]


def _ssd_prompt(proj, dt, w, batch, seq):
    d_inner, nh, g, n = w["d_inner"], w["hpg"], SSM_GROUPS, SSM_D_STATE
    width = nh * SSM_HEAD_DIM
    q = _pick(seq, (SSD_CHUNK, 64, 32, 16, 8))
    nc = seq // q
    xs0, b0, c0 = d_inner // width, 2 * d_inner // n, 2 * d_inner // n + g
    cw0 = d_inner // n
    tok = lambda b, gi, c: b * nc + c
    in_specs = [
        pl.BlockSpec((q, width), lambda b, gi, c: (tok(b, gi, c), gi)),
        pl.BlockSpec((q, width), lambda b, gi, c: (tok(b, gi, c), xs0 + gi)),
        pl.BlockSpec((q, n), lambda b, gi, c: (tok(b, gi, c), b0 + gi)),
        pl.BlockSpec((q, n), lambda b, gi, c: (tok(b, gi, c), c0 + gi)),
        pl.BlockSpec((q, LANES), lambda b, gi, c: (tok(b, gi, c), gi)),
        pl.BlockSpec((CONV_W, width), lambda b, gi, c: (0, gi)),
        pl.BlockSpec((CONV_W, n), lambda b, gi, c: (0, cw0 + gi)),
        pl.BlockSpec((CONV_W, n), lambda b, gi, c: (0, cw0 + g + gi)),
        pl.BlockSpec((1, width), lambda b, gi, c: (0, gi)),
        pl.BlockSpec((1, n), lambda b, gi, c: (0, cw0 + gi)),
        pl.BlockSpec((1, n), lambda b, gi, c: (0, cw0 + g + gi)),
        pl.BlockSpec((1, LANES), lambda b, gi, c: (0, gi)),
        pl.BlockSpec((1, LANES), lambda b, gi, c: (0, gi)),
        pl.BlockSpec((1, LANES), lambda b, gi, c: (0, gi)),
        pl.BlockSpec((1, width), lambda b, gi, c: (0, gi)),
    ]
    y, st = pl.pallas_call(
        functools.partial(_ssd_prompt_kernel, nheads=nh),
        grid=(batch, g, nc),
        in_specs=in_specs,
        out_specs=[pl.BlockSpec((q, width), lambda b, gi, c: (tok(b, gi, c), gi)),
                   pl.BlockSpec((1, nh, SSM_HEAD_DIM, n), lambda b, gi, c: (b, gi, 0, 0))],
        out_shape=[jax.ShapeDtypeStruct((batch * seq, d_inner), BF16),
                   jax.ShapeDtypeStruct((batch, g * nh, SSM_HEAD_DIM, n), F32)],
        scratch_shapes=[pltpu.VMEM((width, n), F32),
                        pltpu.VMEM((q + SUBLANES, width + 2 * n), F32)],
        compiler_params=_cparams(("parallel", "parallel", "arbitrary")),
        name="ssd_prompt",
    )(proj, proj, proj, proj, dt, w["conv_w"], w["conv_w"], w["conv_w"], w["conv_b"], w["conv_b"], w["conv_b"],
      w["dt_bias"], w["a_log"], w["d"], w["norm"])
    return y, st


def _dn_prompt_kernel(q_ref, k_ref, v_ref, z_ref, ba_ref, wq_ref, wk_ref, wv_ref, alog_ref, dtb_ref, nw_ref,
                      o_ref, st_ref, s_scr, cbuf, *, hb, hv):
    c = pl.program_id(2)
    q = q_ref.shape[0]
    hd = DN_HEAD

    @pl.when(c == 0)
    def _():
        s_scr[...] = jnp.zeros_like(s_scr)
        cbuf[0:SUBLANES, :] = jnp.zeros((SUBLANES, cbuf.shape[1]), F32)

    qa = _silu(_conv_chunk(cbuf, 0, q_ref[...], wq_ref[...], None))
    ka = _silu(_conv_chunk(cbuf, hb * hd, k_ref[...], wk_ref[...], None))
    va = _silu(_conv_chunk(cbuf, 2 * hb * hd, v_ref[...], wv_ref[...], None))

    ba = ba_ref[...]
    beta_all = _sigmoid(ba)
    gg = -jnp.exp(alog_ref[...]) * _softplus(ba + dtb_ref[...])
    row = lax.broadcasted_iota(jnp.int32, (q, q), 0)
    col = lax.broadcasted_iota(jnp.int32, (q, q), 1)
    incl = row >= col
    strict = row > col
    acs = _dot01(incl.astype(BF16), gg)
    acs_t = _tr(acs)

    heads = [(kh, j) for kh in range(hb) for j in range(2)]
    q3 = jnp.stack([qa[:, kh * hd:(kh + 1) * hd] for kh in range(hb)])
    k3 = jnp.stack([ka[:, kh * hd:(kh + 1) * hd] for kh in range(hb)])
    q3 = q3 * lax.rsqrt(jnp.sum(q3 * q3, axis=-1, keepdims=True) + NORM_EPS) * (hd ** -0.5)
    k3 = k3 * lax.rsqrt(jnp.sum(k3 * k3, axis=-1, keepdims=True) + NORM_EPS)
    kk3 = _bdot_nt(k3, k3)
    qk3 = _bdot_nt(q3, k3)
    k_t3 = jnp.stack([_tr(k3[kh]) for kh in range(hb)])

    nh = len(heads)
    a_col = jnp.stack([acs[:, hv + i:hv + i + 1] for i in range(nh)])
    a_row = jnp.stack([acs_t[hv + i:hv + i + 1, :] for i in range(nh)])
    last = jnp.stack([acs[q - 1:q, hv + i:hv + i + 1] for i in range(nh)])
    bcol = jnp.stack([beta_all[:, i:i + 1] for i in range(nh)])
    rep = lambda x: jnp.stack([x[kh] for kh, _ in heads])
    k_v, q_v = rep(k3), rep(q3)
    gam = jnp.exp(jnp.where(incl[None], a_col - a_row, NEG_BIG))
    a_mat = jnp.where(strict[None], rep(kk3) * gam * bcol, 0.0)
    t_mat = _tri_inv(a_mat, row, col)
    v3 = jnp.stack([va[:, i * hd:(i + 1) * hd] for i in range(len(heads))])
    e_col = jnp.exp(a_col)
    rhs = jnp.concatenate([bcol * v3, (bcol * e_col) * k_v], axis=2)
    sol = _bdot(t_mat, rhs)
    s_old = s_scr[...]
    u = sol[:, :, :hd] - _bdot(sol[:, :, hd:], s_old)
    o = _bdot(rep(qk3) * gam, u) + _bdot(q_v, s_old) * e_col
    s_scr[...] = s_old * jnp.exp(last) + _bdot(rep(k_t3), jnp.exp(last - a_col) * u)
    on = _rms(o) * nw_ref[...]
    for i in range(len(heads)):
        hs = slice(i * hd, (i + 1) * hd)
        o_ref[:, hs] = (on[i] * _silu(z_ref[:, hs])).astype(BF16)

    @pl.when(c == pl.num_programs(2) - 1)
    def _():
        st_ref[0] = s_scr[...]


def _dn_prompt(proj, ba, w, batch, seq):
    hk, hd = w["hk"], DN_HEAD
    hb = w["hb"]
    q = _pick(seq, (DN_CHUNK, 32, 16, 8))
    nc = seq // q
    nhb = hk // hb
    tok = lambda b, h, c: b * nc + c
    in_specs = [
        pl.BlockSpec((q, hb * hd), lambda b, h, c: (tok(b, h, c), h)),
        pl.BlockSpec((q, hb * hd), lambda b, h, c: (tok(b, h, c), nhb + h)),
        pl.BlockSpec((q, 2 * hb * hd), lambda b, h, c: (tok(b, h, c), nhb + h)),
        pl.BlockSpec((q, 2 * hb * hd), lambda b, h, c: (tok(b, h, c), 2 * nhb + h)),
        pl.BlockSpec((q, LANES), lambda b, h, c: (tok(b, h, c), h)),
        pl.BlockSpec((CONV_W, hb * hd), lambda b, h, c: (0, h)),
        pl.BlockSpec((CONV_W, hb * hd), lambda b, h, c: (0, nhb + h)),
        pl.BlockSpec((CONV_W, 2 * hb * hd), lambda b, h, c: (0, nhb + h)),
        pl.BlockSpec((1, LANES), lambda b, h, c: (0, h)),
        pl.BlockSpec((1, LANES), lambda b, h, c: (0, h)),
        pl.BlockSpec((1, hd), lambda b, h, c: (0, 0)),
    ]
    o, st = pl.pallas_call(
        functools.partial(_dn_prompt_kernel, hb=hb, hv=2 * hk),
        grid=(batch, nhb, nc),
        in_specs=in_specs,
        out_specs=[pl.BlockSpec((q, 2 * hb * hd), lambda b, h, c: (tok(b, h, c), h)),
                   pl.BlockSpec((1, 2 * hb, hd, hd), lambda b, h, c: (b, h, 0, 0))],
        out_shape=[jax.ShapeDtypeStruct((batch * seq, 2 * hk * hd), BF16),
                   jax.ShapeDtypeStruct((batch, 2 * hk, hd, hd), F32)],
        scratch_shapes=[pltpu.VMEM((2 * hb, hd, hd), F32),
                        pltpu.VMEM((q + SUBLANES, 4 * hb * hd), F32)],
        compiler_params=_cparams(("parallel", "parallel", "arbitrary")),
        name="dn_prompt",
    )(proj, proj, proj, proj, ba, w["conv_w"], w["conv_w"], w["conv_w"], w["a_log"], w["dt_bias"], w["norm"])
    return o, st


def _rotate(x, cos, s1, s2):
    w = x.shape[1]
    return x * cos + pltpu.roll(x, w - 1, 1) * s1 + pltpu.roll(x, 1, 1) * s2


def _ret_prompt_kernel(q_ref, k_ref, v_ref, g_ref, cos_ref, s1_ref, s2_ref, lg_ref, y_ref, st_ref, s_scr, *, hb):
    c = pl.program_id(2)
    q = q_ref.shape[0]

    @pl.when(c == 0)
    def _():
        s_scr[...] = jnp.zeros_like(s_scr)

    tile = lambda t: t if hb == 1 else jnp.concatenate([t] * hb, axis=1)
    cos, s1, s2 = tile(cos_ref[...]), tile(s1_ref[...]), tile(s2_ref[...])
    qa = _rotate(q_ref[...], cos, s1, s2)
    ka = _rotate(k_ref[...], cos, s1, s2) * (RET_DK ** -0.5)
    q3 = jnp.stack([qa[:, RET_DK * h:RET_DK * (h + 1)] for h in range(hb)])
    k3 = jnp.stack([ka[:, RET_DK * h:RET_DK * (h + 1)] for h in range(hb)])
    v3 = jnp.stack([v_ref[:, RET_DV * h:RET_DV * (h + 1)] for h in range(hb)])
    lg = lg_ref[...][:, 0:1, 0:1]
    row = lax.broadcasted_iota(jnp.int32, (q, q), 0)
    col = lax.broadcasted_iota(jnp.int32, (q, q), 1)
    dist = (row - col).astype(F32)[None]
    decay = jnp.exp(jnp.where((row >= col)[None], dist * lg, NEG_BIG))
    pos = lax.broadcasted_iota(jnp.int32, (1, q, 1), 1).astype(F32)
    scores = _bdot_nt(q3, k3)
    s_old = s_scr[...]
    y = _bdot(scores * decay, v3) + _bdot(q3, s_old) * jnp.exp((pos + 1.0) * lg)
    wv = jnp.exp((float(q - 1) - pos) * lg) * v3
    k_t = jnp.stack([jnp.concatenate([_tr(k3[h][:, LANES * i:LANES * (i + 1)]) for i in range(RET_DK // LANES)],
                                     axis=0) for h in range(hb)])
    s_new = s_old * jnp.exp(float(q) * lg) + _bdot(k_t, wv)
    s_scr[...] = s_new
    yn = _rms(y)
    for h in range(hb):
        hs = slice(RET_DV * h, RET_DV * (h + 1))
        y_ref[:, hs] = (yn[h] * _silu(g_ref[:, hs])).astype(BF16)

    @pl.when(c == pl.num_programs(2) - 1)
    def _():
        st_ref[0] = s_new


def _ret_prompt(proj, rope, lg, nheads, batch, seq):
    q = _pick(seq, (RET_CHUNK, 64, 32, 16, 8))
    hb = _pick(nheads, (RET_HEADS_PER_STEP, 2, 1))
    nc = seq // q
    nhb = nheads // hb
    tok = lambda b, h, c: b * nc + c
    v0 = 2 * nheads * RET_DK // (hb * RET_DV)
    g0 = v0 + nhb
    cos, s1, s2 = rope
    in_specs = [
        pl.BlockSpec((q, hb * RET_DK), lambda b, h, c: (tok(b, h, c), h)),
        pl.BlockSpec((q, hb * RET_DK), lambda b, h, c: (tok(b, h, c), nhb + h)),
        pl.BlockSpec((q, hb * RET_DV), lambda b, h, c: (tok(b, h, c), v0 + h)),
        pl.BlockSpec((q, hb * RET_DV), lambda b, h, c: (tok(b, h, c), g0 + h)),
        pl.BlockSpec((q, RET_DK), lambda b, h, c: (c, 0)),
        pl.BlockSpec((q, RET_DK), lambda b, h, c: (c, 0)),
        pl.BlockSpec((q, RET_DK), lambda b, h, c: (c, 0)),
        pl.BlockSpec((hb, SUBLANES, LANES), lambda b, h, c: (h, 0, 0)),
    ]
    y, st = pl.pallas_call(
        functools.partial(_ret_prompt_kernel, hb=hb),
        grid=(batch, nhb, nc),
        in_specs=in_specs,
        out_specs=[pl.BlockSpec((q, hb * RET_DV), lambda b, h, c: (tok(b, h, c), h)),
                   pl.BlockSpec((1, hb, RET_DK, RET_DV), lambda b, h, c: (b, h, 0, 0))],
        out_shape=[jax.ShapeDtypeStruct((batch * seq, nheads * RET_DV), BF16),
                   jax.ShapeDtypeStruct((batch, nheads, RET_DK, RET_DV), F32)],
        scratch_shapes=[pltpu.VMEM((hb, RET_DK, RET_DV), F32)],
        compiler_params=_cparams(("parallel", "parallel", "arbitrary")),
        name="ret_prompt",
    )(proj, proj, proj, proj, cos, s1, s2, lg)
    return y, st


def _alias_args(prev, n_in, out_idx):
    if prev is None:
        return [], [], {}
    return [pl.BlockSpec(memory_space=pl.ANY)], [prev], {n_in: out_idx}


def _conv_step_kernel(*refs, has_bias, aliased):
    refs = list(refs)
    x_ref, cs_ref, w_ref = refs[:3]
    b_ref = refs[3] if has_bias else None
    o_ref, cn_ref = refs[3 + int(has_bias) + int(aliased):]
    w = w_ref[...]
    x = x_ref[...]
    c0, c1, c2 = cs_ref[0, 0], cs_ref[0, 1], cs_ref[0, 2]
    acc = c0 * w[0:1, :] + c1 * w[1:2, :] + c2 * w[2:3, :] + x * w[3:4, :]
    if has_bias:
        acc = acc + b_ref[...]
    o_ref[...] = _silu(acc)
    cn_ref[0, 0] = c1
    cn_ref[0, 1] = c2
    cn_ref[0, 2] = x


def _conv_step(proj, col0, cstate, layer, conv_w, conv_b, prev):
    m = proj.shape[0]
    cdim = cstate.shape[3]
    cb = _pick(cdim, (512, 256, 128))
    x0 = col0 // cb
    taps = CONV_W - 1
    in_specs = [pl.BlockSpec((m, cb), lambda j: (0, x0 + j)),
                pl.BlockSpec((1, taps, m, cb), lambda j: (layer, 0, 0, j)),
                pl.BlockSpec((CONV_W, cb), lambda j: (0, j))]
    args = [proj, cstate, conv_w]
    if conv_b is not None:
        in_specs.append(pl.BlockSpec((1, cb), lambda j: (0, j)))
        args.append(conv_b)
    a_specs, a_args, aliases = _alias_args(prev, len(args), 1)
    return pl.pallas_call(
        functools.partial(_conv_step_kernel, has_bias=conv_b is not None, aliased=prev is not None),
        grid=(cdim // cb,),
        in_specs=in_specs + a_specs,
        out_specs=[pl.BlockSpec((m, cb), lambda j: (0, j)),
                   pl.BlockSpec((1, taps, m, cb), lambda j: (layer, 0, 0, j))],
        out_shape=[jax.ShapeDtypeStruct((m, cdim), F32),
                   jax.ShapeDtypeStruct(cstate.shape, F32)],
        input_output_aliases=aliases,
        compiler_params=_cparams(("parallel",)),
        name="conv_step",
    )(*args, *a_args)


def _ssd_step_kernel(*refs, nheads):
    z_ref, xs_ref, b_ref, c_ref, dt_ref, dtb_ref, alog_ref, d_ref, nw_ref, st_ref = refs[:10]
    y_ref, so_ref, ytb = refs[-3:]
    nb = xs_ref.shape[0]
    hd = SSM_HEAD_DIM
    xs = xs_ref[...]
    dtv = _softplus(dt_ref[...] + dtb_ref[...])
    decay = jnp.exp(dtv * (-jnp.exp(alog_ref[...])))
    pad = jnp.zeros((LANES - nb, LANES), F32)
    kmat = jnp.concatenate([b_ref[...], pad], axis=0).astype(BF16)
    qmat = jnp.concatenate([c_ref[...], pad], axis=0).T.astype(BF16)
    v_all = xs * _expand64(dtv, nheads)
    v_t = [jnp.concatenate([v_all[:, LANES * j:LANES * (j + 1)], pad], axis=0).T
           for j in range(nheads // 2)]
    lane = lax.broadcasted_iota(jnp.int32, (hd, LANES), 1)
    for r in range(nheads):
        vt = v_t[r // 2][hd * (r % 2):hd * (r % 2 + 1), :]
        for bi in range(nb):
            outer = jnp.dot(jnp.where(lane == bi, vt, 0.0).astype(BF16), kmat, preferred_element_type=F32)
            so_ref[0, bi, r] = st_ref[0, bi, r] * decay[bi:bi + 1, r:r + 1] + outer
    for r in range(nheads):
        acc = jnp.zeros((hd, LANES), F32)
        for bi in range(nb):
            yb = jnp.dot(so_ref[0, bi, r].astype(BF16), qmat, preferred_element_type=F32)
            acc = jnp.where(lane == bi, yb, acc)
        ytb[hd * r:hd * (r + 1), :] = acc
    y = [ytb[LANES * j:LANES * (j + 1), :].T[:nb, :] for j in range(nheads // 2)]
    y = y[0] if len(y) == 1 else jnp.concatenate(y, axis=1)
    y = y + xs * _expand64(d_ref[...], nheads)
    y = y * _silu(z_ref[...])
    y_ref[...] = (_rms(y) * nw_ref[...]).astype(BF16)


def _ssd_step(proj, xbc, dt, w, state, layer, prev):
    m = proj.shape[0]
    d_inner, nh, g, n = w["d_inner"], w["hpg"], SSM_GROUPS, SSM_D_STATE
    width = nh * SSM_HEAD_DIM
    nb = _pick(m, (2 * STEP_BATCH, STEP_BATCH))
    b0 = d_inner // n
    st_spec = pl.BlockSpec((1, nb, nh, SSM_HEAD_DIM, n), lambda i, gi: (layer, i, gi, 0, 0))
    in_specs = [
        pl.BlockSpec((nb, width), lambda i, gi: (i, gi)),
        pl.BlockSpec((nb, width), lambda i, gi: (i, gi)),
        pl.BlockSpec((nb, n), lambda i, gi: (i, b0 + gi)),
        pl.BlockSpec((nb, n), lambda i, gi: (i, b0 + g + gi)),
        pl.BlockSpec((nb, LANES), lambda i, gi: (i, gi)),
        pl.BlockSpec((1, LANES), lambda i, gi: (0, gi)),
        pl.BlockSpec((1, LANES), lambda i, gi: (0, gi)),
        pl.BlockSpec((1, LANES), lambda i, gi: (0, gi)),
        pl.BlockSpec((1, width), lambda i, gi: (0, gi)),
        st_spec,
    ]
    args = [proj, xbc, xbc, xbc, dt, w["dt_bias"], w["a_log"], w["d"], w["norm"], state]
    a_specs, a_args, aliases = _alias_args(prev, len(args), 1)
    y, st = pl.pallas_call(
        functools.partial(_ssd_step_kernel, nheads=nh),
        grid=(m // nb, g),
        in_specs=in_specs + a_specs,
        out_specs=[pl.BlockSpec((nb, width), lambda i, gi: (i, gi)), st_spec],
        out_shape=[jax.ShapeDtypeStruct((m, d_inner), BF16),
                   jax.ShapeDtypeStruct(state.shape, F32)],
        scratch_shapes=[pltpu.VMEM((width, LANES), F32)],
        input_output_aliases=aliases,
        compiler_params=_cparams(("parallel", "parallel")),
        name="ssd_step",
    )(*args, *a_args)
    return y, st


def _dn_step_kernel(q_ref, k_ref, v_ref, z_ref, ba_ref, alog_ref, dtb_ref, nw_ref, st_ref,
                    o_ref, so_ref, obuf, *, hv):
    nb = q_ref.shape[0]
    hd = DN_HEAD
    qq = q_ref[...]
    kk = k_ref[...]
    qq = qq * lax.rsqrt(jnp.sum(qq * qq, axis=-1, keepdims=True) + NORM_EPS) * (hd ** -0.5)
    kk = kk * lax.rsqrt(jnp.sum(kk * kk, axis=-1, keepdims=True) + NORM_EPS)
    vv = v_ref[...]
    ba = ba_ref[...]
    beta = _sigmoid(ba)
    eg = jnp.exp(-jnp.exp(alog_ref[...]) * _softplus(ba + dtb_ref[...]))
    k_t = _tr(kk)
    q_t = _tr(qq)
    for bi in range(nb):
        kcol = k_t[:, bi:bi + 1]
        qcol = q_t[:, bi:bi + 1]
        for j in range(2):
            s_old = st_ref[0, bi, j]
            b = beta[bi:bi + 1, j:j + 1]
            e = eg[bi:bi + 1, hv + j:hv + j + 1]
            ks = jnp.sum(kcol * s_old, axis=0, keepdims=True)
            u = b * vv[bi:bi + 1, hd * j:hd * (j + 1)] - (b * e) * ks
            s_new = s_old * e + kcol * u
            so_ref[0, bi, j] = s_new
            obuf[bi:bi + 1, hd * j:hd * (j + 1)] = jnp.sum(qcol * s_new, axis=0, keepdims=True)
    for j in range(2):
        o = obuf[:, hd * j:hd * (j + 1)]
        on = _rms(o) * nw_ref[...]
        o_ref[:, hd * j:hd * (j + 1)] = (on * _silu(z_ref[:, hd * j:hd * (j + 1)])).astype(BF16)


def _dn_step(proj, qkv, ba, w, state, layer):
    m = proj.shape[0]
    hk, hd = w["hk"], DN_HEAD
    nb = _pick(m, (2 * STEP_BATCH, STEP_BATCH))
    st_spec = pl.BlockSpec((1, nb, 2, hd, hd), lambda i, h: (layer, i, h, 0, 0))
    in_specs = [
        pl.BlockSpec((nb, hd), lambda i, h: (i, h)),
        pl.BlockSpec((nb, hd), lambda i, h: (i, hk + h)),
        pl.BlockSpec((nb, 2 * hd), lambda i, h: (i, hk + h)),
        pl.BlockSpec((nb, 2 * hd), lambda i, h: (i, 2 * hk + h)),
        pl.BlockSpec((nb, LANES), lambda i, h: (i, h)),
        pl.BlockSpec((1, LANES), lambda i, h: (0, h)),
        pl.BlockSpec((1, LANES), lambda i, h: (0, h)),
        pl.BlockSpec((1, hd), lambda i, h: (0, 0)),
        st_spec,
    ]
    o, st = pl.pallas_call(
        functools.partial(_dn_step_kernel, hv=2 * hk),
        grid=(m // nb, hk),
        in_specs=in_specs,
        out_specs=[pl.BlockSpec((nb, 2 * hd), lambda i, h: (i, h)), st_spec],
        out_shape=[jax.ShapeDtypeStruct((m, 2 * hk * hd), BF16),
                   jax.ShapeDtypeStruct(state.shape, F32)],
        scratch_shapes=[pltpu.VMEM((nb, 2 * hd), F32)],
        compiler_params=_cparams(("parallel", "parallel")),
        name="dn_step",
    )(qkv, qkv, qkv, proj, ba, w["a_log_step"], w["dt_bias_step"], w["norm"], state)
    return o, st


def _ret_step_kernel(q_ref, k_ref, v_ref, g_ref, cos_ref, s1_ref, s2_ref, lg_ref, st_ref, y_ref, so_ref, ybuf):
    nb = q_ref.shape[0]
    cos, s1, s2 = cos_ref[...], s1_ref[...], s2_ref[...]
    qq = _rotate(q_ref[...], cos, s1, s2)
    kk = _rotate(k_ref[...], cos, s1, s2) * (RET_DK ** -0.5)
    vv = v_ref[...]
    gamma = jnp.exp(lg_ref[0][0:1, 0:1])
    nk = RET_DK // LANES
    k_t = jnp.concatenate([_tr(kk[:, LANES * i:LANES * (i + 1)]) for i in range(nk)], axis=0)
    q_t = jnp.concatenate([_tr(qq[:, LANES * i:LANES * (i + 1)]) for i in range(nk)], axis=0)
    for bi in range(nb):
        s_new = st_ref[0, bi, 0] * gamma + k_t[:, bi:bi + 1] * vv[bi:bi + 1, :]
        so_ref[0, bi, 0] = s_new
        ybuf[bi:bi + 1, :] = jnp.sum(q_t[:, bi:bi + 1] * s_new, axis=0, keepdims=True)
    y_ref[...] = (_rms(ybuf[...]) * _silu(g_ref[...])).astype(BF16)


def _ret_step(proj, rope, lg, nheads, state, layer):
    m = proj.shape[0]
    nb = _pick(m, (STEP_BATCH,))
    v0 = 2 * nheads * RET_DK // RET_DV
    g0 = v0 + nheads
    cos, s1, s2 = rope
    st_spec = pl.BlockSpec((1, nb, 1, RET_DK, RET_DV), lambda i, h: (layer, i, h, 0, 0))
    in_specs = [
        pl.BlockSpec((nb, RET_DK), lambda i, h: (i, h)),
        pl.BlockSpec((nb, RET_DK), lambda i, h: (i, nheads + h)),
        pl.BlockSpec((nb, RET_DV), lambda i, h: (i, v0 + h)),
        pl.BlockSpec((nb, RET_DV), lambda i, h: (i, g0 + h)),
        pl.BlockSpec((1, RET_DK), lambda i, h: (0, 0)),
        pl.BlockSpec((1, RET_DK), lambda i, h: (0, 0)),
        pl.BlockSpec((1, RET_DK), lambda i, h: (0, 0)),
        pl.BlockSpec((1, SUBLANES, LANES), lambda i, h: (h, 0, 0)),
        st_spec,
    ]
    y, st = pl.pallas_call(
        _ret_step_kernel,
        grid=(m // nb, nheads),
        in_specs=in_specs,
        out_specs=[pl.BlockSpec((nb, RET_DV), lambda i, h: (i, h)), st_spec],
        out_shape=[jax.ShapeDtypeStruct((m, nheads * RET_DV), BF16),
                   jax.ShapeDtypeStruct(state.shape, F32)],
        scratch_shapes=[pltpu.VMEM((nb, RET_DV), F32)],
        compiler_params=_cparams(("parallel", "parallel")),
        name="ret_step",
    )(proj, proj, proj, proj, cos, s1, s2, lg, state)
    return y, st


def _pad_lanes(x):
    return jnp.pad(x, [(0, 0)] * (x.ndim - 1) + [(0, LANES - x.shape[-1])])


def _prep_ssm(conv_w, conv_b, dt_bias, a_log, d_skip, norm_w):
    heads = dt_bias.shape[0]
    g = SSM_GROUPS
    hpg = heads // g
    d_inner = heads * SSM_HEAD_DIM
    conv_dim = conv_w.shape[1]
    main = d_inner + conv_dim
    per_group = lambda v: _pad_lanes(v.reshape(g, hpg)).reshape(1, g * LANES)
    return dict(d_inner=d_inner, hpg=hpg, conv_dim=conv_dim, main=main,
                conv_w=conv_w, conv_b=conv_b.reshape(1, conv_dim),
                dt_bias=per_group(dt_bias), a_log=per_group(a_log), d=per_group(d_skip),
                norm=norm_w.reshape(1, d_inner))


def _prep_dn(conv_w, a_log, dt_bias, norm_w):
    hv = a_log.shape[0]
    hk = hv // 2
    hb = _pick(hk, (DN_HEADS_PER_STEP, 2, 1))
    conv_dim = conv_w.shape[1]
    main = conv_dim + hv * DN_HEAD

    def table(v, heads_per_step):
        t = v.reshape(hk // heads_per_step, 2 * heads_per_step)
        t = jnp.pad(t, ((0, 0), (hv, LANES - hv - 2 * heads_per_step)))
        return t.reshape(1, -1)

    return dict(hk=hk, hb=hb, conv_dim=conv_dim, main=main, conv_w=conv_w,
                a_log=table(a_log, hb), dt_bias=table(dt_bias, hb),
                a_log_step=table(a_log, 1), dt_bias_step=table(dt_bias, 1),
                norm=norm_w.reshape(1, DN_HEAD))


def _rope_tables(pos):
    half = RET_DK // 2
    inv = 1.0 / (RET_ROPE_BASE ** jnp.linspace(0.0, 1.0, half, dtype=F32))
    ang = pos.astype(F32)[:, None] * inv[None, :]
    cos, sin, zero = jnp.cos(ang), jnp.sin(ang), jnp.zeros_like(ang)
    inter = lambda a, b: jnp.stack([a, b], axis=-1).reshape(pos.shape[0], RET_DK)
    return inter(cos, cos), inter(-sin, zero), inter(zero, sin)


def _forward(x, p, prompt, states, prm, wts):
    batch, seq, d_model = x.shape
    m = batch * seq
    h = x.reshape(m, d_model)
    p = p.reshape(p.shape[0], m, p.shape[-1])
    depth = prm["norm_mix_pre"].shape[0]
    ssm_s, ssm_c, dn_s, dn_c, ret_s = states
    o_ssm = o_ssm_c = o_dn = o_dn_c = o_ret = None
    p_ssm, p_ssm_c, p_dn, p_dn_c, p_ret = [], [], [], [], []
    tail = slice(seq - (CONV_W - 1), seq)
    xn = _norm(h, prm["norm_mix_pre"][0])
    for i in range(depth):
        kind, j = i % 3, i // 3
        if kind == 0:
            w = wts["ssm"][j]
            d_inner, conv_dim = w["d_inner"], w["conv_dim"]
            proj = _mm(xn, wts["ssm_in_t"], j, w["main"], True)
            dt = _mm_small(xn, wts["ssm_in_t"], j, w["main"], SSM_GROUPS * w["hpg"], SSM_GROUPS, w["hpg"])
            if prompt:
                y, st = _ssd_prompt(proj, dt, w, batch, seq)
                p_ssm.append(st)
                p_ssm_c.append(proj.reshape(batch, seq, -1)[:, tail, d_inner:d_inner + conv_dim])
            else:
                xbc, o_ssm_c = _conv_step(proj, d_inner, ssm_c, j, w["conv_w"], w["conv_b"], o_ssm_c)
                y, o_ssm = _ssd_step(proj, xbc, dt, w, ssm_s, j, o_ssm)
            w_out = wts["ssm_out"]
        elif kind == 1:
            w = wts["dn"][j]
            conv_dim = w["conv_dim"]
            proj = _mm(xn, wts["dn_in_t"], j, w["main"], True)
            nblk, width = (w["hk"] // w["hb"], 2 * w["hb"]) if prompt else (w["hk"], 2)
            ba = _mm_small(xn, wts["dn_in_t"], j, w["main"], 4 * w["hk"], nblk, width)
            if prompt:
                y, st = _dn_prompt(proj, ba, w, batch, seq)
                p_dn.append(st)
                p_dn_c.append(proj.reshape(batch, seq, -1)[:, tail, :conv_dim])
            else:
                qkv, o_dn_c = _conv_step(proj, 0, dn_c, j, w["conv_w"], None, o_dn_c)
                y, o_dn = _dn_step(proj, qkv, ba, w, dn_s, j)
            w_out = wts["dn_out"]
        else:
            nheads = wts["ret_heads"]
            proj = _mm(xn, wts["ret_in"], j, wts["ret_in"].shape[2], False)
            if prompt:
                y, st = _ret_prompt(proj, wts["rope"], wts["lg"], nheads, batch, seq)
                p_ret.append(st)
            else:
                y, o_ret = _ret_step(proj, wts["rope"], wts["lg"], nheads, ret_s, j)
            w_out = wts["ret_out"]
        h, xn = _mm_out(y, w_out, j, h, prm["norm_mix_post"][i], prm["norm_ffn_pre"][i])
        act = _ffn_in(xn, wts["ffn_gate"], wts["ffn_up"], i)
        h, xn = _mm_out(act, wts["ffn_down"], i, h, prm["norm_ffn_post"][i], prm["norm_ple"][i])
        gain_next = prm["norm_mix_pre"][i + 1] if i + 1 < depth else None
        h, xn = _ple(h, xn, wts["ple_gate"], p, wts["ple_proj"], i, gain_next)
    h = h.reshape(batch, seq, d_model)
    if prompt:
        return (h, jnp.swapaxes(jnp.stack(p_ssm), -1, -2), jnp.stack(p_ssm_c), jnp.stack(p_dn),
                jnp.stack(p_dn_c), jnp.stack(p_ret))
    return (h, jnp.swapaxes(o_ssm, -1, -2), jnp.swapaxes(o_ssm_c, 1, 2), o_dn, jnp.swapaxes(o_dn_c, 1, 2), o_ret)


def kernel(x_prompt, x_sample, state_ssm, state_ssm_conv, state_delta, state_delta_conv, state_ret, p_prompt, p_sample, norm_mix_pre, norm_mix_post, norm_ffn_pre, norm_ffn_post, norm_ple, ffn_w_gate, ffn_w_up, ffn_w_down, ple_w_proj, ple_w_gate, ssm_w_in, ssm_conv_w, ssm_conv_b, ssm_dt_bias, ssm_a_log, ssm_d, ssm_norm, ssm_w_out, dn_w_in, dn_conv_w, dn_a_log, dn_dt_bias, dn_norm, dn_w_out, ret_w_in, ret_w_out):
    assert x_sample.shape[1] == 1, "the sample group advances one token per sequence"
    prm = dict(norm_mix_pre=norm_mix_pre, norm_mix_post=norm_mix_post, norm_ffn_pre=norm_ffn_pre,
               norm_ffn_post=norm_ffn_post, norm_ple=norm_ple)
    ret_heads = ret_w_out.shape[1] // RET_DV
    log_gamma = jnp.log(1.0 - 2.0 ** (-5.0 - jnp.arange(ret_heads, dtype=F32)))
    ssm_in_t, dn_in_t = jnp.swapaxes(ssm_w_in, 1, 2), jnp.swapaxes(dn_w_in, 1, 2)
    base = dict(
        ssm=[_prep_ssm(ssm_conv_w[j], ssm_conv_b[j], ssm_dt_bias[j], ssm_a_log[j], ssm_d[j],
                       ssm_norm[j]) for j in range(ssm_w_in.shape[0])],
        dn=[_prep_dn(dn_conv_w[j], dn_a_log[j], dn_dt_bias[j], dn_norm[j])
            for j in range(dn_w_in.shape[0])],
        ssm_in_t=ssm_in_t, dn_in_t=dn_in_t,
        ssm_out=ssm_w_out.astype(BF16), dn_out=dn_w_out.astype(BF16),
        ret_in=ret_w_in, ret_out=ret_w_out.astype(BF16), ret_heads=ret_heads,
        ffn_gate=ffn_w_gate, ffn_up=ffn_w_up, ffn_down=ffn_w_down.astype(BF16),
        ple_gate=ple_w_gate.astype(BF16), ple_proj=ple_w_proj.astype(BF16),
        lg=jnp.broadcast_to(log_gamma[:, None, None], (ret_heads, SUBLANES, LANES)),
    )
    seq = x_prompt.shape[1]
    pos_prompt = jnp.arange(seq, dtype=jnp.int32)
    pos_sample = PAST_LEN + jnp.arange(1, dtype=jnp.int32)
    out_p = _forward(x_prompt, p_prompt, True, (None,) * 5, prm, dict(base, rope=_rope_tables(pos_prompt)))
    states = (jnp.swapaxes(state_ssm, -1, -2), jnp.swapaxes(state_ssm_conv, 1, 2), state_delta,
              jnp.swapaxes(state_delta_conv, 1, 2), state_ret)
    out_s = _forward(x_sample, p_sample, False, states, prm, dict(base, rope=_rope_tables(pos_sample)))
    return (out_p[0], out_s[0]) + out_p[1:] + out_s[1:]
```

```python
import functools

import jax
import jax.numpy as jnp
from jax import lax
from jax.experimental import pallas as pl
from jax.experimental.pallas import tpu as pltpu

F32 = jnp.float32
BF16 = jnp.bfloat16

NORM_EPS = 1e-6
PAST_LEN = 16384
CONV_W = 4
SSM_GROUPS = 8
SSM_HEAD_DIM = 64
SSM_D_STATE = 128
DN_HEAD = 128
RET_DK = 256
RET_DV = 512
RET_ROPE_BASE = 10000.0

LANES = 128
SUBLANES = 8
NEG_BIG = -1e30
SSD_CHUNK = 128
RET_CHUNK = 128
DN_CHUNK = 128
DN_HEADS_PER_STEP = 16
RET_HEADS_PER_STEP = 8
STEP_BATCH = 8
VMEM_LIMIT = 52 * 1024 * 1024


def _pick(n, prefs):
    for p in prefs:
        if n % p == 0:
            return p
    return n


def _cparams(sem):
    return pltpu.CompilerParams(dimension_semantics=sem, vmem_limit_bytes=VMEM_LIMIT)


def _sigmoid(x):
    return 1.0 / (1.0 + jnp.exp(-x))


def _silu(x):
    h = 0.5 * x
    return h + h * jnp.tanh(h)


def _softplus(x):
    return jnp.maximum(x, 0.0) + jnp.log1p(jnp.exp(-jnp.abs(x)))


def _dot(a, b):
    return jnp.dot(a.astype(BF16), b.astype(BF16), preferred_element_type=F32)


def _dot_nt(a, b):
    return lax.dot_general(a.astype(BF16), b.astype(BF16), (((1,), (1,)), ((), ())),
                           preferred_element_type=F32)


def _dot01(m01, x):
    hi = x.astype(BF16)
    r = x - hi.astype(F32)
    mid = r.astype(BF16)
    lo = (r - mid.astype(F32)).astype(BF16)
    out = jnp.dot(m01, hi, preferred_element_type=F32)
    out = out + jnp.dot(m01, mid, preferred_element_type=F32)
    return out + jnp.dot(m01, lo, preferred_element_type=F32)


def _tr(x):
    r, c = x.shape
    assert c == LANES and r <= LANES
    if r < LANES:
        x = jnp.concatenate([x, jnp.zeros((LANES - r, c), x.dtype)], axis=0)
    return x.T[:, :r]


def _expand64(x, nheads):
    rows = x.shape[0]
    lane = lax.broadcasted_iota(jnp.int32, (rows, LANES), 1)
    parts = []
    for j in range(nheads // 2):
        a = jnp.broadcast_to(x[:, 2 * j:2 * j + 1], (rows, LANES))
        b = jnp.broadcast_to(x[:, 2 * j + 1:2 * j + 2], (rows, LANES))
        parts.append(jnp.where(lane < 64, a, b))
    return parts[0] if len(parts) == 1 else jnp.concatenate(parts, axis=1)


def _rms(y):
    return y * lax.rsqrt(jnp.mean(y * y, axis=-1, keepdims=True) + NORM_EPS)


def _conv_chunk(buf_ref, col0, x, w, bias):
    q, width = x.shape
    cols = slice(col0, col0 + width)
    buf_ref[SUBLANES:SUBLANES + q, cols] = x
    acc = x * w[CONV_W - 1:CONV_W, :]
    for s in range(1, CONV_W):
        acc = acc + buf_ref[SUBLANES - s:SUBLANES - s + q, cols] * w[CONV_W - 1 - s:CONV_W - s, :]
    buf_ref[0:SUBLANES, cols] = x[q - SUBLANES:q, :]
    if bias is not None:
        acc = acc + bias
    return acc


def _bdot(a, b):
    return lax.dot_general(a.astype(BF16), b.astype(BF16), (((2,), (1,)), ((0,), (0,))),
                           preferred_element_type=F32)


def _bdot_nt(a, b):
    return lax.dot_general(a.astype(BF16), b.astype(BF16), (((2,), (2,)), ((0,), (0,))),
                           preferred_element_type=F32)


def _tri_inv(a, row, col):
    q = a.shape[-1]
    eye = (row == col).astype(F32)[None]
    d = eye - jnp.where((jnp.right_shift(row, 1) == jnp.right_shift(col, 1))[None], a, 0.0)
    sh = 1
    while (1 << sh) < q:
        same_big = jnp.right_shift(row, sh + 1) == jnp.right_shift(col, sh + 1)
        diff_small = jnp.right_shift(row, sh) != jnp.right_shift(col, sh)
        lb = jnp.where((same_big & diff_small)[None], a, 0.0)
        d = d - _bdot(d, _bdot(lb, d))
        sh += 1
    return d


def _norm_kernel(x_ref, g_ref, o_ref):
    o_ref[...] = (_rms(x_ref[...]) * g_ref[...]).astype(BF16)


def _norm(x, gain):
    m, k = x.shape
    tm = _pick(m, (512, 256, 128))
    return pl.pallas_call(
        _norm_kernel,
        grid=(m // tm,),
        in_specs=[pl.BlockSpec((tm, k), lambda i: (i, 0)),
                  pl.BlockSpec((1, k), lambda i: (0, 0))],
        out_specs=pl.BlockSpec((tm, k), lambda i: (i, 0)),
        out_shape=jax.ShapeDtypeStruct((m, k), BF16),
        compiler_params=_cparams(("parallel",)),
        name="norm",
    )(x, gain.reshape(1, k))


def _mm_kernel(x_ref, w_ref, o_ref, wb_ref, *, transposed):
    @pl.when(pl.program_id(1) == 0)
    def _():
        wb_ref[...] = w_ref[0].astype(BF16)

    if transposed:
        o_ref[...] = lax.dot_general(x_ref[...], wb_ref[...], (((1,), (1,)), ((), ())),
                                     preferred_element_type=F32)
    else:
        o_ref[...] = jnp.dot(x_ref[...], wb_ref[...], preferred_element_type=F32)


def _mm(xn, w, layer, n, transposed):
    m, k = xn.shape
    tm = _pick(m, (1024, 512, 256, 128))
    tn = _pick(n, (1024, 512, 256, 128))
    if transposed:
        w_spec = pl.BlockSpec((1, tn, k), lambda j, i: (layer, j, 0))
        w_scratch = pltpu.VMEM((tn, k), BF16)
    else:
        w_spec = pl.BlockSpec((1, k, tn), lambda j, i: (layer, 0, j))
        w_scratch = pltpu.VMEM((k, tn), BF16)
    return pl.pallas_call(
        functools.partial(_mm_kernel, transposed=transposed),
        grid=(n // tn, m // tm),
        in_specs=[pl.BlockSpec((tm, k), lambda j, i: (i, 0)), w_spec],
        out_specs=pl.BlockSpec((tm, tn), lambda j, i: (i, j)),
        out_shape=jax.ShapeDtypeStruct((m, n), F32),
        scratch_shapes=[w_scratch],
        compiler_params=_cparams(("parallel", "arbitrary")),
        name="mm",
    )(xn, w)


def _mm_small_kernel(x_ref, w_ref, o_ref, *, nblk, width):
    w = w_ref[0]
    rows, k = w.shape
    if rows < LANES:
        w = jnp.concatenate([w, jnp.zeros((LANES - rows, k), w.dtype)], axis=0)
    y = _dot_nt(x_ref[...], w)
    for b in range(nblk):
        shift = (LANES - b * width) % LANES
        o_ref[:, LANES * b:LANES * (b + 1)] = y if shift == 0 else pltpu.roll(y, shift, 1)


def _mm_small(xn, w_t, layer, row0, rows, nblk, width):
    m, k = xn.shape
    tm = _pick(m, (1024, 512, 256, 128))
    assert rows <= LANES and row0 % rows == 0 and rows % SUBLANES == 0
    return pl.pallas_call(
        functools.partial(_mm_small_kernel, nblk=nblk, width=width),
        grid=(m // tm,),
        in_specs=[pl.BlockSpec((tm, k), lambda i: (i, 0)),
                  pl.BlockSpec((1, rows, k), lambda i: (layer, row0 // rows, 0))],
        out_specs=pl.BlockSpec((tm, nblk * LANES), lambda i: (i, 0)),
        out_shape=jax.ShapeDtypeStruct((m, nblk * LANES), F32),
        compiler_params=_cparams(("parallel",)),
        name="mm_small",
    )(xn, w_t)


def _mm_out_kernel(a_ref, w_ref, h_ref, g_ref, gn_ref, o_ref, xn_ref):
    y = jnp.dot(a_ref[...], w_ref[0], preferred_element_type=F32)
    h = h_ref[...] + _rms(y) * g_ref[...]
    o_ref[...] = h
    xn_ref[...] = (_rms(h) * gn_ref[...]).astype(BF16)


def _mm_out(a, w, layer, h, gain, gain_next):
    m, k = a.shape
    n = w.shape[2]
    tm = _pick(m, (256, 128))
    return pl.pallas_call(
        _mm_out_kernel,
        grid=(m // tm,),
        in_specs=[pl.BlockSpec((tm, k), lambda i: (i, 0)),
                  pl.BlockSpec((1, k, n), lambda i: (layer, 0, 0), pipeline_mode=pl.Buffered(1)),
                  pl.BlockSpec((tm, n), lambda i: (i, 0)),
                  pl.BlockSpec((1, n), lambda i: (0, 0)),
                  pl.BlockSpec((1, n), lambda i: (0, 0))],
        out_specs=[pl.BlockSpec((tm, n), lambda i: (i, 0)),
                   pl.BlockSpec((tm, n), lambda i: (i, 0))],
        out_shape=[jax.ShapeDtypeStruct((m, n), F32),
                   jax.ShapeDtypeStruct((m, n), BF16)],
        compiler_params=_cparams(("parallel",)),
        name="mm_out",
    )(a, w, h, gain.reshape(1, n), gain_next.reshape(1, n))


def _ffn_in_kernel(x_ref, wg_ref, wu_ref, o_ref, wgb_ref, wub_ref):
    @pl.when(pl.program_id(1) == 0)
    def _():
        wgb_ref[...] = wg_ref[0].astype(BF16)
        wub_ref[...] = wu_ref[0].astype(BF16)

    xn = x_ref[...]
    gate = jnp.dot(xn, wgb_ref[...], preferred_element_type=F32)
    up = jnp.dot(xn, wub_ref[...], preferred_element_type=F32)
    o_ref[...] = (_silu(gate) * up).astype(BF16)


def _ffn_in(xn, wg, wu, layer):
    m, k = xn.shape
    n = wg.shape[2]
    tm = _pick(m, (1024, 512, 256, 128))
    tn = _pick(n, (512, 256, 128))
    return pl.pallas_call(
        _ffn_in_kernel,
        grid=(n // tn, m // tm),
        in_specs=[pl.BlockSpec((tm, k), lambda j, i: (i, 0)),
                  pl.BlockSpec((1, k, tn), lambda j, i: (layer, 0, j)),
                  pl.BlockSpec((1, k, tn), lambda j, i: (layer, 0, j))],
        out_specs=pl.BlockSpec((tm, tn), lambda j, i: (i, j)),
        out_shape=jax.ShapeDtypeStruct((m, n), BF16),
        scratch_shapes=[pltpu.VMEM((k, tn), BF16), pltpu.VMEM((k, tn), BF16)],
        compiler_params=_cparams(("parallel", "arbitrary")),
        name="ffn_in",
    )(xn, wg, wu)


def _ple_kernel(*refs, has_next):
    h_ref, xn_ref, wg_ref, p_ref, wp_ref = refs[:5]
    gate = _sigmoid(jnp.dot(xn_ref[...], wg_ref[0], preferred_element_type=F32))
    proj = jnp.dot(p_ref[0].astype(BF16), wp_ref[0], preferred_element_type=F32)
    h = h_ref[...] + proj * gate
    if has_next:
        gn_ref, o_ref, on_ref = refs[5:]
        on_ref[...] = (_rms(h) * gn_ref[...]).astype(BF16)
    else:
        o_ref, = refs[5:]
    o_ref[...] = h


def _ple(h, xn, wg, p, wp, layer, gain_next):
    m, k = h.shape
    n = wg.shape[2]
    pd = p.shape[2]
    tm = _pick(m, (512, 256, 128))
    has_next = gain_next is not None
    row = lambda i: (i, 0)
    in_specs = [pl.BlockSpec((tm, k), row),
                pl.BlockSpec((tm, k), row),
                pl.BlockSpec((1, k, n), lambda i: (layer, 0, 0), pipeline_mode=pl.Buffered(1)),
                pl.BlockSpec((1, tm, pd), lambda i: (layer, i, 0)),
                pl.BlockSpec((1, pd, n), lambda i: (layer, 0, 0), pipeline_mode=pl.Buffered(1))]
    args = [h, xn, wg, p, wp]
    out_specs = [pl.BlockSpec((tm, n), row)]
    out_shape = [jax.ShapeDtypeStruct((m, n), F32)]
    if has_next:
        in_specs.append(pl.BlockSpec((1, n), lambda i: (0, 0)))
        args.append(gain_next.reshape(1, n))
        out_specs.append(pl.BlockSpec((tm, n), row))
        out_shape.append(jax.ShapeDtypeStruct((m, n), BF16))
    out = pl.pallas_call(
        functools.partial(_ple_kernel, has_next=has_next),
        grid=(m // tm,),
        in_specs=in_specs,
        out_specs=out_specs,
        out_shape=out_shape,
        compiler_params=_cparams(("parallel",)),
        name="ple",
    )(*args)
    return (out[0], out[1]) if has_next else (out[0], None)


def _ssd_prompt_kernel(z_ref, xs_ref, b_ref, c_ref, dt_ref, wx_ref, wb_ref, wc_ref, bx_ref, bb_ref, bc_ref,
                       dtb_ref, alog_ref, d_ref, nw_ref, y_ref, st_ref, s_scr, cbuf, *, nheads):
    c = pl.program_id(2)
    q, width = xs_ref.shape
    n = b_ref.shape[1]
    hd = SSM_HEAD_DIM

    @pl.when(c == 0)
    def _():
        s_scr[...] = jnp.zeros_like(s_scr)
        cbuf[0:SUBLANES, :] = jnp.zeros((SUBLANES, cbuf.shape[1]), F32)

    xs = _silu(_conv_chunk(cbuf, 0, xs_ref[...], wx_ref[...], bx_ref[...]))
    bm = _silu(_conv_chunk(cbuf, width, b_ref[...], wb_ref[...], bb_ref[...]))
    cm = _silu(_conv_chunk(cbuf, width + n, c_ref[...], wc_ref[...], bc_ref[...]))

    dtv = _softplus(dt_ref[...] + dtb_ref[...])
    la = dtv * (-jnp.exp(alog_ref[...]))
    row = lax.broadcasted_iota(jnp.int32, (q, q), 0)
    col = lax.broadcasted_iota(jnp.int32, (q, q), 1)
    causal = row >= col
    acs = _dot01(causal.astype(BF16), la)
    acs_t = _tr(acs)
    scores = _dot_nt(cm, bm)
    dt_t = _tr(dtv)
    s_old = s_scr[...]
    y = _dot_nt(cm, s_old) * _expand64(jnp.exp(acs), nheads)
    lane = lax.broadcasted_iota(jnp.int32, (q, LANES), 1)
    parts = []
    for j in range(nheads // 2):
        xp = xs[:, LANES * j:LANES * (j + 1)].astype(BF16)
        ys = []
        for t in range(2):
            r = 2 * j + t
            seg = acs[:, r:r + 1] - acs_t[r:r + 1, :]
            decay = jnp.exp(jnp.where(causal, seg, NEG_BIG))
            ys.append(jnp.dot((scores * decay * dt_t[r:r + 1, :]).astype(BF16), xp, preferred_element_type=F32))
        parts.append(jnp.where(lane < 64, ys[0], ys[1]))
    y = y + (parts[0] if len(parts) == 1 else jnp.concatenate(parts, axis=1))
    y = y + xs * _expand64(d_ref[...], nheads)

    last = acs[q - 1:q, :]
    wsc_t = _tr(jnp.exp(last - acs) * dtv)
    sub = lax.broadcasted_iota(jnp.int32, (LANES, q), 0)
    wv_t = []
    for j in range(width // LANES):
        scale = jnp.where(sub < hd, wsc_t[2 * j:2 * j + 1, :], wsc_t[2 * j + 1:2 * j + 2, :])
        wv_t.append(_tr(xs[:, LANES * j:LANES * (j + 1)]) * scale)
    wv_t = wv_t[0] if len(wv_t) == 1 else jnp.concatenate(wv_t, axis=0)
    e_last = jnp.exp(last)
    e_rows = jnp.concatenate([jnp.broadcast_to(e_last[:, r:r + 1], (hd, n)) for r in range(nheads)], axis=0)
    s_new = s_old * e_rows + _dot(wv_t, bm)
    s_scr[...] = s_new

    y = y * _silu(z_ref[...])
    y_ref[...] = (_rms(y) * nw_ref[...]).astype(BF16)

    @pl.when(c == pl.num_programs(2) - 1)
    def _():
        for r in range(nheads):
            st_ref[0, r] = s_new[hd * r:hd * (r + 1), :]


def _ssd_prompt(proj, dt, w, batch, seq):
    d_inner, nh, g, n = w["d_inner"], w["hpg"], SSM_GROUPS, SSM_D_STATE
    width = nh * SSM_HEAD_DIM
    q = _pick(seq, (SSD_CHUNK, 64, 32, 16, 8))
    nc = seq // q
    xs0, b0, c0 = d_inner // width, 2 * d_inner // n, 2 * d_inner // n + g
    cw0 = d_inner // n
    tok = lambda b, gi, c: b * nc + c
    in_specs = [
        pl.BlockSpec((q, width), lambda b, gi, c: (tok(b, gi, c), gi)),
        pl.BlockSpec((q, width), lambda b, gi, c: (tok(b, gi, c), xs0 + gi)),
        pl.BlockSpec((q, n), lambda b, gi, c: (tok(b, gi, c), b0 + gi)),
        pl.BlockSpec((q, n), lambda b, gi, c: (tok(b, gi, c), c0 + gi)),
        pl.BlockSpec((q, LANES), lambda b, gi, c: (tok(b, gi, c), gi)),
        pl.BlockSpec((CONV_W, width), lambda b, gi, c: (0, gi)),
        pl.BlockSpec((CONV_W, n), lambda b, gi, c: (0, cw0 + gi)),
        pl.BlockSpec((CONV_W, n), lambda b, gi, c: (0, cw0 + g + gi)),
        pl.BlockSpec((1, width), lambda b, gi, c: (0, gi)),
        pl.BlockSpec((1, n), lambda b, gi, c: (0, cw0 + gi)),
        pl.BlockSpec((1, n), lambda b, gi, c: (0, cw0 + g + gi)),
        pl.BlockSpec((1, LANES), lambda b, gi, c: (0, gi)),
        pl.BlockSpec((1, LANES), lambda b, gi, c: (0, gi)),
        pl.BlockSpec((1, LANES), lambda b, gi, c: (0, gi)),
        pl.BlockSpec((1, width), lambda b, gi, c: (0, gi)),
    ]
    y, st = pl.pallas_call(
        functools.partial(_ssd_prompt_kernel, nheads=nh),
        grid=(batch, g, nc),
        in_specs=in_specs,
        out_specs=[pl.BlockSpec((q, width), lambda b, gi, c: (tok(b, gi, c), gi)),
                   pl.BlockSpec((1, nh, SSM_HEAD_DIM, n), lambda b, gi, c: (b, gi, 0, 0))],
        out_shape=[jax.ShapeDtypeStruct((batch * seq, d_inner), BF16),
                   jax.ShapeDtypeStruct((batch, g * nh, SSM_HEAD_DIM, n), F32)],
        scratch_shapes=[pltpu.VMEM((width, n), F32),
                        pltpu.VMEM((q + SUBLANES, width + 2 * n), F32)],
        compiler_params=_cparams(("parallel", "parallel", "arbitrary")),
        name="ssd_prompt",
    )(proj, proj, proj, proj, dt, w["conv_w"], w["conv_w"], w["conv_w"], w["conv_b"], w["conv_b"], w["conv_b"],
      w["dt_bias"], w["a_log"], w["d"], w["norm"])
    return y, st


def _dn_prompt_kernel(q_ref, k_ref, v_ref, z_ref, ba_ref, wq_ref, wk_ref, wv_ref, alog_ref, dtb_ref, nw_ref,
                      o_ref, st_ref, s_scr, cbuf, *, hb, hv):
    c = pl.program_id(2)
    q = q_ref.shape[0]
    hd = DN_HEAD

    @pl.when(c == 0)
    def _():
        s_scr[...] = jnp.zeros_like(s_scr)
        cbuf[0:SUBLANES, :] = jnp.zeros((SUBLANES, cbuf.shape[1]), F32)

    qa = _silu(_conv_chunk(cbuf, 0, q_ref[...], wq_ref[...], None))
    ka = _silu(_conv_chunk(cbuf, hb * hd, k_ref[...], wk_ref[...], None))
    va = _silu(_conv_chunk(cbuf, 2 * hb * hd, v_ref[...], wv_ref[...], None))

    ba = ba_ref[...]
    beta_all = _sigmoid(ba)
    gg = -jnp.exp(alog_ref[...]) * _softplus(ba + dtb_ref[...])
    row = lax.broadcasted_iota(jnp.int32, (q, q), 0)
    col = lax.broadcasted_iota(jnp.int32, (q, q), 1)
    incl = row >= col
    strict = row > col
    acs = _dot01(incl.astype(BF16), gg)
    acs_t = _tr(acs)

    heads = [(kh, j) for kh in range(hb) for j in range(2)]
    q3 = jnp.stack([qa[:, kh * hd:(kh + 1) * hd] for kh in range(hb)])
    k3 = jnp.stack([ka[:, kh * hd:(kh + 1) * hd] for kh in range(hb)])
    q3 = q3 * lax.rsqrt(jnp.sum(q3 * q3, axis=-1, keepdims=True) + NORM_EPS) * (hd ** -0.5)
    k3 = k3 * lax.rsqrt(jnp.sum(k3 * k3, axis=-1, keepdims=True) + NORM_EPS)
    kk3 = _bdot_nt(k3, k3)
    qk3 = _bdot_nt(q3, k3)
    k_t3 = jnp.stack([_tr(k3[kh]) for kh in range(hb)])

    nh = len(heads)
    a_col = jnp.stack([acs[:, hv + i:hv + i + 1] for i in range(nh)])
    a_row = jnp.stack([acs_t[hv + i:hv + i + 1, :] for i in range(nh)])
    last = jnp.stack([acs[q - 1:q, hv + i:hv + i + 1] for i in range(nh)])
    bcol = jnp.stack([beta_all[:, i:i + 1] for i in range(nh)])
    rep = lambda x: jnp.stack([x[kh] for kh, _ in heads])
    k_v, q_v = rep(k3), rep(q3)
    gam = jnp.exp(jnp.where(incl[None], a_col - a_row, NEG_BIG))
    a_mat = jnp.where(strict[None], rep(kk3) * gam * bcol, 0.0)
    t_mat = _tri_inv(a_mat, row, col)
    v3 = jnp.stack([va[:, i * hd:(i + 1) * hd] for i in range(len(heads))])
    e_col = jnp.exp(a_col)
    rhs = jnp.concatenate([bcol * v3, (bcol * e_col) * k_v], axis=2)
    sol = _bdot(t_mat, rhs)
    s_old = s_scr[...]
    u = sol[:, :, :hd] - _bdot(sol[:, :, hd:], s_old)
    o = _bdot(rep(qk3) * gam, u) + _bdot(q_v, s_old) * e_col
    s_scr[...] = s_old * jnp.exp(last) + _bdot(rep(k_t3), jnp.exp(last - a_col) * u)
    on = _rms(o) * nw_ref[...]
    for i in range(len(heads)):
        hs = slice(i * hd, (i + 1) * hd)
        o_ref[:, hs] = (on[i] * _silu(z_ref[:, hs])).astype(BF16)

    @pl.when(c == pl.num_programs(2) - 1)
    def _():
        st_ref[0] = s_scr[...]


def _dn_prompt(proj, ba, w, batch, seq):
    hk, hd = w["hk"], DN_HEAD
    hb = w["hb"]
    q = _pick(seq, (DN_CHUNK, 32, 16, 8))
    nc = seq // q
    nhb = hk // hb
    tok = lambda b, h, c: b * nc + c
    in_specs = [
        pl.BlockSpec((q, hb * hd), lambda b, h, c: (tok(b, h, c), h)),
        pl.BlockSpec((q, hb * hd), lambda b, h, c: (tok(b, h, c), nhb + h)),
        pl.BlockSpec((q, 2 * hb * hd), lambda b, h, c: (tok(b, h, c), nhb + h)),
        pl.BlockSpec((q, 2 * hb * hd), lambda b, h, c: (tok(b, h, c), 2 * nhb + h)),
        pl.BlockSpec((q, LANES), lambda b, h, c: (tok(b, h, c), h)),
        pl.BlockSpec((CONV_W, hb * hd), lambda b, h, c: (0, h)),
        pl.BlockSpec((CONV_W, hb * hd), lambda b, h, c: (0, nhb + h)),
        pl.BlockSpec((CONV_W, 2 * hb * hd), lambda b, h, c: (0, nhb + h)),
        pl.BlockSpec((1, LANES), lambda b, h, c: (0, h)),
        pl.BlockSpec((1, LANES), lambda b, h, c: (0, h)),
        pl.BlockSpec((1, hd), lambda b, h, c: (0, 0)),
    ]
    o, st = pl.pallas_call(
        functools.partial(_dn_prompt_kernel, hb=hb, hv=2 * hk),
        grid=(batch, nhb, nc),
        in_specs=in_specs,
        out_specs=[pl.BlockSpec((q, 2 * hb * hd), lambda b, h, c: (tok(b, h, c), h)),
                   pl.BlockSpec((1, 2 * hb, hd, hd), lambda b, h, c: (b, h, 0, 0))],
        out_shape=[jax.ShapeDtypeStruct((batch * seq, 2 * hk * hd), BF16),
                   jax.ShapeDtypeStruct((batch, 2 * hk, hd, hd), F32)],
        scratch_shapes=[pltpu.VMEM((2 * hb, hd, hd), F32),
                        pltpu.VMEM((q + SUBLANES, 4 * hb * hd), F32)],
        compiler_params=_cparams(("parallel", "parallel", "arbitrary")),
        name="dn_prompt",
    )(proj, proj, proj, proj, ba, w["conv_w"], w["conv_w"], w["conv_w"], w["a_log"], w["dt_bias"], w["norm"])
    return o, st


def _rotate(x, cos, s1, s2):
    w = x.shape[1]
    return x * cos + pltpu.roll(x, w - 1, 1) * s1 + pltpu.roll(x, 1, 1) * s2


def _ret_prompt_kernel(q_ref, k_ref, v_ref, g_ref, cos_ref, s1_ref, s2_ref, lg_ref, y_ref, st_ref, s_scr, *, hb):
    c = pl.program_id(2)
    q = q_ref.shape[0]

    @pl.when(c == 0)
    def _():
        s_scr[...] = jnp.zeros_like(s_scr)

    tile = lambda t: t if hb == 1 else jnp.concatenate([t] * hb, axis=1)
    cos, s1, s2 = tile(cos_ref[...]), tile(s1_ref[...]), tile(s2_ref[...])
    qa = _rotate(q_ref[...], cos, s1, s2)
    ka = _rotate(k_ref[...], cos, s1, s2) * (RET_DK ** -0.5)
    q3 = jnp.stack([qa[:, RET_DK * h:RET_DK * (h + 1)] for h in range(hb)])
    k3 = jnp.stack([ka[:, RET_DK * h:RET_DK * (h + 1)] for h in range(hb)])
    v3 = jnp.stack([v_ref[:, RET_DV * h:RET_DV * (h + 1)] for h in range(hb)])
    lg = lg_ref[...][:, 0:1, 0:1]
    row = lax.broadcasted_iota(jnp.int32, (q, q), 0)
    col = lax.broadcasted_iota(jnp.int32, (q, q), 1)
    dist = (row - col).astype(F32)[None]
    decay = jnp.exp(jnp.where((row >= col)[None], dist * lg, NEG_BIG))
    pos = lax.broadcasted_iota(jnp.int32, (1, q, 1), 1).astype(F32)
    scores = _bdot_nt(q3, k3)
    s_old = s_scr[...]
    y = _bdot(scores * decay, v3) + _bdot(q3, s_old) * jnp.exp((pos + 1.0) * lg)
    wv = jnp.exp((float(q - 1) - pos) * lg) * v3
    k_t = jnp.stack([jnp.concatenate([_tr(k3[h][:, LANES * i:LANES * (i + 1)]) for i in range(RET_DK // LANES)],
                                     axis=0) for h in range(hb)])
    s_new = s_old * jnp.exp(float(q) * lg) + _bdot(k_t, wv)
    s_scr[...] = s_new
    yn = _rms(y)
    for h in range(hb):
        hs = slice(RET_DV * h, RET_DV * (h + 1))
        y_ref[:, hs] = (yn[h] * _silu(g_ref[:, hs])).astype(BF16)

    @pl.when(c == pl.num_programs(2) - 1)
    def _():
        st_ref[0] = s_new


def _ret_prompt(proj, rope, lg, nheads, batch, seq):
    q = _pick(seq, (RET_CHUNK, 64, 32, 16, 8))
    hb = _pick(nheads, (RET_HEADS_PER_STEP, 2, 1))
    nc = seq // q
    nhb = nheads // hb
    tok = lambda b, h, c: b * nc + c
    v0 = 2 * nheads * RET_DK // (hb * RET_DV)
    g0 = v0 + nhb
    cos, s1, s2 = rope
    in_specs = [
        pl.BlockSpec((q, hb * RET_DK), lambda b, h, c: (tok(b, h, c), h)),
        pl.BlockSpec((q, hb * RET_DK), lambda b, h, c: (tok(b, h, c), nhb + h)),
        pl.BlockSpec((q, hb * RET_DV), lambda b, h, c: (tok(b, h, c), v0 + h)),
        pl.BlockSpec((q, hb * RET_DV), lambda b, h, c: (tok(b, h, c), g0 + h)),
        pl.BlockSpec((q, RET_DK), lambda b, h, c: (c, 0)),
        pl.BlockSpec((q, RET_DK), lambda b, h, c: (c, 0)),
        pl.BlockSpec((q, RET_DK), lambda b, h, c: (c, 0)),
        pl.BlockSpec((hb, SUBLANES, LANES), lambda b, h, c: (h, 0, 0)),
    ]
    y, st = pl.pallas_call(
        functools.partial(_ret_prompt_kernel, hb=hb),
        grid=(batch, nhb, nc),
        in_specs=in_specs,
        out_specs=[pl.BlockSpec((q, hb * RET_DV), lambda b, h, c: (tok(b, h, c), h)),
                   pl.BlockSpec((1, hb, RET_DK, RET_DV), lambda b, h, c: (b, h, 0, 0))],
        out_shape=[jax.ShapeDtypeStruct((batch * seq, nheads * RET_DV), BF16),
                   jax.ShapeDtypeStruct((batch, nheads, RET_DK, RET_DV), F32)],
        scratch_shapes=[pltpu.VMEM((hb, RET_DK, RET_DV), F32)],
        compiler_params=_cparams(("parallel", "parallel", "arbitrary")),
        name="ret_prompt",
    )(proj, proj, proj, proj, cos, s1, s2, lg)
    return y, st


def _alias_args(prev, n_in, out_idx):
    if prev is None:
        return [], [], {}
    return [pl.BlockSpec(memory_space=pl.ANY)], [prev], {n_in: out_idx}


def _conv_step_kernel(*refs, has_bias, aliased):
    refs = list(refs)
    x_ref, cs_ref, w_ref = refs[:3]
    b_ref = refs[3] if has_bias else None
    o_ref, cn_ref = refs[3 + int(has_bias) + int(aliased):]
    w = w_ref[...]
    x = x_ref[...]
    c0, c1, c2 = cs_ref[0, 0], cs_ref[0, 1], cs_ref[0, 2]
    acc = c0 * w[0:1, :] + c1 * w[1:2, :] + c2 * w[2:3, :] + x * w[3:4, :]
    if has_bias:
        acc = acc + b_ref[...]
    o_ref[...] = _silu(acc)
    cn_ref[0, 0] = c1
    cn_ref[0, 1] = c2
    cn_ref[0, 2] = x


def _conv_step(proj, col0, cstate, layer, conv_w, conv_b, prev):
    m = proj.shape[0]
    cdim = cstate.shape[3]
    cb = _pick(cdim, (512, 256, 128))
    x0 = col0 // cb
    taps = CONV_W - 1
    in_specs = [pl.BlockSpec((m, cb), lambda j: (0, x0 + j)),
                pl.BlockSpec((1, taps, m, cb), lambda j: (layer, 0, 0, j)),
                pl.BlockSpec((CONV_W, cb), lambda j: (0, j))]
    args = [proj, cstate, conv_w]
    if conv_b is not None:
        in_specs.append(pl.BlockSpec((1, cb), lambda j: (0, j)))
        args.append(conv_b)
    a_specs, a_args, aliases = _alias_args(prev, len(args), 1)
    return pl.pallas_call(
        functools.partial(_conv_step_kernel, has_bias=conv_b is not None, aliased=prev is not None),
        grid=(cdim // cb,),
        in_specs=in_specs + a_specs,
        out_specs=[pl.BlockSpec((m, cb), lambda j: (0, j)),
                   pl.BlockSpec((1, taps, m, cb), lambda j: (layer, 0, 0, j))],
        out_shape=[jax.ShapeDtypeStruct((m, cdim), F32),
                   jax.ShapeDtypeStruct(cstate.shape, F32)],
        input_output_aliases=aliases,
        compiler_params=_cparams(("parallel",)),
        name="conv_step",
    )(*args, *a_args)


def _ssd_step_kernel(*refs, nheads):
    z_ref, xs_ref, b_ref, c_ref, dt_ref, dtb_ref, alog_ref, d_ref, nw_ref, st_ref = refs[:10]
    y_ref, so_ref, ytb = refs[-3:]
    nb = xs_ref.shape[0]
    hd = SSM_HEAD_DIM
    xs = xs_ref[...]
    dtv = _softplus(dt_ref[...] + dtb_ref[...])
    decay = jnp.exp(dtv * (-jnp.exp(alog_ref[...])))
    pad = jnp.zeros((LANES - nb, LANES), F32)
    kmat = jnp.concatenate([b_ref[...], pad], axis=0).astype(BF16)
    qmat = jnp.concatenate([c_ref[...], pad], axis=0).T.astype(BF16)
    v_all = xs * _expand64(dtv, nheads)
    v_t = [jnp.concatenate([v_all[:, LANES * j:LANES * (j + 1)], pad], axis=0).T
           for j in range(nheads // 2)]
    lane = lax.broadcasted_iota(jnp.int32, (hd, LANES), 1)
    for r in range(nheads):
        vt = v_t[r // 2][hd * (r % 2):hd * (r % 2 + 1), :]
        for bi in range(nb):
            outer = jnp.dot(jnp.where(lane == bi, vt, 0.0).astype(BF16), kmat, preferred_element_type=F32)
            so_ref[0, bi, r] = st_ref[0, bi, r] * decay[bi:bi + 1, r:r + 1] + outer
    for r in range(nheads):
        acc = jnp.zeros((hd, LANES), F32)
        for bi in range(nb):
            yb = jnp.dot(so_ref[0, bi, r].astype(BF16), qmat, preferred_element_type=F32)
            acc = jnp.where(lane == bi, yb, acc)
        ytb[hd * r:hd * (r + 1), :] = acc
    y = [ytb[LANES * j:LANES * (j + 1), :].T[:nb, :] for j in range(nheads // 2)]
    y = y[0] if len(y) == 1 else jnp.concatenate(y, axis=1)
    y = y + xs * _expand64(d_ref[...], nheads)
    y = y * _silu(z_ref[...])
    y_ref[...] = (_rms(y) * nw_ref[...]).astype(BF16)


def _ssd_step(proj, xbc, dt, w, state, layer, prev):
    m = proj.shape[0]
    d_inner, nh, g, n = w["d_inner"], w["hpg"], SSM_GROUPS, SSM_D_STATE
    width = nh * SSM_HEAD_DIM
    nb = _pick(m, (2 * STEP_BATCH, STEP_BATCH))
    b0 = d_inner // n
    st_spec = pl.BlockSpec((1, nb, nh, SSM_HEAD_DIM, n), lambda i, gi: (layer, i, gi, 0, 0))
    in_specs = [
        pl.BlockSpec((nb, width), lambda i, gi: (i, gi)),
        pl.BlockSpec((nb, width), lambda i, gi: (i, gi)),
        pl.BlockSpec((nb, n), lambda i, gi: (i, b0 + gi)),
        pl.BlockSpec((nb, n), lambda i, gi: (i, b0 + g + gi)),
        pl.BlockSpec((nb, LANES), lambda i, gi: (i, gi)),
        pl.BlockSpec((1, LANES), lambda i, gi: (0, gi)),
        pl.BlockSpec((1, LANES), lambda i, gi: (0, gi)),
        pl.BlockSpec((1, LANES), lambda i, gi: (0, gi)),
        pl.BlockSpec((1, width), lambda i, gi: (0, gi)),
        st_spec,
    ]
    args = [proj, xbc, xbc, xbc, dt, w["dt_bias"], w["a_log"], w["d"], w["norm"], state]
    a_specs, a_args, aliases = _alias_args(prev, len(args), 1)
    y, st = pl.pallas_call(
        functools.partial(_ssd_step_kernel, nheads=nh),
        grid=(m // nb, g),
        in_specs=in_specs + a_specs,
        out_specs=[pl.BlockSpec((nb, width), lambda i, gi: (i, gi)), st_spec],
        out_shape=[jax.ShapeDtypeStruct((m, d_inner), BF16),
                   jax.ShapeDtypeStruct(state.shape, F32)],
        scratch_shapes=[pltpu.VMEM((width, LANES), F32)],
        input_output_aliases=aliases,
        compiler_params=_cparams(("parallel", "parallel")),
        name="ssd_step",
    )(*args, *a_args)
    return y, st


def _dn_step_kernel(q_ref, k_ref, v_ref, z_ref, ba_ref, alog_ref, dtb_ref, nw_ref, st_ref,
                    o_ref, so_ref, obuf, *, hv):
    nb = q_ref.shape[0]
    hd = DN_HEAD
    qq = q_ref[...]
    kk = k_ref[...]
    qq = qq * lax.rsqrt(jnp.sum(qq * qq, axis=-1, keepdims=True) + NORM_EPS) * (hd ** -0.5)
    kk = kk * lax.rsqrt(jnp.sum(kk * kk, axis=-1, keepdims=True) + NORM_EPS)
    vv = v_ref[...]
    ba = ba_ref[...]
    beta = _sigmoid(ba)
    eg = jnp.exp(-jnp.exp(alog_ref[...]) * _softplus(ba + dtb_ref[...]))
    k_t = _tr(kk)
    q_t = _tr(qq)
    for bi in range(nb):
        kcol = k_t[:, bi:bi + 1]
        qcol = q_t[:, bi:bi + 1]
        for j in range(2):
            s_old = st_ref[0, bi, j]
            b = beta[bi:bi + 1, j:j + 1]
            e = eg[bi:bi + 1, hv + j:hv + j + 1]
            ks = jnp.sum(kcol * s_old, axis=0, keepdims=True)
            u = b * vv[bi:bi + 1, hd * j:hd * (j + 1)] - (b * e) * ks
            s_new = s_old * e + kcol * u
            so_ref[0, bi, j] = s_new
            obuf[bi:bi + 1, hd * j:hd * (j + 1)] = jnp.sum(qcol * s_new, axis=0, keepdims=True)
    for j in range(2):
        o = obuf[:, hd * j:hd * (j + 1)]
        on = _rms(o) * nw_ref[...]
        o_ref[:, hd * j:hd * (j + 1)] = (on * _silu(z_ref[:, hd * j:hd * (j + 1)])).astype(BF16)


def _dn_step(proj, qkv, ba, w, state, layer):
    m = proj.shape[0]
    hk, hd = w["hk"], DN_HEAD
    nb = _pick(m, (2 * STEP_BATCH, STEP_BATCH))
    st_spec = pl.BlockSpec((1, nb, 2, hd, hd), lambda i, h: (layer, i, h, 0, 0))
    in_specs = [
        pl.BlockSpec((nb, hd), lambda i, h: (i, h)),
        pl.BlockSpec((nb, hd), lambda i, h: (i, hk + h)),
        pl.BlockSpec((nb, 2 * hd), lambda i, h: (i, hk + h)),
        pl.BlockSpec((nb, 2 * hd), lambda i, h: (i, 2 * hk + h)),
        pl.BlockSpec((nb, LANES), lambda i, h: (i, h)),
        pl.BlockSpec((1, LANES), lambda i, h: (0, h)),
        pl.BlockSpec((1, LANES), lambda i, h: (0, h)),
        pl.BlockSpec((1, hd), lambda i, h: (0, 0)),
        st_spec,
    ]
    o, st = pl.pallas_call(
        functools.partial(_dn_step_kernel, hv=2 * hk),
        grid=(m // nb, hk),
        in_specs=in_specs,
        out_specs=[pl.BlockSpec((nb, 2 * hd), lambda i, h: (i, h)), st_spec],
        out_shape=[jax.ShapeDtypeStruct((m, 2 * hk * hd), BF16),
                   jax.ShapeDtypeStruct(state.shape, F32)],
        scratch_shapes=[pltpu.VMEM((nb, 2 * hd), F32)],
        compiler_params=_cparams(("parallel", "parallel")),
        name="dn_step",
    )(qkv, qkv, qkv, proj, ba, w["a_log_step"], w["dt_bias_step"], w["norm"], state)
    return o, st


def _ret_step_kernel(q_ref, k_ref, v_ref, g_ref, cos_ref, s1_ref, s2_ref, lg_ref, st_ref, y_ref, so_ref, ybuf):
    nb = q_ref.shape[0]
    cos, s1, s2 = cos_ref[...], s1_ref[...], s2_ref[...]
    qq = _rotate(q_ref[...], cos, s1, s2)
    kk = _rotate(k_ref[...], cos, s1, s2) * (RET_DK ** -0.5)
    vv = v_ref[...]
    gamma = jnp.exp(lg_ref[0][0:1, 0:1])
    nk = RET_DK // LANES
    k_t = jnp.concatenate([_tr(kk[:, LANES * i:LANES * (i + 1)]) for i in range(nk)], axis=0)
    q_t = jnp.concatenate([_tr(qq[:, LANES * i:LANES * (i + 1)]) for i in range(nk)], axis=0)
    for bi in range(nb):
        s_new = st_ref[0, bi, 0] * gamma + k_t[:, bi:bi + 1] * vv[bi:bi + 1, :]
        so_ref[0, bi, 0] = s_new
        ybuf[bi:bi + 1, :] = jnp.sum(q_t[:, bi:bi + 1] * s_new, axis=0, keepdims=True)
    y_ref[...] = (_rms(ybuf[...]) * _silu(g_ref[...])).astype(BF16)


def _ret_step(proj, rope, lg, nheads, state, layer):
    m = proj.shape[0]
    nb = _pick(m, (STEP_BATCH,))
    v0 = 2 * nheads * RET_DK // RET_DV
    g0 = v0 + nheads
    cos, s1, s2 = rope
    st_spec = pl.BlockSpec((1, nb, 1, RET_DK, RET_DV), lambda i, h: (layer, i, h, 0, 0))
    in_specs = [
        pl.BlockSpec((nb, RET_DK), lambda i, h: (i, h)),
        pl.BlockSpec((nb, RET_DK), lambda i, h: (i, nheads + h)),
        pl.BlockSpec((nb, RET_DV), lambda i, h: (i, v0 + h)),
        pl.BlockSpec((nb, RET_DV), lambda i, h: (i, g0 + h)),
        pl.BlockSpec((1, RET_DK), lambda i, h: (0, 0)),
        pl.BlockSpec((1, RET_DK), lambda i, h: (0, 0)),
        pl.BlockSpec((1, RET_DK), lambda i, h: (0, 0)),
        pl.BlockSpec((1, SUBLANES, LANES), lambda i, h: (h, 0, 0)),
        st_spec,
    ]
    y, st = pl.pallas_call(
        _ret_step_kernel,
        grid=(m // nb, nheads),
        in_specs=in_specs,
        out_specs=[pl.BlockSpec((nb, RET_DV), lambda i, h: (i, h)), st_spec],
        out_shape=[jax.ShapeDtypeStruct((m, nheads * RET_DV), BF16),
                   jax.ShapeDtypeStruct(state.shape, F32)],
        scratch_shapes=[pltpu.VMEM((nb, RET_DV), F32)],
        compiler_params=_cparams(("parallel", "parallel")),
        name="ret_step",
    )(proj, proj, proj, proj, cos, s1, s2, lg, state)
    return y, st


def _pad_lanes(x):
    return jnp.pad(x, [(0, 0)] * (x.ndim - 1) + [(0, LANES - x.shape[-1])])


def _prep_ssm(conv_w, conv_b, dt_bias, a_log, d_skip, norm_w):
    heads = dt_bias.shape[0]
    g = SSM_GROUPS
    hpg = heads // g
    d_inner = heads * SSM_HEAD_DIM
    conv_dim = conv_w.shape[1]
    main = d_inner + conv_dim
    per_group = lambda v: _pad_lanes(v.reshape(g, hpg)).reshape(1, g * LANES)
    return dict(d_inner=d_inner, hpg=hpg, conv_dim=conv_dim, main=main,
                conv_w=conv_w, conv_b=conv_b.reshape(1, conv_dim),
                dt_bias=per_group(dt_bias), a_log=per_group(a_log), d=per_group(d_skip),
                norm=norm_w.reshape(1, d_inner))


def _prep_dn(conv_w, a_log, dt_bias, norm_w):
    hv = a_log.shape[0]
    hk = hv // 2
    hb = _pick(hk, (DN_HEADS_PER_STEP, 2, 1))
    conv_dim = conv_w.shape[1]
    main = conv_dim + hv * DN_HEAD

    def table(v, heads_per_step):
        t = v.reshape(hk // heads_per_step, 2 * heads_per_step)
        t = jnp.pad(t, ((0, 0), (hv, LANES - hv - 2 * heads_per_step)))
        return t.reshape(1, -1)

    return dict(hk=hk, hb=hb, conv_dim=conv_dim, main=main, conv_w=conv_w,
                a_log=table(a_log, hb), dt_bias=table(dt_bias, hb),
                a_log_step=table(a_log, 1), dt_bias_step=table(dt_bias, 1),
                norm=norm_w.reshape(1, DN_HEAD))


def _rope_tables(pos):
    half = RET_DK // 2
    inv = 1.0 / (RET_ROPE_BASE ** jnp.linspace(0.0, 1.0, half, dtype=F32))
    ang = pos.astype(F32)[:, None] * inv[None, :]
    cos, sin, zero = jnp.cos(ang), jnp.sin(ang), jnp.zeros_like(ang)
    inter = lambda a, b: jnp.stack([a, b], axis=-1).reshape(pos.shape[0], RET_DK)
    return inter(cos, cos), inter(-sin, zero), inter(zero, sin)


def _forward(x, p, prompt, states, prm, wts):
    batch, seq, d_model = x.shape
    m = batch * seq
    h = x.reshape(m, d_model)
    p = p.reshape(p.shape[0], m, p.shape[-1])
    depth = prm["norm_mix_pre"].shape[0]
    ssm_s, ssm_c, dn_s, dn_c, ret_s = states
    o_ssm = o_ssm_c = o_dn = o_dn_c = o_ret = None
    p_ssm, p_ssm_c, p_dn, p_dn_c, p_ret = [], [], [], [], []
    tail = slice(seq - (CONV_W - 1), seq)
    xn = _norm(h, prm["norm_mix_pre"][0])
    for i in range(depth):
        kind, j = i % 3, i // 3
        if kind == 0:
            w = wts["ssm"][j]
            d_inner, conv_dim = w["d_inner"], w["conv_dim"]
            proj = _mm(xn, wts["ssm_in_t"], j, w["main"], True)
            dt = _mm_small(xn, wts["ssm_in_t"], j, w["main"], SSM_GROUPS * w["hpg"], SSM_GROUPS, w["hpg"])
            if prompt:
                y, st = _ssd_prompt(proj, dt, w, batch, seq)
                p_ssm.append(st)
                p_ssm_c.append(proj.reshape(batch, seq, -1)[:, tail, d_inner:d_inner + conv_dim])
            else:
                xbc, o_ssm_c = _conv_step(proj, d_inner, ssm_c, j, w["conv_w"], w["conv_b"], o_ssm_c)
                y, o_ssm = _ssd_step(proj, xbc, dt, w, ssm_s, j, o_ssm)
            w_out = wts["ssm_out"]
        elif kind == 1:
            w = wts["dn"][j]
            conv_dim = w["conv_dim"]
            proj = _mm(xn, wts["dn_in_t"], j, w["main"], True)
            nblk, width = (w["hk"] // w["hb"], 2 * w["hb"]) if prompt else (w["hk"], 2)
            ba = _mm_small(xn, wts["dn_in_t"], j, w["main"], 4 * w["hk"], nblk, width)
            if prompt:
                y, st = _dn_prompt(proj, ba, w, batch, seq)
                p_dn.append(st)
                p_dn_c.append(proj.reshape(batch, seq, -1)[:, tail, :conv_dim])
            else:
                qkv, o_dn_c = _conv_step(proj, 0, dn_c, j, w["conv_w"], None, o_dn_c)
                y, o_dn = _dn_step(proj, qkv, ba, w, dn_s, j)
            w_out = wts["dn_out"]
        else:
            nheads = wts["ret_heads"]
            proj = _mm(xn, wts["ret_in"], j, wts["ret_in"].shape[2], False)
            if prompt:
                y, st = _ret_prompt(proj, wts["rope"], wts["lg"], nheads, batch, seq)
                p_ret.append(st)
            else:
                y, o_ret = _ret_step(proj, wts["rope"], wts["lg"], nheads, ret_s, j)
            w_out = wts["ret_out"]
        h, xn = _mm_out(y, w_out, j, h, prm["norm_mix_post"][i], prm["norm_ffn_pre"][i])
        act = _ffn_in(xn, wts["ffn_gate"], wts["ffn_up"], i)
        h, xn = _mm_out(act, wts["ffn_down"], i, h, prm["norm_ffn_post"][i], prm["norm_ple"][i])
        gain_next = prm["norm_mix_pre"][i + 1] if i + 1 < depth else None
        h, xn = _ple(h, xn, wts["ple_gate"], p, wts["ple_proj"], i, gain_next)
    h = h.reshape(batch, seq, d_model)
    if prompt:
        return (h, jnp.swapaxes(jnp.stack(p_ssm), -1, -2), jnp.stack(p_ssm_c), jnp.stack(p_dn),
                jnp.stack(p_dn_c), jnp.stack(p_ret))
    return (h, jnp.swapaxes(o_ssm, -1, -2), jnp.swapaxes(o_ssm_c, 1, 2), o_dn, jnp.swapaxes(o_dn_c, 1, 2), o_ret)


def kernel(x_prompt, x_sample, state_ssm, state_ssm_conv, state_delta, state_delta_conv, state_ret, p_prompt, p_sample, norm_mix_pre, norm_mix_post, norm_ffn_pre, norm_ffn_post, norm_ple, ffn_w_gate, ffn_w_up, ffn_w_down, ple_w_proj, ple_w_gate, ssm_w_in, ssm_conv_w, ssm_conv_b, ssm_dt_bias, ssm_a_log, ssm_d, ssm_norm, ssm_w_out, dn_w_in, dn_conv_w, dn_a_log, dn_dt_bias, dn_norm, dn_w_out, ret_w_in, ret_w_out):
    assert x_sample.shape[1] == 1, "the sample group advances one token per sequence"
    prm = dict(norm_mix_pre=norm_mix_pre, norm_mix_post=norm_mix_post, norm_ffn_pre=norm_ffn_pre,
               norm_ffn_post=norm_ffn_post, norm_ple=norm_ple)
    ret_heads = ret_w_out.shape[1] // RET_DV
    log_gamma = jnp.log(1.0 - 2.0 ** (-5.0 - jnp.arange(ret_heads, dtype=F32)))
    ssm_in_t, dn_in_t = jnp.swapaxes(ssm_w_in, 1, 2), jnp.swapaxes(dn_w_in, 1, 2)
    base = dict(
        ssm=[_prep_ssm(ssm_conv_w[j], ssm_conv_b[j], ssm_dt_bias[j], ssm_a_log[j], ssm_d[j],
                       ssm_norm[j]) for j in range(ssm_w_in.shape[0])],
        dn=[_prep_dn(dn_conv_w[j], dn_a_log[j], dn_dt_bias[j], dn_norm[j])
            for j in range(dn_w_in.shape[0])],
        ssm_in_t=ssm_in_t, dn_in_t=dn_in_t,
        ssm_out=ssm_w_out.astype(BF16), dn_out=dn_w_out.astype(BF16),
        ret_in=ret_w_in, ret_out=ret_w_out.astype(BF16), ret_heads=ret_heads,
        ffn_gate=ffn_w_gate, ffn_up=ffn_w_up, ffn_down=ffn_w_down.astype(BF16),
        ple_gate=ple_w_gate.astype(BF16), ple_proj=ple_w_proj.astype(BF16),
        lg=jnp.broadcast_to(log_gamma[:, None, None], (ret_heads, SUBLANES, LANES)),
    )
    seq = x_prompt.shape[1]
    pos_prompt = jnp.arange(seq, dtype=jnp.int32)
    pos_sample = PAST_LEN + jnp.arange(1, dtype=jnp.int32)
    out_p = _forward(x_prompt, p_prompt, True, (None,) * 5, prm, dict(base, rope=_rope_tables(pos_prompt)))
    states = (jnp.swapaxes(state_ssm, -1, -2), jnp.swapaxes(state_ssm_conv, 1, 2), state_delta,
              jnp.swapaxes(state_delta_conv, 1, 2), state_ret)
    out_s = _forward(x_sample, p_sample, False, states, prm, dict(base, rope=_rope_tables(pos_sample)))
    return (out_p[0], out_s[0]) + out_p[1:] + out_s[1:]
```

```python
import functools

import jax
import jax.numpy as jnp
from jax import lax
from jax.experimental import pallas as pl
from jax.experimental.pallas import tpu as pltpu

F32 = jnp.float32
BF16 = jnp.bfloat16

NORM_EPS = 1e-6
PAST_LEN = 16384
CONV_W = 4
SSM_GROUPS = 8
SSM_HEAD_DIM = 64
SSM_D_STATE = 128
DN_HEAD = 128
RET_DK = 256
RET_DV = 512
RET_ROPE_BASE = 10000.0

LANES = 128
SUBLANES = 8
NEG_BIG = -1e30
SSD_CHUNK = 128
SSD_GROUPS_PER_STEP = 2
RET_CHUNK = 128
DN_CHUNK = 128
DN_HEADS_PER_STEP = 16
RET_HEADS_PER_STEP = 8
STEP_BATCH = 8
VMEM_LIMIT = 52 * 1024 * 1024


def _pick(n, prefs):
    for p in prefs:
        if n % p == 0:
            return p
    return n


def _cparams(sem):
    return pltpu.CompilerParams(dimension_semantics=sem, vmem_limit_bytes=VMEM_LIMIT)


def _sigmoid(x):
    return 1.0 / (1.0 + jnp.exp(-x))


def _silu(x):
    h = 0.5 * x
    return h + h * jnp.tanh(h)


def _softplus(x):
    return jnp.maximum(x, 0.0) + jnp.log1p(jnp.exp(-jnp.abs(x)))


def _dot(a, b):
    return jnp.dot(a.astype(BF16), b.astype(BF16), preferred_element_type=F32)


def _dot_nt(a, b):
    return lax.dot_general(a.astype(BF16), b.astype(BF16), (((1,), (1,)), ((), ())),
                           preferred_element_type=F32)


def _dot01(m01, x):
    hi = x.astype(BF16)
    r = x - hi.astype(F32)
    mid = r.astype(BF16)
    lo = (r - mid.astype(F32)).astype(BF16)
    out = jnp.dot(m01, hi, preferred_element_type=F32)
    out = out + jnp.dot(m01, mid, preferred_element_type=F32)
    return out + jnp.dot(m01, lo, preferred_element_type=F32)


def _tr(x):
    r, c = x.shape
    assert c == LANES and r <= LANES
    if r < LANES:
        x = jnp.concatenate([x, jnp.zeros((LANES - r, c), x.dtype)], axis=0)
    return x.T[:, :r]


def _expand64(x, nheads):
    rows = x.shape[0]
    lane = lax.broadcasted_iota(jnp.int32, (rows, LANES), 1)
    parts = []
    for j in range(nheads // 2):
        a = jnp.broadcast_to(x[:, 2 * j:2 * j + 1], (rows, LANES))
        b = jnp.broadcast_to(x[:, 2 * j + 1:2 * j + 2], (rows, LANES))
        parts.append(jnp.where(lane < 64, a, b))
    return parts[0] if len(parts) == 1 else jnp.concatenate(parts, axis=1)


def _rms(y):
    return y * lax.rsqrt(jnp.mean(y * y, axis=-1, keepdims=True) + NORM_EPS)


def _conv_chunk(buf_ref, col0, x, w, bias):
    q, width = x.shape
    cols = slice(col0, col0 + width)
    buf_ref[SUBLANES:SUBLANES + q, cols] = x
    acc = x * w[CONV_W - 1:CONV_W, :]
    for s in range(1, CONV_W):
        acc = acc + buf_ref[SUBLANES - s:SUBLANES - s + q, cols] * w[CONV_W - 1 - s:CONV_W - s, :]
    buf_ref[0:SUBLANES, cols] = x[q - SUBLANES:q, :]
    if bias is not None:
        acc = acc + bias
    return acc


def _bdot(a, b):
    return lax.dot_general(a.astype(BF16), b.astype(BF16), (((2,), (1,)), ((0,), (0,))),
                           preferred_element_type=F32)


def _bdot_nt(a, b):
    return lax.dot_general(a.astype(BF16), b.astype(BF16), (((2,), (2,)), ((0,), (0,))),
                           preferred_element_type=F32)


def _tri_inv(a, row, col):
    q = a.shape[-1]
    eye = (row == col).astype(F32)[None]
    d = eye - jnp.where((jnp.right_shift(row, 1) == jnp.right_shift(col, 1))[None], a, 0.0)
    sh = 1
    while (1 << sh) < q:
        same_big = jnp.right_shift(row, sh + 1) == jnp.right_shift(col, sh + 1)
        diff_small = jnp.right_shift(row, sh) != jnp.right_shift(col, sh)
        lb = jnp.where((same_big & diff_small)[None], a, 0.0)
        d = d - _bdot(d, _bdot(lb, d))
        sh += 1
    return d


def _norm_kernel(x_ref, g_ref, o_ref):
    o_ref[...] = (_rms(x_ref[...]) * g_ref[...]).astype(BF16)


def _norm(x, gain):
    m, k = x.shape
    tm = _pick(m, (512, 256, 128))
    return pl.pallas_call(
        _norm_kernel,
        grid=(m // tm,),
        in_specs=[pl.BlockSpec((tm, k), lambda i: (i, 0)),
                  pl.BlockSpec((1, k), lambda i: (0, 0))],
        out_specs=pl.BlockSpec((tm, k), lambda i: (i, 0)),
        out_shape=jax.ShapeDtypeStruct((m, k), BF16),
        compiler_params=_cparams(("parallel",)),
        name="norm",
    )(x, gain.reshape(1, k))


def _mm_kernel(x_ref, w_ref, o_ref, wb_ref, *, transposed):
    @pl.when(pl.program_id(1) == 0)
    def _():
        wb_ref[...] = w_ref[0].astype(BF16)

    if transposed:
        o_ref[...] = lax.dot_general(x_ref[...], wb_ref[...], (((1,), (1,)), ((), ())),
                                     preferred_element_type=F32)
    else:
        o_ref[...] = jnp.dot(x_ref[...], wb_ref[...], preferred_element_type=F32)


def _mm(xn, w, layer, n, transposed):
    m, k = xn.shape
    tm = _pick(m, (1024, 512, 256, 128))
    tn = _pick(n, (1024, 512, 256, 128))
    if transposed:
        w_spec = pl.BlockSpec((1, tn, k), lambda j, i: (layer, j, 0))
        w_scratch = pltpu.VMEM((tn, k), BF16)
    else:
        w_spec = pl.BlockSpec((1, k, tn), lambda j, i: (layer, 0, j))
        w_scratch = pltpu.VMEM((k, tn), BF16)
    return pl.pallas_call(
        functools.partial(_mm_kernel, transposed=transposed),
        grid=(n // tn, m // tm),
        in_specs=[pl.BlockSpec((tm, k), lambda j, i: (i, 0)), w_spec],
        out_specs=pl.BlockSpec((tm, tn), lambda j, i: (i, j)),
        out_shape=jax.ShapeDtypeStruct((m, n), F32),
        scratch_shapes=[w_scratch],
        compiler_params=_cparams(("parallel", "arbitrary")),
        name="mm",
    )(xn, w)


def _mm_small_kernel(x_ref, w_ref, o_ref, *, nblk, width):
    w = w_ref[0]
    rows, k = w.shape
    if rows < LANES:
        w = jnp.concatenate([w, jnp.zeros((LANES - rows, k), w.dtype)], axis=0)
    y = _dot_nt(x_ref[...], w)
    for b in range(nblk):
        shift = (LANES - b * width) % LANES
        o_ref[:, LANES * b:LANES * (b + 1)] = y if shift == 0 else pltpu.roll(y, shift, 1)


def _mm_small(xn, w_t, layer, row0, rows, nblk, width):
    m, k = xn.shape
    tm = _pick(m, (1024, 512, 256, 128))
    assert rows <= LANES and row0 % rows == 0 and rows % SUBLANES == 0
    return pl.pallas_call(
        functools.partial(_mm_small_kernel, nblk=nblk, width=width),
        grid=(m // tm,),
        in_specs=[pl.BlockSpec((tm, k), lambda i: (i, 0)),
                  pl.BlockSpec((1, rows, k), lambda i: (layer, row0 // rows, 0))],
        out_specs=pl.BlockSpec((tm, nblk * LANES), lambda i: (i, 0)),
        out_shape=jax.ShapeDtypeStruct((m, nblk * LANES), F32),
        compiler_params=_cparams(("parallel",)),
        name="mm_small",
    )(xn, w_t)


def _mm_out_kernel(a_ref, w_ref, h_ref, g_ref, gn_ref, o_ref, xn_ref):
    y = jnp.dot(a_ref[...], w_ref[0], preferred_element_type=F32)
    h = h_ref[...] + _rms(y) * g_ref[...]
    o_ref[...] = h
    xn_ref[...] = (_rms(h) * gn_ref[...]).astype(BF16)


def _mm_out(a, w, layer, h, gain, gain_next):
    m, k = a.shape
    n = w.shape[2]
    tm = _pick(m, (256, 128))
    return pl.pallas_call(
        _mm_out_kernel,
        grid=(m // tm,),
        in_specs=[pl.BlockSpec((tm, k), lambda i: (i, 0)),
                  pl.BlockSpec((1, k, n), lambda i: (layer, 0, 0), pipeline_mode=pl.Buffered(1)),
                  pl.BlockSpec((tm, n), lambda i: (i, 0)),
                  pl.BlockSpec((1, n), lambda i: (0, 0)),
                  pl.BlockSpec((1, n), lambda i: (0, 0))],
        out_specs=[pl.BlockSpec((tm, n), lambda i: (i, 0)),
                   pl.BlockSpec((tm, n), lambda i: (i, 0))],
        out_shape=[jax.ShapeDtypeStruct((m, n), F32),
                   jax.ShapeDtypeStruct((m, n), BF16)],
        compiler_params=_cparams(("parallel",)),
        name="mm_out",
    )(a, w, h, gain.reshape(1, n), gain_next.reshape(1, n))


def _ffn_in_kernel(x_ref, wg_ref, wu_ref, o_ref, wgb_ref, wub_ref):
    @pl.when(pl.program_id(1) == 0)
    def _():
        wgb_ref[...] = wg_ref[0].astype(BF16)
        wub_ref[...] = wu_ref[0].astype(BF16)

    xn = x_ref[...]
    gate = jnp.dot(xn, wgb_ref[...], preferred_element_type=F32)
    up = jnp.dot(xn, wub_ref[...], preferred_element_type=F32)
    o_ref[...] = (_silu(gate) * up).astype(BF16)


def _ffn_in(xn, wg, wu, layer):
    m, k = xn.shape
    n = wg.shape[2]
    tm = _pick(m, (1024, 512, 256, 128))
    tn = _pick(n, (512, 256, 128))
    return pl.pallas_call(
        _ffn_in_kernel,
        grid=(n // tn, m // tm),
        in_specs=[pl.BlockSpec((tm, k), lambda j, i: (i, 0)),
                  pl.BlockSpec((1, k, tn), lambda j, i: (layer, 0, j)),
                  pl.BlockSpec((1, k, tn), lambda j, i: (layer, 0, j))],
        out_specs=pl.BlockSpec((tm, tn), lambda j, i: (i, j)),
        out_shape=jax.ShapeDtypeStruct((m, n), BF16),
        scratch_shapes=[pltpu.VMEM((k, tn), BF16), pltpu.VMEM((k, tn), BF16)],
        compiler_params=_cparams(("parallel", "arbitrary")),
        name="ffn_in",
    )(xn, wg, wu)


def _ple_kernel(*refs, has_next):
    h_ref, xn_ref, wg_ref, p_ref, wp_ref = refs[:5]
    gate = _sigmoid(jnp.dot(xn_ref[...], wg_ref[0], preferred_element_type=F32))
    proj = jnp.dot(p_ref[0].astype(BF16), wp_ref[0], preferred_element_type=F32)
    h = h_ref[...] + proj * gate
    if has_next:
        gn_ref, o_ref, on_ref = refs[5:]
        on_ref[...] = (_rms(h) * gn_ref[...]).astype(BF16)
    else:
        o_ref, = refs[5:]
    o_ref[...] = h


def _ple(h, xn, wg, p, wp, layer, gain_next):
    m, k = h.shape
    n = wg.shape[2]
    pd = p.shape[2]
    tm = _pick(m, (512, 256, 128))
    has_next = gain_next is not None
    row = lambda i: (i, 0)
    in_specs = [pl.BlockSpec((tm, k), row),
                pl.BlockSpec((tm, k), row),
                pl.BlockSpec((1, k, n), lambda i: (layer, 0, 0), pipeline_mode=pl.Buffered(1)),
                pl.BlockSpec((1, tm, pd), lambda i: (layer, i, 0)),
                pl.BlockSpec((1, pd, n), lambda i: (layer, 0, 0), pipeline_mode=pl.Buffered(1))]
    args = [h, xn, wg, p, wp]
    out_specs = [pl.BlockSpec((tm, n), row)]
    out_shape = [jax.ShapeDtypeStruct((m, n), F32)]
    if has_next:
        in_specs.append(pl.BlockSpec((1, n), lambda i: (0, 0)))
        args.append(gain_next.reshape(1, n))
        out_specs.append(pl.BlockSpec((tm, n), row))
        out_shape.append(jax.ShapeDtypeStruct((m, n), BF16))
    out = pl.pallas_call(
        functools.partial(_ple_kernel, has_next=has_next),
        grid=(m // tm,),
        in_specs=in_specs,
        out_specs=out_specs,
        out_shape=out_shape,
        compiler_params=_cparams(("parallel",)),
        name="ple",
    )(*args)
    return (out[0], out[1]) if has_next else (out[0], None)


def _ssd_prompt_kernel(z_ref, xs_ref, b_ref, c_ref, dt_ref, wx_ref, wb_ref, wc_ref, bx_ref, bb_ref, bc_ref,
                       dtb_ref, alog_ref, d_ref, nw_ref, y_ref, st_ref, s_scr, cbuf, *, nheads, ng):
    c = pl.program_id(2)
    q = xs_ref.shape[0]
    width = xs_ref.shape[1] // ng
    n = b_ref.shape[1] // ng
    hd = SSM_HEAD_DIM
    groups = range(ng)

    @pl.when(c == 0)
    def _():
        s_scr[...] = jnp.zeros_like(s_scr)
        cbuf[0:SUBLANES, :] = jnp.zeros((SUBLANES, cbuf.shape[1]), F32)

    xs_a = _silu(_conv_chunk(cbuf, 0, xs_ref[...], wx_ref[...], bx_ref[...]))
    bm_a = _silu(_conv_chunk(cbuf, ng * width, b_ref[...], wb_ref[...], bb_ref[...]))
    cm_a = _silu(_conv_chunk(cbuf, ng * (width + n), c_ref[...], wc_ref[...], bc_ref[...]))
    xs = [xs_a[:, width * g:width * (g + 1)] for g in groups]
    bm = [bm_a[:, n * g:n * (g + 1)] for g in groups]
    cm = [cm_a[:, n * g:n * (g + 1)] for g in groups]

    dtv_a = _softplus(dt_ref[...] + dtb_ref[...])
    la_a = dtv_a * (-jnp.exp(alog_ref[...]))
    row = lax.broadcasted_iota(jnp.int32, (q, q), 0)
    col = lax.broadcasted_iota(jnp.int32, (q, q), 1)
    causal = row >= col
    acs_a = _dot01(causal.astype(BF16), la_a)
    dtv = [dtv_a[:, LANES * g:LANES * (g + 1)] for g in groups]
    acs = [acs_a[:, LANES * g:LANES * (g + 1)] for g in groups]
    acs_t = [_tr(a) for a in acs]
    dt_t = [_tr(d) for d in dtv]
    scores = [_dot_nt(cm[g], bm[g]) for g in groups]
    s_old = [s_scr[width * g:width * (g + 1), :] for g in groups]
    y = [_dot_nt(cm[g], s_old[g]) * _expand64(jnp.exp(acs[g]), nheads) for g in groups]
    lane = lax.broadcasted_iota(jnp.int32, (q, LANES), 1)
    parts = [[] for _ in groups]
    for j in range(nheads // 2):
        for g in groups:
            xp = xs[g][:, LANES * j:LANES * (j + 1)].astype(BF16)
            ys = []
            for t in range(2):
                r = 2 * j + t
                seg = acs[g][:, r:r + 1] - acs_t[g][r:r + 1, :]
                decay = jnp.exp(jnp.where(causal, seg, NEG_BIG))
                ys.append(jnp.dot((scores[g] * decay * dt_t[g][r:r + 1, :]).astype(BF16), xp,
                                  preferred_element_type=F32))
            parts[g].append(jnp.where(lane < 64, ys[0], ys[1]))
    d_a = d_ref[...]
    y = [y[g] + (parts[g][0] if len(parts[g]) == 1 else jnp.concatenate(parts[g], axis=1))
         + xs[g] * _expand64(d_a[:, LANES * g:LANES * (g + 1)], nheads) for g in groups]

    last = [acs[g][q - 1:q, :] for g in groups]
    wsc_t = [_tr(jnp.exp(last[g] - acs[g]) * dtv[g]) for g in groups]
    sub = lax.broadcasted_iota(jnp.int32, (LANES, q), 0)
    wv_t = [[] for _ in groups]
    for j in range(width // LANES):
        for g in groups:
            scale = jnp.where(sub < hd, wsc_t[g][2 * j:2 * j + 1, :], wsc_t[g][2 * j + 1:2 * j + 2, :])
            wv_t[g].append(_tr(xs[g][:, LANES * j:LANES * (j + 1)]) * scale)
    wv_t = [w[0] if len(w) == 1 else jnp.concatenate(w, axis=0) for w in wv_t]
    e_last = [jnp.exp(last[g]) for g in groups]
    e_rows = [jnp.concatenate([jnp.broadcast_to(e_last[g][:, r:r + 1], (hd, n)) for r in range(nheads)], axis=0)
              for g in groups]
    s_new = [s_old[g] * e_rows[g] + _dot(wv_t[g], bm[g]) for g in groups]
    for g in groups:
        s_scr[width * g:width * (g + 1), :] = s_new[g]

    z_a = z_ref[...]
    nw_a = nw_ref[...]
    for g in groups:
        ws = slice(width * g, width * (g + 1))
        yg = y[g] * _silu(z_a[:, ws])
        y_ref[:, ws] = (_rms(yg) * nw_a[:, ws]).astype(BF16)

    @pl.when(c == pl.num_programs(2) - 1)
    def _():
        for g in groups:
            for r in range(nheads):
                st_ref[0, g * nheads + r] = s_new[g][hd * r:hd * (r + 1), :]


def _ssd_prompt(proj, dt, w, batch, seq):
    d_inner, nh, g, n = w["d_inner"], w["hpg"], SSM_GROUPS, SSM_D_STATE
    width = nh * SSM_HEAD_DIM
    ng = SSD_GROUPS_PER_STEP
    q = _pick(seq, (SSD_CHUNK, 64, 32, 16, 8))
    nc = seq // q
    bw, bn = ng * width, ng * n
    xs0, b0, c0 = d_inner // bw, 2 * d_inner // bn, (2 * d_inner + g * n) // bn
    cwb, cwc = d_inner // bn, (d_inner + g * n) // bn
    tok = lambda b, gi, c: b * nc + c
    in_specs = [
        pl.BlockSpec((q, bw), lambda b, gi, c: (tok(b, gi, c), gi)),
        pl.BlockSpec((q, bw), lambda b, gi, c: (tok(b, gi, c), xs0 + gi)),
        pl.BlockSpec((q, bn), lambda b, gi, c: (tok(b, gi, c), b0 + gi)),
        pl.BlockSpec((q, bn), lambda b, gi, c: (tok(b, gi, c), c0 + gi)),
        pl.BlockSpec((q, ng * LANES), lambda b, gi, c: (tok(b, gi, c), gi)),
        pl.BlockSpec((CONV_W, bw), lambda b, gi, c: (0, gi)),
        pl.BlockSpec((CONV_W, bn), lambda b, gi, c: (0, cwb + gi)),
        pl.BlockSpec((CONV_W, bn), lambda b, gi, c: (0, cwc + gi)),
        pl.BlockSpec((1, bw), lambda b, gi, c: (0, gi)),
        pl.BlockSpec((1, bn), lambda b, gi, c: (0, cwb + gi)),
        pl.BlockSpec((1, bn), lambda b, gi, c: (0, cwc + gi)),
        pl.BlockSpec((1, ng * LANES), lambda b, gi, c: (0, gi)),
        pl.BlockSpec((1, ng * LANES), lambda b, gi, c: (0, gi)),
        pl.BlockSpec((1, ng * LANES), lambda b, gi, c: (0, gi)),
        pl.BlockSpec((1, bw), lambda b, gi, c: (0, gi)),
    ]
    y, st = pl.pallas_call(
        functools.partial(_ssd_prompt_kernel, nheads=nh, ng=ng),
        grid=(batch, g // ng, nc),
        in_specs=in_specs,
        out_specs=[pl.BlockSpec((q, bw), lambda b, gi, c: (tok(b, gi, c), gi)),
                   pl.BlockSpec((1, ng * nh, SSM_HEAD_DIM, n), lambda b, gi, c: (b, gi, 0, 0))],
        out_shape=[jax.ShapeDtypeStruct((batch * seq, d_inner), BF16),
                   jax.ShapeDtypeStruct((batch, g * nh, SSM_HEAD_DIM, n), F32)],
        scratch_shapes=[pltpu.VMEM((bw, n), F32),
                        pltpu.VMEM((q + SUBLANES, bw + 2 * bn), F32)],
        compiler_params=_cparams(("parallel", "parallel", "arbitrary")),
        name="ssd_prompt",
    )(proj, proj, proj, proj, dt, w["conv_w"], w["conv_w"], w["conv_w"], w["conv_b"], w["conv_b"], w["conv_b"],
      w["dt_bias"], w["a_log"], w["d"], w["norm"])
    return y, st


def _dn_prompt_kernel(q_ref, k_ref, v_ref, z_ref, ba_ref, wq_ref, wk_ref, wv_ref, alog_ref, dtb_ref, nw_ref,
                      o_ref, st_ref, s_scr, cbuf, *, hb, hv):
    c = pl.program_id(2)
    q = q_ref.shape[0]
    hd = DN_HEAD

    @pl.when(c == 0)
    def _():
        s_scr[...] = jnp.zeros_like(s_scr)
        cbuf[0:SUBLANES, :] = jnp.zeros((SUBLANES, cbuf.shape[1]), F32)

    qa = _silu(_conv_chunk(cbuf, 0, q_ref[...], wq_ref[...], None))
    ka = _silu(_conv_chunk(cbuf, hb * hd, k_ref[...], wk_ref[...], None))
    va = _silu(_conv_chunk(cbuf, 2 * hb * hd, v_ref[...], wv_ref[...], None))

    ba = ba_ref[...]
    beta_all = _sigmoid(ba)
    gg = -jnp.exp(alog_ref[...]) * _softplus(ba + dtb_ref[...])
    row = lax.broadcasted_iota(jnp.int32, (q, q), 0)
    col = lax.broadcasted_iota(jnp.int32, (q, q), 1)
    incl = row >= col
    strict = row > col
    acs = _dot01(incl.astype(BF16), gg)
    acs_t = _tr(acs)

    heads = [(kh, j) for kh in range(hb) for j in range(2)]
    q3 = jnp.stack([qa[:, kh * hd:(kh + 1) * hd] for kh in range(hb)])
    k3 = jnp.stack([ka[:, kh * hd:(kh + 1) * hd] for kh in range(hb)])
    q3 = q3 * lax.rsqrt(jnp.sum(q3 * q3, axis=-1, keepdims=True) + NORM_EPS) * (hd ** -0.5)
    k3 = k3 * lax.rsqrt(jnp.sum(k3 * k3, axis=-1, keepdims=True) + NORM_EPS)
    kk3 = _bdot_nt(k3, k3)
    qk3 = _bdot_nt(q3, k3)
    k_t3 = jnp.stack([_tr(k3[kh]) for kh in range(hb)])

    nh = len(heads)
    a_col = jnp.stack([acs[:, hv + i:hv + i + 1] for i in range(nh)])
    a_row = jnp.stack([acs_t[hv + i:hv + i + 1, :] for i in range(nh)])
    last = jnp.stack([acs[q - 1:q, hv + i:hv + i + 1] for i in range(nh)])
    bcol = jnp.stack([beta_all[:, i:i + 1] for i in range(nh)])
    rep = lambda x: jnp.stack([x[kh] for kh, _ in heads])
    k_v, q_v = rep(k3), rep(q3)
    gam = jnp.exp(jnp.where(incl[None], a_col - a_row, NEG_BIG))
    a_mat = jnp.where(strict[None], rep(kk3) * gam * bcol, 0.0)
    t_mat = _tri_inv(a_mat, row, col)
    v3 = jnp.stack([va[:, i * hd:(i + 1) * hd] for i in range(len(heads))])
    e_col = jnp.exp(a_col)
    rhs = jnp.concatenate([bcol * v3, (bcol * e_col) * k_v], axis=2)
    sol = _bdot(t_mat, rhs)
    s_old = s_scr[...]
    u = sol[:, :, :hd] - _bdot(sol[:, :, hd:], s_old)
    o = _bdot(rep(qk3) * gam, u) + _bdot(q_v, s_old) * e_col
    s_scr[...] = s_old * jnp.exp(last) + _bdot(rep(k_t3), jnp.exp(last - a_col) * u)
    on = _rms(o) * nw_ref[...]
    for i in range(len(heads)):
        hs = slice(i * hd, (i + 1) * hd)
        o_ref[:, hs] = (on[i] * _silu(z_ref[:, hs])).astype(BF16)

    @pl.when(c == pl.num_programs(2) - 1)
    def _():
        st_ref[0] = s_scr[...]


def _dn_prompt(proj, ba, w, batch, seq):
    hk, hd = w["hk"], DN_HEAD
    hb = w["hb"]
    q = _pick(seq, (DN_CHUNK, 32, 16, 8))
    nc = seq // q
    nhb = hk // hb
    tok = lambda b, h, c: b * nc + c
    in_specs = [
        pl.BlockSpec((q, hb * hd), lambda b, h, c: (tok(b, h, c), h)),
        pl.BlockSpec((q, hb * hd), lambda b, h, c: (tok(b, h, c), nhb + h)),
        pl.BlockSpec((q, 2 * hb * hd), lambda b, h, c: (tok(b, h, c), nhb + h)),
        pl.BlockSpec((q, 2 * hb * hd), lambda b, h, c: (tok(b, h, c), 2 * nhb + h)),
        pl.BlockSpec((q, LANES), lambda b, h, c: (tok(b, h, c), h)),
        pl.BlockSpec((CONV_W, hb * hd), lambda b, h, c: (0, h)),
        pl.BlockSpec((CONV_W, hb * hd), lambda b, h, c: (0, nhb + h)),
        pl.BlockSpec((CONV_W, 2 * hb * hd), lambda b, h, c: (0, nhb + h)),
        pl.BlockSpec((1, LANES), lambda b, h, c: (0, h)),
        pl.BlockSpec((1, LANES), lambda b, h, c: (0, h)),
        pl.BlockSpec((1, hd), lambda b, h, c: (0, 0)),
    ]
    o, st = pl.pallas_call(
        functools.partial(_dn_prompt_kernel, hb=hb, hv=2 * hk),
        grid=(batch, nhb, nc),
        in_specs=in_specs,
        out_specs=[pl.BlockSpec((q, 2 * hb * hd), lambda b, h, c: (tok(b, h, c), h)),
                   pl.BlockSpec((1, 2 * hb, hd, hd), lambda b, h, c: (b, h, 0, 0))],
        out_shape=[jax.ShapeDtypeStruct((batch * seq, 2 * hk * hd), BF16),
                   jax.ShapeDtypeStruct((batch, 2 * hk, hd, hd), F32)],
        scratch_shapes=[pltpu.VMEM((2 * hb, hd, hd), F32),
                        pltpu.VMEM((q + SUBLANES, 4 * hb * hd), F32)],
        compiler_params=_cparams(("parallel", "parallel", "arbitrary")),
        name="dn_prompt",
    )(proj, proj, proj, proj, ba, w["conv_w"], w["conv_w"], w["conv_w"], w["a_log"], w["dt_bias"], w["norm"])
    return o, st


def _rotate(x, cos, s1, s2):
    w = x.shape[1]
    return x * cos + pltpu.roll(x, w - 1, 1) * s1 + pltpu.roll(x, 1, 1) * s2


def _ret_prompt_kernel(q_ref, k_ref, v_ref, g_ref, cos_ref, s1_ref, s2_ref, lg_ref, y_ref, st_ref, s_scr, *, hb):
    c = pl.program_id(2)
    q = q_ref.shape[0]

    @pl.when(c == 0)
    def _():
        s_scr[...] = jnp.zeros_like(s_scr)

    tile = lambda t: t if hb == 1 else jnp.concatenate([t] * hb, axis=1)
    cos, s1, s2 = tile(cos_ref[...]), tile(s1_ref[...]), tile(s2_ref[...])
    qa = _rotate(q_ref[...], cos, s1, s2)
    ka = _rotate(k_ref[...], cos, s1, s2) * (RET_DK ** -0.5)
    q3 = jnp.stack([qa[:, RET_DK * h:RET_DK * (h + 1)] for h in range(hb)])
    k3 = jnp.stack([ka[:, RET_DK * h:RET_DK * (h + 1)] for h in range(hb)])
    v3 = jnp.stack([v_ref[:, RET_DV * h:RET_DV * (h + 1)] for h in range(hb)])
    lg = lg_ref[...][:, 0:1, 0:1]
    row = lax.broadcasted_iota(jnp.int32, (q, q), 0)
    col = lax.broadcasted_iota(jnp.int32, (q, q), 1)
    dist = (row - col).astype(F32)[None]
    decay = jnp.exp(jnp.where((row >= col)[None], dist * lg, NEG_BIG))
    pos = lax.broadcasted_iota(jnp.int32, (1, q, 1), 1).astype(F32)
    scores = _bdot_nt(q3, k3)
    s_old = s_scr[...]
    y = _bdot(scores * decay, v3) + _bdot(q3, s_old) * jnp.exp((pos + 1.0) * lg)
    wv = jnp.exp((float(q - 1) - pos) * lg) * v3
    k_t = jnp.stack([jnp.concatenate([_tr(k3[h][:, LANES * i:LANES * (i + 1)]) for i in range(RET_DK // LANES)],
                                     axis=0) for h in range(hb)])
    s_new = s_old * jnp.exp(float(q) * lg) + _bdot(k_t, wv)
    s_scr[...] = s_new
    yn = _rms(y)
    for h in range(hb):
        hs = slice(RET_DV * h, RET_DV * (h + 1))
        y_ref[:, hs] = (yn[h] * _silu(g_ref[:, hs])).astype(BF16)

    @pl.when(c == pl.num_programs(2) - 1)
    def _():
        st_ref[0] = s_new


def _ret_prompt(proj, rope, lg, nheads, batch, seq):
    q = _pick(seq, (RET_CHUNK, 64, 32, 16, 8))
    hb = _pick(nheads, (RET_HEADS_PER_STEP, 2, 1))
    nc = seq // q
    nhb = nheads // hb
    tok = lambda b, h, c: b * nc + c
    v0 = 2 * nheads * RET_DK // (hb * RET_DV)
    g0 = v0 + nhb
    cos, s1, s2 = rope
    in_specs = [
        pl.BlockSpec((q, hb * RET_DK), lambda b, h, c: (tok(b, h, c), h)),
        pl.BlockSpec((q, hb * RET_DK), lambda b, h, c: (tok(b, h, c), nhb + h)),
        pl.BlockSpec((q, hb * RET_DV), lambda b, h, c: (tok(b, h, c), v0 + h)),
        pl.BlockSpec((q, hb * RET_DV), lambda b, h, c: (tok(b, h, c), g0 + h)),
        pl.BlockSpec((q, RET_DK), lambda b, h, c: (c, 0)),
        pl.BlockSpec((q, RET_DK), lambda b, h, c: (c, 0)),
        pl.BlockSpec((q, RET_DK), lambda b, h, c: (c, 0)),
        pl.BlockSpec((hb, SUBLANES, LANES), lambda b, h, c: (h, 0, 0)),
    ]
    y, st = pl.pallas_call(
        functools.partial(_ret_prompt_kernel, hb=hb),
        grid=(batch, nhb, nc),
        in_specs=in_specs,
        out_specs=[pl.BlockSpec((q, hb * RET_DV), lambda b, h, c: (tok(b, h, c), h)),
                   pl.BlockSpec((1, hb, RET_DK, RET_DV), lambda b, h, c: (b, h, 0, 0))],
        out_shape=[jax.ShapeDtypeStruct((batch * seq, nheads * RET_DV), BF16),
                   jax.ShapeDtypeStruct((batch, nheads, RET_DK, RET_DV), F32)],
        scratch_shapes=[pltpu.VMEM((hb, RET_DK, RET_DV), F32)],
        compiler_params=_cparams(("parallel", "parallel", "arbitrary")),
        name="ret_prompt",
    )(proj, proj, proj, proj, cos, s1, s2, lg)
    return y, st


def _alias_args(prev, n_in, out_idx):
    if prev is None:
        return [], [], {}
    return [pl.BlockSpec(memory_space=pl.ANY)], [prev], {n_in: out_idx}


def _conv_step_kernel(*refs, has_bias, aliased):
    refs = list(refs)
    x_ref, cs_ref, w_ref = refs[:3]
    b_ref = refs[3] if has_bias else None
    o_ref, cn_ref = refs[3 + int(has_bias) + int(aliased):]
    w = w_ref[...]
    x = x_ref[...]
    c0, c1, c2 = cs_ref[0, 0], cs_ref[0, 1], cs_ref[0, 2]
    acc = c0 * w[0:1, :] + c1 * w[1:2, :] + c2 * w[2:3, :] + x * w[3:4, :]
    if has_bias:
        acc = acc + b_ref[...]
    o_ref[...] = _silu(acc)
    cn_ref[0, 0] = c1
    cn_ref[0, 1] = c2
    cn_ref[0, 2] = x


def _conv_step(proj, col0, cstate, layer, conv_w, conv_b, prev):
    m = proj.shape[0]
    cdim = cstate.shape[3]
    cb = _pick(cdim, (512, 256, 128))
    x0 = col0 // cb
    taps = CONV_W - 1
    in_specs = [pl.BlockSpec((m, cb), lambda j: (0, x0 + j)),
                pl.BlockSpec((1, taps, m, cb), lambda j: (layer, 0, 0, j)),
                pl.BlockSpec((CONV_W, cb), lambda j: (0, j))]
    args = [proj, cstate, conv_w]
    if conv_b is not None:
        in_specs.append(pl.BlockSpec((1, cb), lambda j: (0, j)))
        args.append(conv_b)
    a_specs, a_args, aliases = _alias_args(prev, len(args), 1)
    return pl.pallas_call(
        functools.partial(_conv_step_kernel, has_bias=conv_b is not None, aliased=prev is not None),
        grid=(cdim // cb,),
        in_specs=in_specs + a_specs,
        out_specs=[pl.BlockSpec((m, cb), lambda j: (0, j)),
                   pl.BlockSpec((1, taps, m, cb), lambda j: (layer, 0, 0, j))],
        out_shape=[jax.ShapeDtypeStruct((m, cdim), F32),
                   jax.ShapeDtypeStruct(cstate.shape, F32)],
        input_output_aliases=aliases,
        compiler_params=_cparams(("parallel",)),
        name="conv_step",
    )(*args, *a_args)


def _ssd_step_kernel(*refs, nheads):
    z_ref, xs_ref, b_ref, c_ref, dt_ref, dtb_ref, alog_ref, d_ref, nw_ref, st_ref = refs[:10]
    y_ref, so_ref, ytb = refs[-3:]
    nb = xs_ref.shape[0]
    hd = SSM_HEAD_DIM
    xs = xs_ref[...]
    dtv = _softplus(dt_ref[...] + dtb_ref[...])
    decay = jnp.exp(dtv * (-jnp.exp(alog_ref[...])))
    pad = jnp.zeros((LANES - nb, LANES), F32)
    kmat = jnp.concatenate([b_ref[...], pad], axis=0).astype(BF16)
    qmat = jnp.concatenate([c_ref[...], pad], axis=0).T.astype(BF16)
    v_all = xs * _expand64(dtv, nheads)
    v_t = [jnp.concatenate([v_all[:, LANES * j:LANES * (j + 1)], pad], axis=0).T
           for j in range(nheads // 2)]
    lane = lax.broadcasted_iota(jnp.int32, (hd, LANES), 1)
    for r in range(nheads):
        vt = v_t[r // 2][hd * (r % 2):hd * (r % 2 + 1), :]
        for bi in range(nb):
            outer = jnp.dot(jnp.where(lane == bi, vt, 0.0).astype(BF16), kmat, preferred_element_type=F32)
            so_ref[0, bi, r] = st_ref[0, bi, r] * decay[bi:bi + 1, r:r + 1] + outer
    for r in range(nheads):
        acc = jnp.zeros((hd, LANES), F32)
        for bi in range(nb):
            yb = jnp.dot(so_ref[0, bi, r].astype(BF16), qmat, preferred_element_type=F32)
            acc = jnp.where(lane == bi, yb, acc)
        ytb[hd * r:hd * (r + 1), :] = acc
    y = [ytb[LANES * j:LANES * (j + 1), :].T[:nb, :] for j in range(nheads // 2)]
    y = y[0] if len(y) == 1 else jnp.concatenate(y, axis=1)
    y = y + xs * _expand64(d_ref[...], nheads)
    y = y * _silu(z_ref[...])
    y_ref[...] = (_rms(y) * nw_ref[...]).astype(BF16)


def _ssd_step(proj, xbc, dt, w, state, layer, prev):
    m = proj.shape[0]
    d_inner, nh, g, n = w["d_inner"], w["hpg"], SSM_GROUPS, SSM_D_STATE
    width = nh * SSM_HEAD_DIM
    nb = _pick(m, (2 * STEP_BATCH, STEP_BATCH))
    b0 = d_inner // n
    st_spec = pl.BlockSpec((1, nb, nh, SSM_HEAD_DIM, n), lambda i, gi: (layer, i, gi, 0, 0))
    in_specs = [
        pl.BlockSpec((nb, width), lambda i, gi: (i, gi)),
        pl.BlockSpec((nb, width), lambda i, gi: (i, gi)),
        pl.BlockSpec((nb, n), lambda i, gi: (i, b0 + gi)),
        pl.BlockSpec((nb, n), lambda i, gi: (i, b0 + g + gi)),
        pl.BlockSpec((nb, LANES), lambda i, gi: (i, gi)),
        pl.BlockSpec((1, LANES), lambda i, gi: (0, gi)),
        pl.BlockSpec((1, LANES), lambda i, gi: (0, gi)),
        pl.BlockSpec((1, LANES), lambda i, gi: (0, gi)),
        pl.BlockSpec((1, width), lambda i, gi: (0, gi)),
        st_spec,
    ]
    args = [proj, xbc, xbc, xbc, dt, w["dt_bias"], w["a_log"], w["d"], w["norm"], state]
    a_specs, a_args, aliases = _alias_args(prev, len(args), 1)
    y, st = pl.pallas_call(
        functools.partial(_ssd_step_kernel, nheads=nh),
        grid=(m // nb, g),
        in_specs=in_specs + a_specs,
        out_specs=[pl.BlockSpec((nb, width), lambda i, gi: (i, gi)), st_spec],
        out_shape=[jax.ShapeDtypeStruct((m, d_inner), BF16),
                   jax.ShapeDtypeStruct(state.shape, F32)],
        scratch_shapes=[pltpu.VMEM((width, LANES), F32)],
        input_output_aliases=aliases,
        compiler_params=_cparams(("parallel", "parallel")),
        name="ssd_step",
    )(*args, *a_args)
    return y, st


def _dn_step_kernel(q_ref, k_ref, v_ref, z_ref, ba_ref, alog_ref, dtb_ref, nw_ref, st_ref,
                    o_ref, so_ref, obuf, *, hv):
    nb = q_ref.shape[0]
    hd = DN_HEAD
    qq = q_ref[...]
    kk = k_ref[...]
    qq = qq * lax.rsqrt(jnp.sum(qq * qq, axis=-1, keepdims=True) + NORM_EPS) * (hd ** -0.5)
    kk = kk * lax.rsqrt(jnp.sum(kk * kk, axis=-1, keepdims=True) + NORM_EPS)
    vv = v_ref[...]
    ba = ba_ref[...]
    beta = _sigmoid(ba)
    eg = jnp.exp(-jnp.exp(alog_ref[...]) * _softplus(ba + dtb_ref[...]))
    k_t = _tr(kk)
    q_t = _tr(qq)
    for bi in range(nb):
        kcol = k_t[:, bi:bi + 1]
        qcol = q_t[:, bi:bi + 1]
        for j in range(2):
            s_old = st_ref[0, bi, j]
            b = beta[bi:bi + 1, j:j + 1]
            e = eg[bi:bi + 1, hv + j:hv + j + 1]
            ks = jnp.sum(kcol * s_old, axis=0, keepdims=True)
            u = b * vv[bi:bi + 1, hd * j:hd * (j + 1)] - (b * e) * ks
            s_new = s_old * e + kcol * u
            so_ref[0, bi, j] = s_new
            obuf[bi:bi + 1, hd * j:hd * (j + 1)] = jnp.sum(qcol * s_new, axis=0, keepdims=True)
    for j in range(2):
        o = obuf[:, hd * j:hd * (j + 1)]
        on = _rms(o) * nw_ref[...]
        o_ref[:, hd * j:hd * (j + 1)] = (on * _silu(z_ref[:, hd * j:hd * (j + 1)])).astype(BF16)


def _dn_step(proj, qkv, ba, w, state, layer):
    m = proj.shape[0]
    hk, hd = w["hk"], DN_HEAD
    nb = _pick(m, (2 * STEP_BATCH, STEP_BATCH))
    st_spec = pl.BlockSpec((1, nb, 2, hd, hd), lambda i, h: (layer, i, h, 0, 0))
    in_specs = [
        pl.BlockSpec((nb, hd), lambda i, h: (i, h)),
        pl.BlockSpec((nb, hd), lambda i, h: (i, hk + h)),
        pl.BlockSpec((nb, 2 * hd), lambda i, h: (i, hk + h)),
        pl.BlockSpec((nb, 2 * hd), lambda i, h: (i, 2 * hk + h)),
        pl.BlockSpec((nb, LANES), lambda i, h: (i, h)),
        pl.BlockSpec((1, LANES), lambda i, h: (0, h)),
        pl.BlockSpec((1, LANES), lambda i, h: (0, h)),
        pl.BlockSpec((1, hd), lambda i, h: (0, 0)),
        st_spec,
    ]
    o, st = pl.pallas_call(
        functools.partial(_dn_step_kernel, hv=2 * hk),
        grid=(m // nb, hk),
        in_specs=in_specs,
        out_specs=[pl.BlockSpec((nb, 2 * hd), lambda i, h: (i, h)), st_spec],
        out_shape=[jax.ShapeDtypeStruct((m, 2 * hk * hd), BF16),
                   jax.ShapeDtypeStruct(state.shape, F32)],
        scratch_shapes=[pltpu.VMEM((nb, 2 * hd), F32)],
        compiler_params=_cparams(("parallel", "parallel")),
        name="dn_step",
    )(qkv, qkv, qkv, proj, ba, w["a_log_step"], w["dt_bias_step"], w["norm"], state)
    return o, st


def _ret_step_kernel(q_ref, k_ref, v_ref, g_ref, cos_ref, s1_ref, s2_ref, lg_ref, st_ref, y_ref, so_ref, ybuf):
    nb = q_ref.shape[0]
    cos, s1, s2 = cos_ref[...], s1_ref[...], s2_ref[...]
    qq = _rotate(q_ref[...], cos, s1, s2)
    kk = _rotate(k_ref[...], cos, s1, s2) * (RET_DK ** -0.5)
    vv = v_ref[...]
    gamma = jnp.exp(lg_ref[0][0:1, 0:1])
    nk = RET_DK // LANES
    k_t = jnp.concatenate([_tr(kk[:, LANES * i:LANES * (i + 1)]) for i in range(nk)], axis=0)
    q_t = jnp.concatenate([_tr(qq[:, LANES * i:LANES * (i + 1)]) for i in range(nk)], axis=0)
    for bi in range(nb):
        s_new = st_ref[0, bi, 0] * gamma + k_t[:, bi:bi + 1] * vv[bi:bi + 1, :]
        so_ref[0, bi, 0] = s_new
        ybuf[bi:bi + 1, :] = jnp.sum(q_t[:, bi:bi + 1] * s_new, axis=0, keepdims=True)
    y_ref[...] = (_rms(ybuf[...]) * _silu(g_ref[...])).astype(BF16)


def _ret_step(proj, rope, lg, nheads, state, layer):
    m = proj.shape[0]
    nb = _pick(m, (STEP_BATCH,))
    v0 = 2 * nheads * RET_DK // RET_DV
    g0 = v0 + nheads
    cos, s1, s2 = rope
    st_spec = pl.BlockSpec((1, nb, 1, RET_DK, RET_DV), lambda i, h: (layer, i, h, 0, 0))
    in_specs = [
        pl.BlockSpec((nb, RET_DK), lambda i, h: (i, h)),
        pl.BlockSpec((nb, RET_DK), lambda i, h: (i, nheads + h)),
        pl.BlockSpec((nb, RET_DV), lambda i, h: (i, v0 + h)),
        pl.BlockSpec((nb, RET_DV), lambda i, h: (i, g0 + h)),
        pl.BlockSpec((1, RET_DK), lambda i, h: (0, 0)),
        pl.BlockSpec((1, RET_DK), lambda i, h: (0, 0)),
        pl.BlockSpec((1, RET_DK), lambda i, h: (0, 0)),
        pl.BlockSpec((1, SUBLANES, LANES), lambda i, h: (h, 0, 0)),
        st_spec,
    ]
    y, st = pl.pallas_call(
        _ret_step_kernel,
        grid=(m // nb, nheads),
        in_specs=in_specs,
        out_specs=[pl.BlockSpec((nb, RET_DV), lambda i, h: (i, h)), st_spec],
        out_shape=[jax.ShapeDtypeStruct((m, nheads * RET_DV), BF16),
                   jax.ShapeDtypeStruct(state.shape, F32)],
        scratch_shapes=[pltpu.VMEM((nb, RET_DV), F32)],
        compiler_params=_cparams(("parallel", "parallel")),
        name="ret_step",
    )(proj, proj, proj, proj, cos, s1, s2, lg, state)
    return y, st


def _pad_lanes(x):
    return jnp.pad(x, [(0, 0)] * (x.ndim - 1) + [(0, LANES - x.shape[-1])])


def _prep_ssm(conv_w, conv_b, dt_bias, a_log, d_skip, norm_w):
    heads = dt_bias.shape[0]
    g = SSM_GROUPS
    hpg = heads // g
    d_inner = heads * SSM_HEAD_DIM
    conv_dim = conv_w.shape[1]
    main = d_inner + conv_dim
    per_group = lambda v: _pad_lanes(v.reshape(g, hpg)).reshape(1, g * LANES)
    return dict(d_inner=d_inner, hpg=hpg, conv_dim=conv_dim, main=main,
                conv_w=conv_w, conv_b=conv_b.reshape(1, conv_dim),
                dt_bias=per_group(dt_bias), a_log=per_group(a_log), d=per_group(d_skip),
                norm=norm_w.reshape(1, d_inner))


def _prep_dn(conv_w, a_log, dt_bias, norm_w):
    hv = a_log.shape[0]
    hk = hv // 2
    hb = _pick(hk, (DN_HEADS_PER_STEP, 2, 1))
    conv_dim = conv_w.shape[1]
    main = conv_dim + hv * DN_HEAD

    def table(v, heads_per_step):
        t = v.reshape(hk // heads_per_step, 2 * heads_per_step)
        t = jnp.pad(t, ((0, 0), (hv, LANES - hv - 2 * heads_per_step)))
        return t.reshape(1, -1)

    return dict(hk=hk, hb=hb, conv_dim=conv_dim, main=main, conv_w=conv_w,
                a_log=table(a_log, hb), dt_bias=table(dt_bias, hb),
                a_log_step=table(a_log, 1), dt_bias_step=table(dt_bias, 1),
                norm=norm_w.reshape(1, DN_HEAD))


def _rope_tables(pos):
    half = RET_DK // 2
    inv = 1.0 / (RET_ROPE_BASE ** jnp.linspace(0.0, 1.0, half, dtype=F32))
    ang = pos.astype(F32)[:, None] * inv[None, :]
    cos, sin, zero = jnp.cos(ang), jnp.sin(ang), jnp.zeros_like(ang)
    inter = lambda a, b: jnp.stack([a, b], axis=-1).reshape(pos.shape[0], RET_DK)
    return inter(cos, cos), inter(-sin, zero), inter(zero, sin)


def _forward(x, p, prompt, states, prm, wts):
    batch, seq, d_model = x.shape
    m = batch * seq
    h = x.reshape(m, d_model)
    p = p.reshape(p.shape[0], m, p.shape[-1])
    depth = prm["norm_mix_pre"].shape[0]
    ssm_s, ssm_c, dn_s, dn_c, ret_s = states
    o_ssm = o_ssm_c = o_dn = o_dn_c = o_ret = None
    p_ssm, p_ssm_c, p_dn, p_dn_c, p_ret = [], [], [], [], []
    tail = slice(seq - (CONV_W - 1), seq)
    xn = _norm(h, prm["norm_mix_pre"][0])
    for i in range(depth):
        kind, j = i % 3, i // 3
        if kind == 0:
            w = wts["ssm"][j]
            d_inner, conv_dim = w["d_inner"], w["conv_dim"]
            proj = _mm(xn, wts["ssm_in_t"], j, w["main"], True)
            dt = _mm_small(xn, wts["ssm_in_t"], j, w["main"], SSM_GROUPS * w["hpg"], SSM_GROUPS, w["hpg"])
            if prompt:
                y, st = _ssd_prompt(proj, dt, w, batch, seq)
                p_ssm.append(st)
                p_ssm_c.append(proj.reshape(batch, seq, -1)[:, tail, d_inner:d_inner + conv_dim])
            else:
                xbc, o_ssm_c = _conv_step(proj, d_inner, ssm_c, j, w["conv_w"], w["conv_b"], o_ssm_c)
                y, o_ssm = _ssd_step(proj, xbc, dt, w, ssm_s, j, o_ssm)
            w_out = wts["ssm_out"]
        elif kind == 1:
            w = wts["dn"][j]
            conv_dim = w["conv_dim"]
            proj = _mm(xn, wts["dn_in_t"], j, w["main"], True)
            nblk, width = (w["hk"] // w["hb"], 2 * w["hb"]) if prompt else (w["hk"], 2)
            ba = _mm_small(xn, wts["dn_in_t"], j, w["main"], 4 * w["hk"], nblk, width)
            if prompt:
                y, st = _dn_prompt(proj, ba, w, batch, seq)
                p_dn.append(st)
                p_dn_c.append(proj.reshape(batch, seq, -1)[:, tail, :conv_dim])
            else:
                qkv, o_dn_c = _conv_step(proj, 0, dn_c, j, w["conv_w"], None, o_dn_c)
                y, o_dn = _dn_step(proj, qkv, ba, w, dn_s, j)
            w_out = wts["dn_out"]
        else:
            nheads = wts["ret_heads"]
            proj = _mm(xn, wts["ret_in"], j, wts["ret_in"].shape[2], False)
            if prompt:
                y, st = _ret_prompt(proj, wts["rope"], wts["lg"], nheads, batch, seq)
                p_ret.append(st)
            else:
                y, o_ret = _ret_step(proj, wts["rope"], wts["lg"], nheads, ret_s, j)
            w_out = wts["ret_out"]
        h, xn = _mm_out(y, w_out, j, h, prm["norm_mix_post"][i], prm["norm_ffn_pre"][i])
        act = _ffn_in(xn, wts["ffn_gate"], wts["ffn_up"], i)
        h, xn = _mm_out(act, wts["ffn_down"], i, h, prm["norm_ffn_post"][i], prm["norm_ple"][i])
        gain_next = prm["norm_mix_pre"][i + 1] if i + 1 < depth else None
        h, xn = _ple(h, xn, wts["ple_gate"], p, wts["ple_proj"], i, gain_next)
    h = h.reshape(batch, seq, d_model)
    if prompt:
        return (h, jnp.swapaxes(jnp.stack(p_ssm), -1, -2), jnp.stack(p_ssm_c), jnp.stack(p_dn),
                jnp.stack(p_dn_c), jnp.stack(p_ret))
    return (h, jnp.swapaxes(o_ssm, -1, -2), jnp.swapaxes(o_ssm_c, 1, 2), o_dn, jnp.swapaxes(o_dn_c, 1, 2), o_ret)


def kernel(x_prompt, x_sample, state_ssm, state_ssm_conv, state_delta, state_delta_conv, state_ret, p_prompt, p_sample, norm_mix_pre, norm_mix_post, norm_ffn_pre, norm_ffn_post, norm_ple, ffn_w_gate, ffn_w_up, ffn_w_down, ple_w_proj, ple_w_gate, ssm_w_in, ssm_conv_w, ssm_conv_b, ssm_dt_bias, ssm_a_log, ssm_d, ssm_norm, ssm_w_out, dn_w_in, dn_conv_w, dn_a_log, dn_dt_bias, dn_norm, dn_w_out, ret_w_in, ret_w_out):
    assert x_sample.shape[1] == 1, "the sample group advances one token per sequence"
    prm = dict(norm_mix_pre=norm_mix_pre, norm_mix_post=norm_mix_post, norm_ffn_pre=norm_ffn_pre,
               norm_ffn_post=norm_ffn_post, norm_ple=norm_ple)
    ret_heads = ret_w_out.shape[1] // RET_DV
    log_gamma = jnp.log(1.0 - 2.0 ** (-5.0 - jnp.arange(ret_heads, dtype=F32)))
    ssm_in_t, dn_in_t = jnp.swapaxes(ssm_w_in, 1, 2), jnp.swapaxes(dn_w_in, 1, 2)
    base = dict(
        ssm=[_prep_ssm(ssm_conv_w[j], ssm_conv_b[j], ssm_dt_bias[j], ssm_a_log[j], ssm_d[j],
                       ssm_norm[j]) for j in range(ssm_w_in.shape[0])],
        dn=[_prep_dn(dn_conv_w[j], dn_a_log[j], dn_dt_bias[j], dn_norm[j])
            for j in range(dn_w_in.shape[0])],
        ssm_in_t=ssm_in_t, dn_in_t=dn_in_t,
        ssm_out=ssm_w_out.astype(BF16), dn_out=dn_w_out.astype(BF16),
        ret_in=ret_w_in, ret_out=ret_w_out.astype(BF16), ret_heads=ret_heads,
        ffn_gate=ffn_w_gate, ffn_up=ffn_w_up, ffn_down=ffn_w_down.astype(BF16),
        ple_gate=ple_w_gate.astype(BF16), ple_proj=ple_w_proj.astype(BF16),
        lg=jnp.broadcast_to(log_gamma[:, None, None], (ret_heads, SUBLANES, LANES)),
    )
    seq = x_prompt.shape[1]
    pos_prompt = jnp.arange(seq, dtype=jnp.int32)
    pos_sample = PAST_LEN + jnp.arange(1, dtype=jnp.int32)
    out_p = _forward(x_prompt, p_prompt, True, (None,) * 5, prm, dict(base, rope=_rope_tables(pos_prompt)))
    states = (jnp.swapaxes(state_ssm, -1, -2), jnp.swapaxes(state_ssm_conv, 1, 2), state_delta,
              jnp.swapaxes(state_delta_conv, 1, 2), state_ret)
    out_s = _forward(x_sample, p_sample, False, states, prm, dict(base, rope=_rope_tables(pos_sample)))
    return (out_p[0], out_s[0]) + out_p[1:] + out_s[1:]
```
